```python
import math
import jax, jax.numpy as jnp
from jax import lax
import numpy as np


D_MODEL = 1024
BATCH = 8
SEQ = 8192
DEPTH = 1

PLE_DIM = 256
NSA_HEADS = 8
NSA_KV_GROUPS = 2
NSA_REP = NSA_HEADS // NSA_KV_GROUPS
NSA_DH = 64
NSA_WIDTH = NSA_HEADS * NSA_DH
NSA_KV = NSA_KV_GROUPS * NSA_DH
CMP_LEN = 32
CMP_STRIDE = 16
CMP_HIDDEN = 256
SEL_BLOCK = 64
TOP_N = 16
N_LOCAL = 2
WINDOW = 512
Q_BLOCK = 128
GLA_HEADS = 4
GLA_DK = 64
GLA_DV = 128
GLA_WIDTH = GLA_HEADS * GLA_DV
GLA_RANK = 16
GLA_TAU = 16.0
GLA_CHUNK = 64
MIX_WIDTH = NSA_WIDTH + GLA_WIDTH
NUM_BUCKETS = 32
MAX_DISTANCE = 128
ALPHA = (2.0 * DEPTH) ** 0.25
BETA = (8.0 * DEPTH) ** -0.25
EPS = 1e-5
NEG_INF = -1e30
POS_INF = 1e30
IN_WIDTHS = (NSA_WIDTH, NSA_KV, NSA_KV, NSA_KV, NSA_KV, NSA_KV, NSA_KV, 3 * NSA_HEADS, NSA_WIDTH,
             GLA_HEADS * GLA_DK, GLA_HEADS * GLA_DK, GLA_WIDTH, GLA_RANK, GLA_WIDTH)
IN_SPLIT_POINTS = tuple(int(v) for v in np.cumsum(IN_WIDTHS)[:-1])
D_IN = sum(IN_WIDTHS)

kernel_name = 'hymba_nsa_gla_deepnorm'


def layer_norm(x, g, b):
    xf = x.astype(jnp.float32)
    mu = jnp.mean(xf, axis=-1, keepdims=True)
    var = jnp.mean(jnp.square(xf - mu), axis=-1, keepdims=True)
    return (xf - mu) * lax.rsqrt(var + EPS) * g + b


def rms_norm(x, g):
    xf = x.astype(jnp.float32)
    return xf * lax.rsqrt(jnp.mean(jnp.square(xf), axis=-1, keepdims=True) + EPS) * g


def t5_bucket(dist):
    n = jnp.maximum(dist, 0)
    max_exact = NUM_BUCKETS // 2
    nf = jnp.maximum(n, 1).astype(jnp.float32)
    large = max_exact + (jnp.log(nf / max_exact) / math.log(MAX_DISTANCE / max_exact)
                         * (NUM_BUCKETS - max_exact)).astype(jnp.int32)
    large = jnp.minimum(large, NUM_BUCKETS - 1)
    return jnp.where(n < max_exact, n, large)


def compress_blocks(kv, pos, w1, b1, w2):
    B, S, G, DH = kv.shape
    ch = kv.transpose(0, 2, 1, 3).reshape(B, G, S // CMP_STRIDE, CMP_STRIDE, DH)
    blocks = jnp.concatenate([ch[:, :, :-1], ch[:, :, 1:]], axis=3) + pos
    nc = blocks.shape[2]
    h = jax.nn.gelu(blocks.reshape(B, G, nc, CMP_LEN * DH) @ w1 + b1)
    return h @ w2


def nsa_attention(q, kc, vc, ks, vs, kw, vw, gates, rel_bias):
    B, S = q.shape[0], q.shape[1]
    G, R, DH = NSA_KV_GROUPS, NSA_REP, NSA_DH
    qg = q.reshape(B, S, G, R, DH).transpose(0, 2, 3, 1, 4)
    nc = kc.shape[2]
    nsel = S // SEL_BLOCK
    top_n = min(TOP_N, nsel)
    ks_blk = ks.transpose(0, 2, 1, 3).reshape(B, G, nsel, SEL_BLOCK, DH)
    vs_blk = vs.transpose(0, 2, 1, 3).reshape(B, G, nsel, SEL_BLOCK, DH)
    pad = ((0, 0), (0, 0), (WINDOW, 0), (0, 0))
    kw_pad = jnp.pad(kw.transpose(0, 2, 1, 3), pad)
    vw_pad = jnp.pad(vw.transpose(0, 2, 1, 3), pad)
    rb = rel_bias.reshape(NUM_BUCKETS, G, R)
    scale = DH ** -0.5
    cmp_end = jnp.arange(nc) * CMP_STRIDE + CMP_LEN - 1
    c_start = jnp.arange(nc) * CMP_STRIDE
    b_start = jnp.arange(nsel) * SEL_BLOCK
    overlap = jnp.clip(jnp.minimum(c_start[:, None] + CMP_LEN, b_start[None, :] + SEL_BLOCK)
                       - jnp.maximum(c_start[:, None], b_start[None, :]), 0, None).astype(jnp.float32) / CMP_LEN
    c_win = jnp.arange(Q_BLOCK + WINDOW)
    dist_w = jnp.arange(Q_BLOCK)[:, None] + WINDOW - c_win[None, :]
    band = (dist_w >= 0) & (dist_w < WINDOW)
    bias_w = rb[t5_bucket(dist_w)].transpose(2, 3, 0, 1)
    blk = jnp.arange(nsel)
    g_idx = jnp.arange(G)[None, :, None, None]
    gather = jax.vmap(jax.vmap(lambda blocks, ix: blocks[ix]))

    def block(i):
        t0 = i * Q_BLOCK
        t = t0 + jnp.arange(Q_BLOCK)
        qb = lax.dynamic_slice_in_dim(qg, t0, Q_BLOCK, axis=3)
        valid_c = cmp_end[None, :] <= t[:, None]
        bias_c = rb[t5_bucket(t[:, None] - cmp_end[None, :])].transpose(2, 3, 0, 1)
        s_c = jnp.einsum('bgrqd,bgcd->bgrqc', qb, kc).astype(jnp.float32) * scale + bias_c
        p_c = jnp.where(valid_c, jax.nn.softmax(jnp.where(valid_c, s_c, NEG_INF), axis=-1), 0.0)
        o_c = jnp.einsum('bgrqc,bgcd->bgrqd', p_c, vc)
        imp = jnp.einsum('bgrqc,cn->bgqn', p_c, overlap)
        cur = (t // SEL_BLOCK)[:, None]
        causal_b = blk[None, :] <= cur
        forced = causal_b & ((blk[None, :] == 0) | (blk[None, :] >= cur - (N_LOCAL - 1)))
        imp = jnp.where(forced, POS_INF, jnp.where(causal_b, imp, NEG_INF))
        _, idx = lax.top_k(imp, top_n)
        k_sel = gather(ks_blk, idx).reshape(B, G, Q_BLOCK, top_n * SEL_BLOCK, DH)
        v_sel = gather(vs_blk, idx).reshape(B, G, Q_BLOCK, top_n * SEL_BLOCK, DH)
        pos = (idx[..., None] * SEL_BLOCK + jnp.arange(SEL_BLOCK)).reshape(B, G, Q_BLOCK, top_n * SEL_BLOCK)
        dist_s = t[:, None] - pos
        bias_s = jnp.moveaxis(rb[t5_bucket(dist_s), g_idx], -1, 2)
        s_s = jnp.einsum('bgrqd,bgqmd->bgrqm', qb, k_sel).astype(jnp.float32) * scale + bias_s
        p_s = jax.nn.softmax(jnp.where(dist_s[:, :, None] >= 0, s_s, NEG_INF), axis=-1)
        o_s = jnp.einsum('bgrqm,bgqmd->bgrqd', p_s, v_sel)
        kwb = lax.dynamic_slice_in_dim(kw_pad, t0, Q_BLOCK + WINDOW, axis=2)
        vwb = lax.dynamic_slice_in_dim(vw_pad, t0, Q_BLOCK + WINDOW, axis=2)
        valid_w = band & ((t0 - WINDOW + c_win) >= 0)[None, :]
        s_w = jnp.einsum('bgrqd,bgkd->bgrqk', qb, kwb).astype(jnp.float32) * scale + bias_w
        p_w = jax.nn.softmax(jnp.where(valid_w, s_w, NEG_INF), axis=-1)
        o_w = jnp.einsum('bgrqk,bgkd->bgrqd', p_w, vwb)
        gb = lax.dynamic_slice_in_dim(gates, t0, Q_BLOCK, axis=1)
        gb = gb.reshape(B, Q_BLOCK, G, R, 3).transpose(0, 2, 3, 1, 4)
        return gb[..., 0:1] * o_c + gb[..., 1:2] * o_s + gb[..., 2:3] * o_w

    out = lax.map(block, jnp.arange(S // Q_BLOCK))
    return out.transpose(1, 0, 4, 2, 3, 5).reshape(B, S, G * R * DH)


def gla_chunked(q, k, v, log_a):
    B, S, H, DK = q.shape
    DV = v.shape[-1]
    C = GLA_CHUNK
    N = S // C

    def to_chunks(a):
        return a.astype(jnp.float32).reshape(B, N, C, H, a.shape[-1]).transpose(0, 3, 1, 2, 4)

    q, k, v, log_a = to_chunks(q), to_chunks(k), to_chunks(v), to_chunks(log_a)
    b = jnp.cumsum(log_a, axis=3)
    b_last = b[:, :, :, -1:, :]
    q_e = q * (DK ** -0.5) * jnp.exp(b)
    k_e = k * jnp.exp(-b)
    k_last = k * jnp.exp(b_last - b)
    causal = jnp.tril(jnp.ones((C, C), dtype=bool))
    attn = jnp.where(causal, jnp.einsum('bhncd,bhnsd->bhncs', q_e, k_e), 0.0)
    o_intra = jnp.einsum('bhncs,bhnsv->bhncv', attn, v)
    kv = jnp.einsum('bhnsd,bhnsv->bhndv', k_last, v)
    decay = jnp.exp(b_last[:, :, :, 0, :])

    def step(state, inp):
        q_n, kv_n, d_n = inp
        o = jnp.einsum('bhcd,bhdv->bhcv', q_n, state)
        return d_n[..., None] * state + kv_n, o

    state0 = jnp.zeros((B, H, DK, DV), jnp.float32)
    _, o_inter = lax.scan(step, state0, (jnp.moveaxis(q_e, 2, 0), jnp.moveaxis(kv, 2, 0), jnp.moveaxis(decay, 2, 0)))
    o = o_intra + jnp.moveaxis(o_inter, 0, 2)
    return o.transpose(0, 2, 3, 1, 4).reshape(B, S, H, DV)


def hybrid_layer(x, p_l, rel_bias, w_in, w_a2, b_a, gla_norm_w, pos_cmp, w_ck1, b_ck1, w_ck2,
                 w_cv1, b_cv1, w_cv2, w_out, w_pe, w_pg, b_pg, ln_g, ln_b):
    B, S, _ = x.shape
    u = x @ w_in
    (q_n, kc_r, vc_r, ks_r, vs_r, kw_r, vw_r, g_n, z_n,
     q_g, k_g, v_g, a_low, z_g) = jnp.split(u, IN_SPLIT_POINTS, axis=-1)
    kvs = lambda a: a.reshape(B, S, NSA_KV_GROUPS, NSA_DH)
    kc = compress_blocks(kvs(kc_r), pos_cmp, w_ck1, b_ck1, w_ck2)
    vc = compress_blocks(kvs(vc_r), pos_cmp, w_cv1, b_cv1, w_cv2)
    gates = jax.nn.sigmoid(g_n.reshape(B, S, NSA_HEADS, 3))
    o_nsa = nsa_attention(q_n.reshape(B, S, NSA_HEADS, NSA_DH), kc, vc, kvs(ks_r), kvs(vs_r),
                          kvs(kw_r), kvs(vw_r), gates, rel_bias)
    o_nsa = o_nsa * jax.nn.silu(z_n)
    log_a = jax.nn.log_sigmoid((a_low @ w_a2 + b_a).astype(jnp.float32)) / GLA_TAU
    o_gla = gla_chunked(q_g.reshape(B, S, GLA_HEADS, GLA_DK), k_g.reshape(B, S, GLA_HEADS, GLA_DK),
                        v_g.reshape(B, S, GLA_HEADS, GLA_DV), log_a.reshape(B, S, GLA_HEADS, GLA_DK))
    o_gla = rms_norm(o_gla, gla_norm_w).reshape(B, S, GLA_WIDTH) * jax.nn.silu(z_g)
    y = jnp.concatenate([o_nsa, o_gla], axis=-1) @ w_out
    r = ALPHA * x + y
    r = r + jax.nn.sigmoid(r @ w_pg + b_pg) * (p_l @ w_pe)
    return layer_norm(r, ln_g, ln_b)


def setup_inputs(seed: int = 0) -> dict:
    key = jax.random.key(seed)
    ks = jax.random.split(key, 22)

    def nrm(k, shape, scale):
        return jax.random.normal(k, shape, jnp.float32) * scale

    return {
        'x': nrm(ks[0], (BATCH, SEQ, D_MODEL), 1.0),
        'p': nrm(ks[1], (DEPTH, BATCH, SEQ, PLE_DIM), 1.0),
        'ln0_g': 1.0 + nrm(ks[2], (D_MODEL,), 0.05),
        'ln0_b': nrm(ks[3], (D_MODEL,), 0.02),
        'rel_bias': nrm(ks[4], (NUM_BUCKETS, NSA_HEADS), 0.3),
        'w_in': nrm(ks[5], (DEPTH, D_MODEL, D_IN), D_MODEL ** -0.5),
        'w_a2': nrm(ks[6], (DEPTH, GLA_RANK, GLA_HEADS * GLA_DK), GLA_RANK ** -0.5),
        'b_a': nrm(ks[7], (DEPTH, GLA_HEADS * GLA_DK), 0.1),
        'gla_norm_w': 1.0 + nrm(ks[8], (DEPTH, GLA_DV), 0.05),
        'pos_cmp': nrm(ks[9], (DEPTH, CMP_LEN, NSA_DH), 0.1),
        'w_ck1': nrm(ks[10], (DEPTH, CMP_LEN * NSA_DH, CMP_HIDDEN), (CMP_LEN * NSA_DH) ** -0.5),
        'b_ck1': nrm(ks[11], (DEPTH, CMP_HIDDEN), 0.02),
        'w_ck2': nrm(ks[12], (DEPTH, CMP_HIDDEN, NSA_DH), CMP_HIDDEN ** -0.5),
        'w_cv1': nrm(ks[13], (DEPTH, CMP_LEN * NSA_DH, CMP_HIDDEN), (CMP_LEN * NSA_DH) ** -0.5),
        'b_cv1': nrm(ks[14], (DEPTH, CMP_HIDDEN), 0.02),
        'w_cv2': nrm(ks[15], (DEPTH, CMP_HIDDEN, NSA_DH), CMP_HIDDEN ** -0.5),
        'w_out': nrm(ks[16], (DEPTH, MIX_WIDTH, D_MODEL), MIX_WIDTH ** -0.5 * BETA),
        'w_pe': nrm(ks[17], (DEPTH, PLE_DIM, D_MODEL), PLE_DIM ** -0.5 * BETA),
        'w_pg': nrm(ks[18], (DEPTH, D_MODEL, D_MODEL), D_MODEL ** -0.5),
        'b_pg': nrm(ks[19], (DEPTH, D_MODEL), 0.02),
        'ln_g': 1.0 + nrm(ks[20], (DEPTH, D_MODEL), 0.05),
        'ln_b': nrm(ks[21], (DEPTH, D_MODEL), 0.02),
    }


def reference(x, p, ln0_g, ln0_b, rel_bias, w_in, w_a2, b_a, gla_norm_w, pos_cmp, w_ck1, b_ck1, w_ck2,
              w_cv1, b_cv1, w_cv2, w_out, w_pe, w_pg, b_pg, ln_g, ln_b):
    h = layer_norm(x, ln0_g, ln0_b)
    for i in range(DEPTH):
        h = hybrid_layer(h, p[i], rel_bias, w_in[i], w_a2[i], b_a[i], gla_norm_w[i], pos_cmp[i],
                         w_ck1[i], b_ck1[i], w_ck2[i], w_cv1[i], b_cv1[i], w_cv2[i], w_out[i],
                         w_pe[i], w_pg[i], b_pg[i], ln_g[i], ln_b[i])
    return h
```

```python
import functools
import math

import numpy as np
import jax
import jax.numpy as jnp
from jax import lax
from jax.experimental import pallas as pl
from jax.experimental.pallas import tpu as pltpu

F32 = jnp.float32
BF16 = jnp.bfloat16

D_MODEL = 1024
PLE_DIM = 256
NSA_HEADS = 8
NSA_G = 2
NSA_R = NSA_HEADS // NSA_G
DH = 64
CMP_LEN = 32
CMP_STRIDE = 16
CMP_HIDDEN = 256
SEL_BLOCK = 64
TOP_N = 16
N_LOCAL = 2
WINDOW = 512
QB = 128
GLA_HEADS = 4
GLA_DK = 64
GLA_DV = 128
GLA_RANK = 16
GLA_TAU = 16.0
GLA_CHUNK = 64
NUM_BUCKETS = 32
MAX_DISTANCE = 128
DEPTH = 1
ALPHA = (2.0 * DEPTH) ** 0.25
EPS = 1e-5
NEG = -1e30
POS = 1e30

LANES = 128
VMEM_LIMIT = 56 * 1024 * 1024

SEL_TILE = 256
SUPER = 64 * SEL_BLOCK
TW_WIDTH = WINDOW + QB + WINDOW

C_QN, C_KC, C_VC, C_KS, C_VS, C_KVW0, C_KVW1, C_ZN, C_QKG, C_VG, C_ZG, C_MISC, C_END = (
    0, 512, 640, 768, 896, 1024, 1152, 1280, 1792, 2304, 2816, 3328, 3456)
MISC_GATES = 0
MISC_ALOW = 24


def _bucket_thresholds():
    d = np.arange(0, 4 * MAX_DISTANCE)
    max_exact = NUM_BUCKETS // 2
    nf = np.maximum(d, 1).astype(np.float32)
    large = max_exact + (np.log(nf / np.float32(max_exact)) / np.float32(math.log(MAX_DISTANCE / max_exact))
                         * np.float32(NUM_BUCKETS - max_exact)).astype(np.int32)
    large = np.minimum(large, NUM_BUCKETS - 1)
    bucket = np.where(d < max_exact, d, large)
    assert np.all(np.diff(bucket) >= 0) and bucket[-1] == NUM_BUCKETS - 1
    return [int(np.argmax(bucket >= k)) for k in range(NUM_BUCKETS)]


_THR = _bucket_thresholds()


def _dot(a, b, **kw):
    return jnp.dot(a, b, preferred_element_type=F32, **kw)


def _dot_nt(a, b, **kw):
    return lax.dot_general(a, b, (((1,), (1,)), ((), ())), preferred_element_type=F32, **kw)


def _dot_tn(a, b, **kw):
    return lax.dot_general(a, b, (((0,), (0,)), ((), ())), preferred_element_type=F32, **kw)


def _layer_norm(x, g, b):
    mu = jnp.mean(x, axis=-1, keepdims=True)
    xc = x - mu
    var = jnp.mean(xc * xc, axis=-1, keepdims=True)
    return xc * lax.rsqrt(var + EPS) * g + b


def _sigmoid(x):
    return 1.0 / (1.0 + jnp.exp(-x))


def _silu(x):
    return x * _sigmoid(x)


def _bias_from_dist(rb_ref, dist, valid, out_ref_for_head):
    masks = [dist >= _THR[k] for k in range(1, NUM_BUCKETS)]
    for h in range(NSA_HEADS):
        val = jnp.full(dist.shape, rb_ref[0, h], F32)
        for k in range(1, NUM_BUCKETS):
            val = jnp.where(masks[k - 1], rb_ref[k, h], val)
        val = val - rb_ref[NUM_BUCKETS - 1, h]
        out_ref_for_head(h, jnp.where(valid, val, NEG))


def _bias_cmp_kernel(rb_ref, out_ref):
    i = pl.program_id(0)
    shape = out_ref.shape[2:]
    a = lax.broadcasted_iota(jnp.int32, shape, 0)
    j = lax.broadcasted_iota(jnp.int32, shape, 1)
    dist = a + QB * i - CMP_STRIDE * j - (CMP_LEN - 1)

    def put(h, v):
        out_ref[0, h] = v
    _bias_from_dist(rb_ref, dist, dist >= 0, put)


def _bias_win_kernel(rb_ref, out_ref):
    shape = out_ref.shape[1:]
    a = lax.broadcasted_iota(jnp.int32, shape, 0)
    y = lax.broadcasted_iota(jnp.int32, shape, 1)
    dist = a + WINDOW - y

    def put(h, v):
        out_ref[h] = v
    _bias_from_dist(rb_ref, dist, (dist >= 0) & (dist < WINDOW), put)


def _bias_tables(rel_bias, nb, nc):
    smem = pl.BlockSpec(memory_space=pltpu.SMEM)
    bias_c = pl.pallas_call(
        _bias_cmp_kernel,
        grid=(nb,),
        in_specs=[smem],
        out_specs=pl.BlockSpec((1, NSA_HEADS, QB, nc), lambda i: (i, 0, 0, 0)),
        out_shape=jax.ShapeDtypeStruct((nb, NSA_HEADS, QB, nc), F32),
        compiler_params=pltpu.CompilerParams(dimension_semantics=("parallel",)),
        name="bias_cmp",
    )(rel_bias)
    bias_w = pl.pallas_call(
        _bias_win_kernel,
        in_specs=[smem],
        out_shape=jax.ShapeDtypeStruct((NSA_HEADS, QB, TW_WIDTH), F32),
        name="bias_win",
    )(rel_bias)
    return bias_c, bias_w


def _inproj_kernel(x_ref, g_ref, b_ref, w_ref, qn_ref, kcr_ref, vcr_ref, ksa_ref, vsa_ref, kvw_ref,
                   zn_ref, qkg_ref, vg_ref, zg_ref, misc_ref):
    t = pl.program_id(1)
    tm = x_ref.shape[1]
    h = _layer_norm(x_ref[0], g_ref[...], b_ref[...]).astype(BF16)

    def mm(c0, c1):
        return _dot(h, w_ref[:, c0:c1])

    qn_ref[0] = mm(C_QN, C_KC).astype(BF16)
    kcr_ref[0] = mm(C_KC, C_VC).astype(BF16)
    vcr_ref[0] = mm(C_VC, C_KS).astype(BF16)
    ks = mm(C_KS, C_VS)
    vs = mm(C_VS, C_KVW0)
    lane = lax.broadcasted_iota(jnp.int32, (tm, LANES), 1)
    row = lax.broadcasted_iota(jnp.int32, (tm, LANES), 0)
    low = lane < DH
    blk = ((t * tm + row) // SEL_BLOCK) % (SUPER // SEL_BLOCK)
    onehot = jnp.where(lane - DH == blk, 1.0, 0.0)
    ones_col = jnp.where(lane == DH, 1.0, 0.0)
    ksa_ref[0, 0] = jnp.where(low, ks, onehot).astype(BF16)
    ksa_ref[0, 1] = jnp.where(low, pltpu.roll(ks, DH, 1), onehot).astype(BF16)
    vsa_ref[0, 0] = jnp.where(low, vs, ones_col).astype(BF16)
    vsa_ref[0, 1] = jnp.where(low, pltpu.roll(vs, DH, 1), ones_col).astype(BF16)
    kvw_ref[0, 0] = mm(C_KVW0, C_KVW1).astype(BF16)
    kvw_ref[0, 1] = mm(C_KVW1, C_ZN).astype(BF16)
    zn_ref[0] = mm(C_ZN, C_QKG)
    qkg_ref[0] = mm(C_QKG, C_VG)
    vg_ref[0] = mm(C_VG, C_ZG)
    zg_ref[0] = mm(C_ZG, C_MISC)
    misc_ref[0] = mm(C_MISC, C_END)


def _in_projection(x, ln0_g, ln0_b, w_perm, tm):
    B, S, D = x.shape
    tok = lambda w: pl.BlockSpec((1, tm, w), lambda b, t: (b, t, 0))
    grp = pl.BlockSpec((1, NSA_G, tm, LANES), lambda b, t: (b, 0, t, 0))
    full2 = lambda a: pl.BlockSpec(a.shape, lambda b, t: (0, 0))
    sds = jax.ShapeDtypeStruct
    return pl.pallas_call(
        _inproj_kernel,
        grid=(B, S // tm),
        in_specs=[tok(D), full2(ln0_g), full2(ln0_b), full2(w_perm)],
        out_specs=[tok(512), tok(LANES), tok(LANES), grp, grp, grp, tok(512), tok(512), tok(512), tok(512),
                   tok(LANES)],
        out_shape=[sds((B, S, 512), BF16), sds((B, S, LANES), BF16), sds((B, S, LANES), BF16),
                   sds((B, NSA_G, S, LANES), BF16), sds((B, NSA_G, S, LANES), BF16),
                   sds((B, NSA_G, S, LANES), BF16),
                   sds((B, S, 512), F32), sds((B, S, 512), F32), sds((B, S, 512), F32), sds((B, S, 512), F32),
                   sds((B, S, LANES), F32)],
        compiler_params=pltpu.CompilerParams(dimension_semantics=("parallel", "parallel"),
                                             vmem_limit_bytes=VMEM_LIMIT),
        name="ln0_inproj",
    )(x, ln0_g, ln0_b, w_perm)


def _gelu_tanh(x):
    c = math.sqrt(2.0 / math.pi)
    return x * (0.5 * (1.0 + jnp.tanh(c * (x + 0.044715 * (x * x * x)))))


def _compress_kernel(xk_ref, xv_ref, pos_ref, wk1_ref, bk1_ref, wk2_ref, wv1_ref, bv1_ref, wv2_ref, out_ref):
    half = CMP_STRIDE * DH
    nch = xk_ref.shape[2]

    def branch(x_ref, w1_ref, b1_ref, w2_ref):
        x = x_ref[0, 0]
        top = _dot(x, w1_ref[0:half, :])
        bot = _dot(x, w1_ref[half:2 * half, :])
        cpos = _dot(pos_ref[...], w1_ref[...])[0:1, :]
        pre = top + pltpu.roll(bot, nch - 1, 0) + cpos + b1_ref[...]
        return _dot(_gelu_tanh(pre).astype(BF16), w2_ref[...])

    out = branch(xk_ref, wk1_ref, bk1_ref, wk2_ref) + branch(xv_ref, wv1_ref, bv1_ref, wv2_ref)
    out_ref[0, 0] = out.astype(BF16)


def _compress(xk, xv, pos8, wk1, bk1, wk2p, wv1, bv1, wv2p):
    B, G, nch, width = xk.shape
    blk = pl.BlockSpec((1, 1, nch, width), lambda b, g: (b, g, 0, 0))
    full2 = lambda a: pl.BlockSpec(a.shape, lambda b, g: (0, 0))
    return pl.pallas_call(
        _compress_kernel,
        grid=(B, G),
        in_specs=[blk, blk, full2(pos8), full2(wk1), full2(bk1), full2(wk2p), full2(wv1), full2(bv1),
                  full2(wv2p)],
        out_specs=pl.BlockSpec((1, 1, nch, LANES), lambda b, g: (b, g, 0, 0)),
        out_shape=jax.ShapeDtypeStruct((B, G, nch, LANES), BF16),
        compiler_params=pltpu.CompilerParams(dimension_semantics=("parallel", "parallel"),
                                             vmem_limit_bytes=VMEM_LIMIT),
        name="kv_compress",
    )(xk, xv, pos8, wk1, bk1, wk2p, wv1, bv1, wv2p)


def _nsa_kernel(q_ref, ksa_ref, vsa_ref, kvw_ref, kvc_ref, tw_ref, bc_ref, misc_ref, ovl_ref, o_ref,
                qa_ref, m_ref, acc_ref):
    g = pl.program_id(1)
    i = pl.program_id(2)
    t0 = i * QB
    rows = NSA_R * QB

    qt = q_ref[0].astype(F32)
    q4 = jnp.concatenate([qt[:, r * DH:(r + 1) * DH] for r in range(NSA_R)], axis=0)
    qpad = jnp.concatenate([q4, jnp.zeros((rows, DH), F32)], axis=1).astype(BF16)

    kvc = kvc_ref[0, 0]
    bc = bc_ref[0].reshape(rows, bc_ref.shape[3])
    s = _dot_nt(qpad, kvc) + bc
    valid = bc > 0.5 * NEG
    m = jnp.max(s, axis=1, keepdims=True)
    p = jnp.where(valid, jnp.exp(s - m), 0.0)
    l = jnp.sum(p, axis=1, keepdims=True)
    pn = p * jnp.where(l > 0.0, 1.0 / l, 0.0)
    o_c = _dot(pn.astype(BF16), kvc)[:, DH:2 * DH]

    psum = pn[0:QB] + pn[QB:2 * QB] + pn[2 * QB:3 * QB] + pn[3 * QB:4 * QB]
    imp_t = _dot_nt(ovl_ref[...], psum, precision=lax.Precision.HIGHEST)
    nsel = imp_t.shape[0]
    nidx = lax.broadcasted_iota(jnp.int32, (nsel, QB), 0)
    qidx = lax.broadcasted_iota(jnp.int32, (nsel, QB), 1)
    cur = (t0 + qidx) // SEL_BLOCK
    causal = nidx <= cur
    forced = causal & ((nidx == 0) | (nidx >= cur - (N_LOCAL - 1)))
    w = jnp.where(forced, POS, jnp.where(causal, imp_t, NEG))
    nf = nidx.astype(F32)
    sel = jnp.zeros((nsel, QB), F32)
    for _ in range(min(TOP_N, nsel)):
        mx = jnp.max(w, axis=0, keepdims=True)
        first = jnp.min(jnp.where(w == mx, nf, float(nsel)), axis=0, keepdims=True)
        pick = nf == first
        sel = jnp.where(pick, 1.0, sel)
        w = jnp.where(pick, -3e38, w)
    sel_bias = jnp.where((sel > 0.5) & causal, 0.0, NEG).T

    n_super = qa_ref.shape[0]
    for u in range(n_super):
        sb = sel_bias[:, u * 64:(u + 1) * 64]
        if sb.shape[1] < 64:
            sb = jnp.concatenate([sb, jnp.full((QB, 64 - sb.shape[1]), NEG, F32)], axis=1)
        sb4 = jnp.concatenate([sb] * NSA_R, axis=0)
        qa_ref[u] = jnp.concatenate([q4, sb4], axis=1).astype(BF16)

    m_ref[...] = jnp.full(m_ref.shape, NEG, F32)
    acc_ref[...] = jnp.zeros(acc_ref.shape, F32)

    def tile_step(j, bias_off):
        k0 = pl.multiple_of(j * SEL_TILE, SEL_TILE)
        k = ksa_ref[0, 0, pl.ds(k0, SEL_TILE), :]
        qa = qa_ref[j // (SUPER // SEL_TILE)]
        sc = _dot_nt(qa, k)
        if bias_off is not None:
            off = pl.multiple_of(bias_off, LANES)
            sc = sc + tw_ref[:, :, pl.ds(off, SEL_TILE)].reshape(rows, SEL_TILE)
        m_prev = m_ref[...]
        m_new = jnp.maximum(m_prev, jnp.max(sc, axis=1, keepdims=True))
        alpha = jnp.exp(m_prev - m_new)
        pt = jnp.exp(sc - m_new)
        v = vsa_ref[0, 0, pl.ds(k0, SEL_TILE), :]
        acc_ref[...] = alpha * acc_ref[...] + _dot(pt.astype(BF16), v)
        m_ref[...] = m_new

    jd1 = i // 2
    jd0 = jnp.maximum((i + 1) // 2 - 1, 0)
    tile_step(jd1, WINDOW - (t0 - jd1 * SEL_TILE))

    @pl.when(jd0 < jd1)
    def _():
        tile_step(jd0, WINDOW - (t0 - jd0 * SEL_TILE))

    def far_body(j, carry):
        tile_step(j, None)
        return carry
    lax.fori_loop(0, jd0, far_body, 0)

    acc = acc_ref[...]
    o_s = acc[:, 0:DH] / acc[:, DH:DH + 1]

    wlen = WINDOW + QB
    start = pl.multiple_of(jnp.maximum(t0 - WINDOW, 0), LANES)
    offw = pl.multiple_of(start - t0 + WINDOW, LANES)
    kvw = kvw_ref[0, 0, pl.ds(start, wlen), :]
    sw = _dot_nt(qpad, kvw) + tw_ref[:, :, pl.ds(offw, wlen)].reshape(rows, wlen)
    mw = jnp.max(sw, axis=1, keepdims=True)
    pw = jnp.exp(sw - mw)
    lw = jnp.sum(pw, axis=1, keepdims=True)
    o_w = _dot(pw.astype(BF16), kvw)[:, DH:2 * DH] / lw

    sg = _sigmoid(misc_ref[0])
    outs = []
    for r in range(NSA_R):
        def gate(c):
            c0 = MISC_GATES + 3 * r + c
            c1 = c0 + 3 * NSA_R
            return jnp.where(g == 0, sg[:, c0:c0 + 1], sg[:, c1:c1 + 1])
        sl = slice(r * QB, (r + 1) * QB)
        outs.append(gate(0) * o_c[sl] + gate(1) * o_s[sl] + gate(2) * o_w[sl])
    o_ref[0] = jnp.concatenate(outs, axis=1).astype(o_ref.dtype)


def _nsa(qn, ksa, vsa, kvw, kvc, bias_w, bias_c, misc, ovl_t):
    B, S, _ = qn.shape
    nb = S // QB
    nc = kvc.shape[2]
    n_super = -(-S // SUPER)
    rows = NSA_R * QB
    seq = lambda a: pl.BlockSpec((1, 1) + a.shape[2:], lambda b, g, i: (b, g, 0, 0))
    return pl.pallas_call(
        _nsa_kernel,
        grid=(B, NSA_G, nb),
        in_specs=[
            pl.BlockSpec((1, QB, NSA_R * DH), lambda b, g, i: (b, i, g)),
            seq(ksa), seq(vsa), seq(kvw), seq(kvc),
            pl.BlockSpec((NSA_R, QB, TW_WIDTH), lambda b, g, i: (g, 0, 0)),
            pl.BlockSpec((1, NSA_R, QB, nc), lambda b, g, i: (i, g, 0, 0)),
            pl.BlockSpec((1, QB, LANES), lambda b, g, i: (b, i, 0)),
            pl.BlockSpec(ovl_t.shape, lambda b, g, i: (0, 0)),
        ],
        out_specs=pl.BlockSpec((1, QB, NSA_R * DH), lambda b, g, i: (b, i, g)),
        out_shape=jax.ShapeDtypeStruct((B, S, NSA_HEADS * DH), BF16),
        scratch_shapes=[pltpu.VMEM((n_super, rows, LANES), BF16),
                        pltpu.VMEM((rows, 1), F32),
                        pltpu.VMEM((rows, LANES), F32)],
        compiler_params=pltpu.CompilerParams(dimension_semantics=("parallel", "parallel", "arbitrary"),
                                             vmem_limit_bytes=VMEM_LIMIT),
        name="nsa_attention",
    )(qn, ksa, vsa, kvw, kvc, bias_w, bias_c, misc, ovl_t)


def _gla_kernel(qk_ref, v_ref, z_ref, misc_ref, wa_ref, ba_ref, nw_ref, tri_ref, o_ref, st_ref):
    t = pl.program_id(2)
    T = qk_ref.shape[1]
    C = GLA_CHUNK

    @pl.when(t == 0)
    def _():
        st_ref[...] = jnp.zeros(st_ref.shape, F32)

    lane = lax.broadcasted_iota(jnp.int32, (1, LANES), 1)
    sign = jnp.where(lane < GLA_DK, 1.0 / GLA_TAU, -1.0 / GLA_TAU)
    zz = _dot(misc_ref[0].astype(BF16), wa_ref[0]) + ba_ref[0]
    log_sig = jnp.minimum(zz, 0.0) - jnp.log(1.0 + jnp.exp(-jnp.abs(zz)))
    la2 = log_sig * sign
    rr = lax.broadcasted_iota(jnp.int32, (C, C), 0)
    cc = lax.broadcasted_iota(jnp.int32, (C, C), 1)
    tril = rr >= cc
    scale = GLA_DK ** -0.5

    for n in range(T // C):
        sl = slice(n * C, (n + 1) * C)
        b2 = _dot(tri_ref[...], la2[sl], precision=lax.Precision.HIGHEST)
        e = jnp.exp(b2)
        qke = qk_ref[0, sl, :] * e
        q_e = (qke[:, 0:GLA_DK] * scale).astype(BF16)
        k_e = qke[:, GLA_DK:2 * GLA_DK].astype(BF16)
        v = v_ref[0, sl, :].astype(BF16)
        attn = jnp.where(tril, _dot_nt(q_e, k_e), 0.0)
        st = st_ref[...]
        o = _dot(attn.astype(BF16), v) + _dot_nt(q_e, st.astype(BF16))
        decay = e[C - 1:C, 0:GLA_DK]
        st_ref[...] = (st + _dot_tn(v, k_e)) * decay
        y = o * lax.rsqrt(jnp.mean(o * o, axis=-1, keepdims=True) + EPS) * nw_ref[...]
        o_ref[0, sl, :] = (y * _silu(z_ref[0, sl, :])).astype(o_ref.dtype)


def _gla(qkg, vg, zg, misc, wa_big, ba2, norm_w, tri, T):
    B, S, _ = qkg.shape
    hb = pl.BlockSpec((1, T, LANES), lambda b, h, t: (b, t, h))
    return pl.pallas_call(
        _gla_kernel,
        grid=(B, GLA_HEADS, S // T),
        in_specs=[hb, hb, hb,
                  pl.BlockSpec((1, T, LANES), lambda b, h, t: (b, t, 0)),
                  pl.BlockSpec((1, LANES, LANES), lambda b, h, t: (h, 0, 0)),
                  pl.BlockSpec((1, 1, LANES), lambda b, h, t: (h, 0, 0)),
                  pl.BlockSpec(norm_w.shape, lambda b, h, t: (0, 0)),
                  pl.BlockSpec(tri.shape, lambda b, h, t: (0, 0))],
        out_specs=hb,
        out_shape=jax.ShapeDtypeStruct((B, S, GLA_HEADS * GLA_DV), BF16),
        scratch_shapes=[pltpu.VMEM((GLA_DV, GLA_DK), F32)],
        compiler_params=pltpu.CompilerParams(dimension_semantics=("parallel", "parallel", "arbitrary"),
                                             vmem_limit_bytes=VMEM_LIMIT),
        name="gla_chunked",
    )(qkg, vg, zg, misc, wa_big, ba2, norm_w, tri)


def _out_kernel(x_ref, g0_ref, b0_ref, on_ref, zn_ref, og_ref, p_ref, wo_ref, wpg_ref, bpg_ref, wpe_ref,
                lg_ref, lb_ref, o_ref):
    half = on_ref.shape[2]
    h = _layer_norm(x_ref[0], g0_ref[...], b0_ref[...])
    mix_n = (on_ref[0].astype(F32) * _silu(zn_ref[0])).astype(BF16)
    y = _dot(mix_n, wo_ref[0:half, :]) + _dot(og_ref[0], wo_ref[half:2 * half, :])
    r = ALPHA * h + y
    gate = _sigmoid(_dot(r.astype(BF16), wpg_ref[...]) + bpg_ref[...])
    r = r + gate * _dot(p_ref[0].astype(BF16), wpe_ref[...])
    o_ref[0] = _layer_norm(r, lg_ref[...], lb_ref[...])


def _out_projection(x, ln0_g, ln0_b, o_nsa, zn, o_gla, p, w_out, w_pg, b_pg, w_pe, ln_g, ln_b, tm):
    B, S, D = x.shape
    tok = lambda w: pl.BlockSpec((1, tm, w), lambda b, t: (b, t, 0))
    full2 = lambda a: pl.BlockSpec(a.shape, lambda b, t: (0, 0))
    return pl.pallas_call(
        _out_kernel,
        grid=(B, S // tm),
        in_specs=[tok(D), full2(ln0_g), full2(ln0_b), tok(512), tok(512), tok(512), tok(PLE_DIM),
                  full2(w_out), full2(w_pg), full2(b_pg), full2(w_pe), full2(ln_g), full2(ln_b)],
        out_specs=tok(D),
        out_shape=jax.ShapeDtypeStruct((B, S, D), F32),
        compiler_params=pltpu.CompilerParams(dimension_semantics=("parallel", "parallel"),
                                             vmem_limit_bytes=VMEM_LIMIT),
        name="out_proj_deepnorm",
    )(x, ln0_g, ln0_b, o_nsa, zn, o_gla, p, w_out, w_pg, b_pg, w_pe, ln_g, ln_b)


def _permute_w_in(w):
    widths = (512, 128, 128, 128, 128, 128, 128, 24, 512, 256, 256, 512, 16, 512)
    offs = np.concatenate([[0], np.cumsum(widths)])
    (q_n, kc, vc, ks, vs, kw, vw, gates, z_n, q_g, k_g, v_g, a_low, z_g) = [
        w[:, int(offs[k]):int(offs[k + 1])] for k in range(len(widths))]
    kvw = [jnp.concatenate([kw[:, g * DH:(g + 1) * DH], vw[:, g * DH:(g + 1) * DH]], axis=1)
           for g in range(NSA_G)]
    qk = jnp.concatenate([jnp.concatenate([q_g[:, h * GLA_DK:(h + 1) * GLA_DK],
                                           k_g[:, h * GLA_DK:(h + 1) * GLA_DK]], axis=1)
                          for h in range(GLA_HEADS)], axis=1)
    misc = jnp.concatenate([gates, a_low,
                            jnp.zeros((w.shape[0], LANES - gates.shape[1] - a_low.shape[1]), w.dtype)], axis=1)
    cols = [q_n * (DH ** -0.5), kc, vc, ks, vs, kvw[0], kvw[1], z_n, qk, v_g, z_g, misc]
    return jnp.concatenate(cols, axis=1).astype(BF16)


def _overlap_t(nc_pad, nsel):
    c_start = np.arange(nc_pad) * CMP_STRIDE
    b_start = np.arange(nsel) * SEL_BLOCK
    ov = np.clip(np.minimum(c_start[:, None] + CMP_LEN, b_start[None, :] + SEL_BLOCK)
                 - np.maximum(c_start[:, None], b_start[None, :]), 0, None).astype(np.float32) / CMP_LEN
    ov[nc_pad - 1, :] = 0.0
    return jnp.asarray(ov.T)


def kernel(x, p, ln0_g, ln0_b, rel_bias, w_in, w_a2, b_a, gla_norm_w, pos_cmp, w_ck1, b_ck1, w_ck2,
           w_cv1, b_cv1, w_cv2, w_out, w_pe, w_pg, b_pg, ln_g, ln_b):
    B, S, D = x.shape
    assert D == D_MODEL and S % 512 == 0 and S >= WINDOW + QB and w_in.shape[0] == DEPTH == 1
    nb = S // QB
    nch = S // CMP_STRIDE
    nsel = S // SEL_BLOCK
    row = lambda a: a.reshape(1, -1)

    bias_c, bias_w = _bias_tables(rel_bias, nb, nch)

    w_perm = _permute_w_in(w_in[0])
    tm = 256
    (qn, kcr, vcr, ksa, vsa, kvw, zn, qkg, vg, zg, misc) = _in_projection(x, row(ln0_g), row(ln0_b), w_perm, tm)

    def chunks(a):
        return a.reshape(B, nch, CMP_STRIDE, NSA_G, DH).transpose(0, 3, 1, 2, 4).reshape(
            B, NSA_G, nch, CMP_STRIDE * DH)
    pos8 = jnp.broadcast_to(pos_cmp[0].reshape(1, CMP_LEN * DH), (8, CMP_LEN * DH)).astype(BF16)
    zpad = jnp.zeros((CMP_HIDDEN, DH), F32)
    wk2p = jnp.concatenate([w_ck2[0], zpad], axis=1).astype(BF16)
    wv2p = jnp.concatenate([zpad, w_cv2[0]], axis=1).astype(BF16)
    kvc = _compress(chunks(kcr), chunks(vcr), pos8, w_ck1[0].astype(BF16), row(b_ck1[0]), wk2p,
                    w_cv1[0].astype(BF16), row(b_cv1[0]), wv2p)

    o_nsa = _nsa(qn, ksa, vsa, kvw, kvc, bias_w, bias_c, misc, _overlap_t(nch, nsel))

    wa = w_a2[0]
    wa_big = jnp.zeros((GLA_HEADS, LANES, LANES), F32)
    for h in range(GLA_HEADS):
        wh = wa[:, h * GLA_DK:(h + 1) * GLA_DK]
        wa_big = wa_big.at[h, MISC_ALOW:MISC_ALOW + GLA_RANK, :].set(jnp.concatenate([wh, wh], axis=1))
    ba = b_a[0].reshape(GLA_HEADS, 1, GLA_DK)
    ba2 = jnp.concatenate([ba, ba], axis=2)
    tri = jnp.asarray(np.tril(np.ones((GLA_CHUNK, GLA_CHUNK), np.float32)))
    o_gla = _gla(qkg, vg, zg, misc, wa_big.astype(BF16), ba2, row(gla_norm_w[0]), tri, 512)

    return _out_projection(x, row(ln0_g), row(ln0_b), o_nsa, zn, o_gla, p[0], w_out[0].astype(BF16),
                           w_pg[0].astype(BF16), row(b_pg[0]), w_pe[0].astype(BF16), row(ln_g[0]),
                           row(ln_b[0]), 256)
```

```python
import math

import numpy as np
import jax
import jax.numpy as jnp
from jax import lax
from jax.experimental import pallas as pl
from jax.experimental.pallas import tpu as pltpu

F32 = jnp.float32
BF16 = jnp.bfloat16

D_MODEL = 1024
PLE_DIM = 256
NSA_HEADS = 8
NSA_G = 2
NSA_R = NSA_HEADS // NSA_G
DH = 64
CMP_LEN = 32
CMP_STRIDE = 16
CMP_HIDDEN = 256
SEL_BLOCK = 64
TOP_N = 16
N_LOCAL = 2
WINDOW = 512
QB = 128
GLA_HEADS = 4
GLA_DK = 64
GLA_DV = 128
GLA_RANK = 16
GLA_TAU = 16.0
GLA_CHUNK = 64
NUM_BUCKETS = 32
MAX_DISTANCE = 128
DEPTH = 1
ALPHA = (2.0 * DEPTH) ** 0.25
EPS = 1e-5
NEG = -1e30
POS = 1e30

LANES = 128
VMEM_LIMIT = 56 * 1024 * 1024

SEL_TILE = 256
SWEEP_GROUP = 4
SUPER = 64 * SEL_BLOCK
TW_WIDTH = WINDOW + QB + WINDOW
ROWS = NSA_R * QB

C_QN, C_KC, C_VC, C_KS, C_VS, C_KVW0, C_KVW1, C_ZN, C_QKG, C_VG, C_ZG, C_MISC, C_END = (
    0, 512, 640, 768, 896, 1024, 1152, 1280, 1792, 2304, 2816, 3328, 3456)
MISC_GATES = 0
MISC_ALOW = 24


def _bucket_thresholds():
    d = np.arange(0, 4 * MAX_DISTANCE)
    max_exact = NUM_BUCKETS // 2
    nf = np.maximum(d, 1).astype(np.float32)
    large = max_exact + (np.log(nf / np.float32(max_exact)) / np.float32(math.log(MAX_DISTANCE / max_exact))
                         * np.float32(NUM_BUCKETS - max_exact)).astype(np.int32)
    large = np.minimum(large, NUM_BUCKETS - 1)
    bucket = np.where(d < max_exact, d, large)
    assert np.all(np.diff(bucket) >= 0) and bucket[-1] == NUM_BUCKETS - 1
    return [int(np.argmax(bucket >= k)) for k in range(NUM_BUCKETS)]


_THR = _bucket_thresholds()


def _dot(a, b, **kw):
    return jnp.dot(a, b, preferred_element_type=F32, **kw)


def _dot_nt(a, b, **kw):
    return lax.dot_general(a, b, (((1,), (1,)), ((), ())), preferred_element_type=F32, **kw)


def _dot_tn(a, b, **kw):
    return lax.dot_general(a, b, (((0,), (0,)), ((), ())), preferred_element_type=F32, **kw)


def _layer_norm(x, g, b):
    mu = jnp.mean(x, axis=-1, keepdims=True)
    xc = x - mu
    var = jnp.mean(xc * xc, axis=-1, keepdims=True)
    return xc * lax.rsqrt(var + EPS) * g + b


def _sigmoid(x):
    return 1.0 / (1.0 + jnp.exp(-x))


def _silu(x):
    return x * _sigmoid(x)


def _bias_from_dist(rb_ref, dist, valid, put):
    masks = [dist >= _THR[k] for k in range(1, NUM_BUCKETS)]
    for h in range(NSA_HEADS):
        val = jnp.full(dist.shape, rb_ref[0, h], F32)
        for k in range(1, NUM_BUCKETS):
            val = jnp.where(masks[k - 1], rb_ref[k, h], val)
        val = val - rb_ref[NUM_BUCKETS - 1, h]
        put(h, jnp.where(valid, val, NEG))


def _bias_cmp_kernel(rb_ref, out_ref):
    i = pl.program_id(0)
    shape = out_ref.shape[2:]
    j = lax.broadcasted_iota(jnp.int32, shape, 0)
    a = lax.broadcasted_iota(jnp.int32, shape, 1)
    dist = a + QB * i - CMP_STRIDE * j - (CMP_LEN - 1)

    def put(h, v):
        out_ref[0, h] = v
    _bias_from_dist(rb_ref, dist, dist >= 0, put)


def _bias_win_kernel(rb_ref, out_ref):
    shape = out_ref.shape[1:]
    y = lax.broadcasted_iota(jnp.int32, shape, 0)
    a = lax.broadcasted_iota(jnp.int32, shape, 1)
    dist = a + WINDOW - y

    def put(h, v):
        out_ref[h] = v
    _bias_from_dist(rb_ref, dist, (dist >= 0) & (dist < WINDOW), put)


def _bias_tables(rel_bias, nb, nc):
    smem = pl.BlockSpec(memory_space=pltpu.SMEM)
    bias_c = pl.pallas_call(
        _bias_cmp_kernel,
        grid=(nb,),
        in_specs=[smem],
        out_specs=pl.BlockSpec((1, NSA_HEADS, nc, QB), lambda i: (i, 0, 0, 0)),
        out_shape=jax.ShapeDtypeStruct((nb, NSA_HEADS, nc, QB), F32),
        compiler_params=pltpu.CompilerParams(dimension_semantics=("parallel",)),
        name="bias_cmp",
    )(rel_bias)
    bias_w = pl.pallas_call(
        _bias_win_kernel,
        in_specs=[smem],
        out_shape=jax.ShapeDtypeStruct((NSA_HEADS, TW_WIDTH, QB), F32),
        name="bias_win",
    )(rel_bias)
    return bias_c, bias_w


def _inproj_kernel(x_ref, g_ref, b_ref, w_ref, qn_ref, kcr_ref, vcr_ref, ksa_ref, vsat_ref, kvw_ref, kvwt_ref,
                   zn_ref, qkg_ref, vg_ref, zg_ref, misc_ref):
    t = pl.program_id(1)
    tm = x_ref.shape[1]
    h = _layer_norm(x_ref[0], g_ref[...], b_ref[...]).astype(BF16)

    def mm(c0, c1):
        return _dot(h, w_ref[:, c0:c1])

    qn_ref[0] = mm(C_QN, C_KC).astype(BF16)
    kcr_ref[0] = mm(C_KC, C_VC).astype(BF16)
    vcr_ref[0] = mm(C_VC, C_KS).astype(BF16)
    ks = mm(C_KS, C_VS)
    vs = mm(C_VS, C_KVW0)
    lane = lax.broadcasted_iota(jnp.int32, (tm, LANES), 1)
    row = lax.broadcasted_iota(jnp.int32, (tm, LANES), 0)
    low = lane < DH
    blk = ((t * tm + row) // SEL_BLOCK) % (SUPER // SEL_BLOCK)
    onehot = jnp.where(lane - DH == blk, 1.0, 0.0)
    ones_col = jnp.where(lane == DH, 1.0, 0.0)
    ksa_ref[0, 0] = jnp.where(low, ks, onehot).astype(BF16)
    ksa_ref[0, 1] = jnp.where(low, pltpu.roll(ks, DH, 1), onehot).astype(BF16)
    vsat_ref[0, 0] = jnp.where(low, vs, ones_col).T.astype(BF16)
    vsat_ref[0, 1] = jnp.where(low, pltpu.roll(vs, DH, 1), ones_col).T.astype(BF16)
    for g, (c0, c1) in enumerate(((C_KVW0, C_KVW1), (C_KVW1, C_ZN))):
        kvw = mm(c0, c1)
        kvw_ref[0, g] = kvw.astype(BF16)
        kvwt_ref[0, g] = kvw.T.astype(BF16)
    zn_ref[0] = mm(C_ZN, C_QKG)
    qkg_ref[0] = mm(C_QKG, C_VG)
    vg_ref[0] = mm(C_VG, C_ZG)
    zg_ref[0] = mm(C_ZG, C_MISC)
    misc_ref[0] = mm(C_MISC, C_END)


def _in_projection(x, ln0_g, ln0_b, w_perm, tm):
    B, S, D = x.shape
    tok = lambda w: pl.BlockSpec((1, tm, w), lambda b, t: (b, t, 0))
    grp = pl.BlockSpec((1, NSA_G, tm, LANES), lambda b, t: (b, 0, t, 0))
    grp_t = pl.BlockSpec((1, NSA_G, LANES, tm), lambda b, t: (b, 0, 0, t))
    full2 = lambda a: pl.BlockSpec(a.shape, lambda b, t: (0, 0))
    sds = jax.ShapeDtypeStruct
    return pl.pallas_call(
        _inproj_kernel,
        grid=(B, S // tm),
        in_specs=[tok(D), full2(ln0_g), full2(ln0_b), full2(w_perm)],
        out_specs=[tok(512), tok(LANES), tok(LANES), grp, grp_t, grp, grp_t, tok(512), tok(512), tok(512),
                   tok(512), tok(LANES)],
        out_shape=[sds((B, S, 512), BF16), sds((B, S, LANES), BF16), sds((B, S, LANES), BF16),
                   sds((B, NSA_G, S, LANES), BF16), sds((B, NSA_G, LANES, S), BF16),
                   sds((B, NSA_G, S, LANES), BF16), sds((B, NSA_G, LANES, S), BF16),
                   sds((B, S, 512), F32), sds((B, S, 512), F32), sds((B, S, 512), F32), sds((B, S, 512), F32),
                   sds((B, S, LANES), F32)],
        compiler_params=pltpu.CompilerParams(dimension_semantics=("parallel", "parallel"),
                                             vmem_limit_bytes=VMEM_LIMIT),
        name="ln0_inproj",
    )(x, ln0_g, ln0_b, w_perm)


def _gelu_tanh(x):
    c = math.sqrt(2.0 / math.pi)
    return x * (0.5 * (1.0 + jnp.tanh(c * (x + 0.044715 * (x * x * x)))))


def _compress_kernel(xk_ref, xv_ref, pos_ref, wk1_ref, bk1_ref, wk2_ref, wv1_ref, bv1_ref, wv2_ref,
                     out_ref, outt_ref):
    half = CMP_STRIDE * DH
    nch = xk_ref.shape[2]

    def branch(x_ref, w1_ref, b1_ref, w2_ref):
        x = x_ref[0, 0]
        top = _dot(x, w1_ref[0:half, :])
        bot = _dot(x, w1_ref[half:2 * half, :])
        cpos = _dot(pos_ref[...], w1_ref[...])[0:1, :]
        pre = top + pltpu.roll(bot, nch - 1, 0) + cpos + b1_ref[...]
        return _dot(_gelu_tanh(pre).astype(BF16), w2_ref[...])

    out = branch(xk_ref, wk1_ref, bk1_ref, wk2_ref) + branch(xv_ref, wv1_ref, bv1_ref, wv2_ref)
    out_ref[0, 0] = out.astype(BF16)
    outt_ref[0, 0] = out.T.astype(BF16)


def _compress(xk, xv, pos8, wk1, bk1, wk2p, wv1, bv1, wv2p):
    B, G, nch, width = xk.shape
    blk = pl.BlockSpec((1, 1, nch, width), lambda b, g: (b, g, 0, 0))
    full2 = lambda a: pl.BlockSpec(a.shape, lambda b, g: (0, 0))
    return pl.pallas_call(
        _compress_kernel,
        grid=(B, G),
        in_specs=[blk, blk, full2(pos8), full2(wk1), full2(bk1), full2(wk2p), full2(wv1), full2(bv1),
                  full2(wv2p)],
        out_specs=[pl.BlockSpec((1, 1, nch, LANES), lambda b, g: (b, g, 0, 0)),
                   pl.BlockSpec((1, 1, LANES, nch), lambda b, g: (b, g, 0, 0))],
        out_shape=[jax.ShapeDtypeStruct((B, G, nch, LANES), BF16),
                   jax.ShapeDtypeStruct((B, G, LANES, nch), BF16)],
        compiler_params=pltpu.CompilerParams(dimension_semantics=("parallel", "parallel"),
                                             vmem_limit_bytes=VMEM_LIMIT),
        name="kv_compress",
    )(xk, xv, pos8, wk1, bk1, wk2p, wv1, bv1, wv2p)


def _heads_on_lanes(t4):
    return jnp.concatenate([t4[r] for r in range(NSA_R)], axis=1)


def _nsa_kernel(q_ref, ksa_ref, vsat_ref, kvw_ref, kvwt_ref, kvc_ref, kvct_ref, tw_ref, bc_ref, misc_ref,
                ovl_ref, o_ref, qa_ref, m_ref, acc_ref, s_ref):
    g = pl.program_id(1)
    i = pl.program_id(2)
    t0 = i * QB

    qt = q_ref[0].astype(F32).T
    q_t = jnp.concatenate([qt[r * DH:(r + 1) * DH] for r in range(NSA_R)], axis=1)
    qpad_t = jnp.concatenate([q_t, jnp.zeros((DH, ROWS), F32)], axis=0).astype(BF16)

    bc = _heads_on_lanes(bc_ref[0])
    s = _dot(kvc_ref[0, 0], qpad_t) + bc
    valid = bc > 0.5 * NEG
    m = jnp.max(s, axis=0, keepdims=True)
    p = jnp.where(valid, jnp.exp(s - m), 0.0)
    l = jnp.sum(p, axis=0, keepdims=True)
    pn = p * jnp.where(l > 0.0, 1.0 / l, 0.0)
    o_c = _dot(kvct_ref[0, 0], pn.astype(BF16))[DH:2 * DH]

    psum = pn[:, 0:QB] + pn[:, QB:2 * QB] + pn[:, 2 * QB:3 * QB] + pn[:, 3 * QB:4 * QB]
    imp = _dot(ovl_ref[...], psum, precision=lax.Precision.HIGHEST)
    nsel = imp.shape[0]
    nidx = lax.broadcasted_iota(jnp.int32, (nsel, QB), 0)
    qidx = lax.broadcasted_iota(jnp.int32, (nsel, QB), 1)
    cur = (t0 + qidx) // SEL_BLOCK
    causal = nidx <= cur
    forced = causal & ((nidx == 0) | (nidx >= cur - (N_LOCAL - 1)))
    w = jnp.where(forced, POS, jnp.where(causal, imp, NEG))
    nf = nidx.astype(F32)
    sel = jnp.zeros((nsel, QB), F32)
    for _ in range(min(TOP_N, nsel)):
        mx = jnp.max(w, axis=0, keepdims=True)
        first = jnp.min(jnp.where(w == mx, nf, float(nsel)), axis=0, keepdims=True)
        pick = nf == first
        sel = jnp.where(pick, 1.0, sel)
        w = jnp.where(pick, -3e38, w)
    sel_bias = jnp.where((sel > 0.5) & causal, 0.0, NEG)

    for u in range(qa_ref.shape[0]):
        sb = sel_bias[u * 64:(u + 1) * 64]
        if sb.shape[0] < 64:
            sb = jnp.concatenate([sb, jnp.full((64 - sb.shape[0], QB), NEG, F32)], axis=0)
        sb4 = jnp.concatenate([sb] * NSA_R, axis=1)
        qa_ref[u] = jnp.concatenate([q_t, sb4], axis=0).astype(BF16)

    jd1 = i // 2
    jd0 = jnp.maximum((i + 1) // 2 - 1, 0)

    def score_tile(j, ntiles, bias_off):
        n = ntiles * SEL_TILE
        k0 = pl.multiple_of(j * SEL_TILE, SEL_TILE)
        sc = _dot(ksa_ref[0, 0, pl.ds(k0, n), :], qa_ref[j // (SUPER // SEL_TILE)])
        if bias_off is not None:
            off = pl.multiple_of(bias_off, LANES)
            sc = sc + _heads_on_lanes(tw_ref[:, pl.ds(off, n), :])
        s_ref[pl.ds(k0, n), :] = sc
        return jnp.max(sc, axis=0, keepdims=True)

    def far_group(jj, mm):
        for h in range(SWEEP_GROUP // 2):
            mm = jnp.maximum(mm, score_tile(SWEEP_GROUP * jj + 2 * h, 2, None))
        return mm

    n_grp = jd0 // SWEEP_GROUP
    m_far = lax.fori_loop(0, n_grp, far_group, jnp.full((1, ROWS), NEG, F32))
    m_far = lax.fori_loop(n_grp * SWEEP_GROUP, jd0, lambda j, mm: jnp.maximum(mm, score_tile(j, 1, None)), m_far)
    m_ref[...] = jnp.maximum(m_far, score_tile(jd1, 1, WINDOW - (t0 - jd1 * SEL_TILE)))

    @pl.when(jd0 < jd1)
    def _():
        m_ref[...] = jnp.maximum(m_ref[...], score_tile(jd0, 1, WINDOW - (t0 - jd0 * SEL_TILE)))

    m_sel = m_ref[...]

    def pv_tile(j, ntiles):
        n = ntiles * SEL_TILE
        k0 = pl.multiple_of(j * SEL_TILE, SEL_TILE)
        pt = jnp.exp(s_ref[pl.ds(k0, n), :] - m_sel).astype(BF16)
        return _dot(vsat_ref[0, 0, :, pl.ds(k0, n)], pt)

    acc_ref[...] = pv_tile(jd1, 1)

    def pv_group(jj, carry):
        part = pv_tile(SWEEP_GROUP * jj, 2)
        for h in range(1, SWEEP_GROUP // 2):
            part = part + pv_tile(SWEEP_GROUP * jj + 2 * h, 2)
        acc_ref[...] += part
        return carry

    def pv_single(j, carry):
        acc_ref[...] += pv_tile(j, 1)
        return carry
    n_grp2 = jd1 // SWEEP_GROUP
    lax.fori_loop(0, n_grp2, pv_group, 0)
    lax.fori_loop(n_grp2 * SWEEP_GROUP, jd1, pv_single, 0)

    acc = acc_ref[...]
    o_s = acc[0:DH] / acc[DH:DH + 1]

    wlen = WINDOW + QB
    start = pl.multiple_of(jnp.maximum(t0 - WINDOW, 0), LANES)
    offw = pl.multiple_of(start - t0 + WINDOW, LANES)
    sw = _dot(kvw_ref[0, 0, pl.ds(start, wlen), :], qpad_t) + _heads_on_lanes(tw_ref[:, pl.ds(offw, wlen), :])
    mw = jnp.max(sw, axis=0, keepdims=True)
    pw = jnp.exp(sw - mw)
    lw = jnp.sum(pw, axis=0, keepdims=True)
    o_w = _dot(kvwt_ref[0, 0, :, pl.ds(start, wlen)], pw.astype(BF16))[DH:2 * DH] / lw

    sg = _sigmoid(misc_ref[0]).T
    outs = []
    for r in range(NSA_R):
        def gate(c):
            c0 = MISC_GATES + 3 * r + c
            c1 = c0 + 3 * NSA_R
            return jnp.where(g == 0, sg[c0:c0 + 1], sg[c1:c1 + 1])
        sl = slice(r * QB, (r + 1) * QB)
        outs.append(gate(0) * o_c[:, sl] + gate(1) * o_s[:, sl] + gate(2) * o_w[:, sl])
    o_ref[0] = jnp.concatenate(outs, axis=0).T.astype(o_ref.dtype)


def _nsa(qn, ksa, vsat, kvw, kvwt, kvc, kvct, bias_w, bias_c, misc, ovl):
    B, S, _ = qn.shape
    nb = S // QB
    nc = kvc.shape[2]
    n_super = -(-S // SUPER)
    seq = lambda a: pl.BlockSpec((1, 1) + a.shape[2:], lambda b, g, i: (b, g, 0, 0))
    return pl.pallas_call(
        _nsa_kernel,
        grid=(B, NSA_G, nb),
        in_specs=[
            pl.BlockSpec((1, QB, NSA_R * DH), lambda b, g, i: (b, i, g)),
            seq(ksa), seq(vsat), seq(kvw), seq(kvwt), seq(kvc), seq(kvct),
            pl.BlockSpec((NSA_R, TW_WIDTH, QB), lambda b, g, i: (g, 0, 0)),
            pl.BlockSpec((1, NSA_R, nc, QB), lambda b, g, i: (i, g, 0, 0)),
            pl.BlockSpec((1, QB, LANES), lambda b, g, i: (b, i, 0)),
            pl.BlockSpec(ovl.shape, lambda b, g, i: (0, 0)),
        ],
        out_specs=pl.BlockSpec((1, QB, NSA_R * DH), lambda b, g, i: (b, i, g)),
        out_shape=jax.ShapeDtypeStruct((B, S, NSA_HEADS * DH), BF16),
        scratch_shapes=[pltpu.VMEM((n_super, LANES, ROWS), BF16),
                        pltpu.VMEM((1, ROWS), F32),
                        pltpu.VMEM((LANES, ROWS), F32),
                        pltpu.VMEM((S, ROWS), F32)],
        compiler_params=pltpu.CompilerParams(dimension_semantics=("parallel", "parallel", "arbitrary"),
                                             vmem_limit_bytes=VMEM_LIMIT),
        name="nsa_attention",
    )(qn, ksa, vsat, kvw, kvwt, kvc, kvct, bias_w, bias_c, misc, ovl)


def _gla_kernel(qk_ref, v_ref, z_ref, misc_ref, wa_ref, ba_ref, nw_ref, tri_ref, o_ref, st_ref):
    t = pl.program_id(2)
    T = qk_ref.shape[1]
    C = GLA_CHUNK

    @pl.when(t == 0)
    def _():
        st_ref[...] = jnp.zeros(st_ref.shape, F32)

    lane = lax.broadcasted_iota(jnp.int32, (1, LANES), 1)
    sign = jnp.where(lane < GLA_DK, 1.0 / GLA_TAU, -1.0 / GLA_TAU)
    zz = _dot(misc_ref[0].astype(BF16), wa_ref[0]) + ba_ref[0]
    log_sig = jnp.minimum(zz, 0.0) - jnp.log(1.0 + jnp.exp(-jnp.abs(zz)))
    la2 = log_sig * sign
    rr = lax.broadcasted_iota(jnp.int32, (C, C), 0)
    cc = lax.broadcasted_iota(jnp.int32, (C, C), 1)
    tril = rr >= cc
    scale = GLA_DK ** -0.5

    for n in range(T // C):
        sl = slice(n * C, (n + 1) * C)
        b2 = _dot(tri_ref[...], la2[sl], precision=lax.Precision.HIGHEST)
        e = jnp.exp(b2)
        qke = qk_ref[0, sl, :] * e
        q_e = (qke[:, 0:GLA_DK] * scale).astype(BF16)
        k_e = qke[:, GLA_DK:2 * GLA_DK].astype(BF16)
        v = v_ref[0, sl, :].astype(BF16)
        attn = jnp.where(tril, _dot_nt(q_e, k_e), 0.0)
        st = st_ref[...]
        o = _dot(attn.astype(BF16), v) + _dot_nt(q_e, st.astype(BF16))
        decay = e[C - 1:C, 0:GLA_DK]
        st_ref[...] = (st + _dot_tn(v, k_e)) * decay
        y = o * lax.rsqrt(jnp.mean(o * o, axis=-1, keepdims=True) + EPS) * nw_ref[...]
        o_ref[0, sl, :] = (y * _silu(z_ref[0, sl, :])).astype(o_ref.dtype)


def _gla(qkg, vg, zg, misc, wa_big, ba2, norm_w, tri, T):
    B, S, _ = qkg.shape
    hb = pl.BlockSpec((1, T, LANES), lambda b, h, t: (b, t, h))
    return pl.pallas_call(
        _gla_kernel,
        grid=(B, GLA_HEADS, S // T),
        in_specs=[hb, hb, hb,
                  pl.BlockSpec((1, T, LANES), lambda b, h, t: (b, t, 0)),
                  pl.BlockSpec((1, LANES, LANES), lambda b, h, t: (h, 0, 0)),
                  pl.BlockSpec((1, 1, LANES), lambda b, h, t: (h, 0, 0)),
                  pl.BlockSpec(norm_w.shape, lambda b, h, t: (0, 0)),
                  pl.BlockSpec(tri.shape, lambda b, h, t: (0, 0))],
        out_specs=hb,
        out_shape=jax.ShapeDtypeStruct((B, S, GLA_HEADS * GLA_DV), BF16),
        scratch_shapes=[pltpu.VMEM((GLA_DV, GLA_DK), F32)],
        compiler_params=pltpu.CompilerParams(dimension_semantics=("parallel", "parallel", "arbitrary"),
                                             vmem_limit_bytes=VMEM_LIMIT),
        name="gla_chunked",
    )(qkg, vg, zg, misc, wa_big, ba2, norm_w, tri)


def _out_kernel(x_ref, g0_ref, b0_ref, on_ref, zn_ref, og_ref, p_ref, wo_ref, wpg_ref, bpg_ref, wpe_ref,
                lg_ref, lb_ref, o_ref):
    half = on_ref.shape[2]
    h = _layer_norm(x_ref[0], g0_ref[...], b0_ref[...])
    mix_n = (on_ref[0].astype(F32) * _silu(zn_ref[0])).astype(BF16)
    y = _dot(mix_n, wo_ref[0:half, :]) + _dot(og_ref[0], wo_ref[half:2 * half, :])
    r = ALPHA * h + y
    gate = _sigmoid(_dot(r.astype(BF16), wpg_ref[...]) + bpg_ref[...])
    r = r + gate * _dot(p_ref[0].astype(BF16), wpe_ref[...])
    o_ref[0] = _layer_norm(r, lg_ref[...], lb_ref[...])


def _out_projection(x, ln0_g, ln0_b, o_nsa, zn, o_gla, p, w_out, w_pg, b_pg, w_pe, ln_g, ln_b, tm):
    B, S, D = x.shape
    tok = lambda w: pl.BlockSpec((1, tm, w), lambda b, t: (b, t, 0))
    full2 = lambda a: pl.BlockSpec(a.shape, lambda b, t: (0, 0))
    return pl.pallas_call(
        _out_kernel,
        grid=(B, S // tm),
        in_specs=[tok(D), full2(ln0_g), full2(ln0_b), tok(512), tok(512), tok(512), tok(PLE_DIM),
                  full2(w_out), full2(w_pg), full2(b_pg), full2(w_pe), full2(ln_g), full2(ln_b)],
        out_specs=tok(D),
        out_shape=jax.ShapeDtypeStruct((B, S, D), F32),
        compiler_params=pltpu.CompilerParams(dimension_semantics=("parallel", "parallel"),
                                             vmem_limit_bytes=VMEM_LIMIT),
        name="out_proj_deepnorm",
    )(x, ln0_g, ln0_b, o_nsa, zn, o_gla, p, w_out, w_pg, b_pg, w_pe, ln_g, ln_b)


def _permute_w_in(w):
    widths = (512, 128, 128, 128, 128, 128, 128, 24, 512, 256, 256, 512, 16, 512)
    offs = np.concatenate([[0], np.cumsum(widths)])
    (q_n, kc, vc, ks, vs, kw, vw, gates, z_n, q_g, k_g, v_g, a_low, z_g) = [
        w[:, int(offs[k]):int(offs[k + 1])] for k in range(len(widths))]
    kvw = [jnp.concatenate([kw[:, g * DH:(g + 1) * DH], vw[:, g * DH:(g + 1) * DH]], axis=1)
           for g in range(NSA_G)]
    qk = jnp.concatenate([jnp.concatenate([q_g[:, h * GLA_DK:(h + 1) * GLA_DK],
                                           k_g[:, h * GLA_DK:(h + 1) * GLA_DK]], axis=1)
                          for h in range(GLA_HEADS)], axis=1)
    misc = jnp.concatenate([gates, a_low,
                            jnp.zeros((w.shape[0], LANES - gates.shape[1] - a_low.shape[1]), w.dtype)], axis=1)
    cols = [q_n * (DH ** -0.5), kc, vc, ks, vs, kvw[0], kvw[1], z_n, qk, v_g, z_g, misc]
    return jnp.concatenate(cols, axis=1).astype(BF16)


def _overlap(nc_pad, nsel):
    c_start = np.arange(nc_pad) * CMP_STRIDE
    b_start = np.arange(nsel) * SEL_BLOCK
    ov = np.clip(np.minimum(c_start[:, None] + CMP_LEN, b_start[None, :] + SEL_BLOCK)
                 - np.maximum(c_start[:, None], b_start[None, :]), 0, None).astype(np.float32) / CMP_LEN
    ov[nc_pad - 1, :] = 0.0
    return jnp.asarray(ov.T)


def kernel(x, p, ln0_g, ln0_b, rel_bias, w_in, w_a2, b_a, gla_norm_w, pos_cmp, w_ck1, b_ck1, w_ck2,
           w_cv1, b_cv1, w_cv2, w_out, w_pe, w_pg, b_pg, ln_g, ln_b):
    B, S, D = x.shape
    assert D == D_MODEL and S % 512 == 0 and S >= WINDOW + QB and w_in.shape[0] == DEPTH == 1
    nb = S // QB
    nch = S // CMP_STRIDE
    nsel = S // SEL_BLOCK
    row = lambda a: a.reshape(1, -1)

    bias_c, bias_w = _bias_tables(rel_bias, nb, nch)

    w_perm = _permute_w_in(w_in[0])
    (qn, kcr, vcr, ksa, vsat, kvw, kvwt, zn, qkg, vg, zg, misc) = _in_projection(
        x, row(ln0_g), row(ln0_b), w_perm, 256)

    def chunks(a):
        return a.reshape(B, nch, CMP_STRIDE, NSA_G, DH).transpose(0, 3, 1, 2, 4).reshape(
            B, NSA_G, nch, CMP_STRIDE * DH)
    pos8 = jnp.broadcast_to(pos_cmp[0].reshape(1, CMP_LEN * DH), (8, CMP_LEN * DH)).astype(BF16)
    zpad = jnp.zeros((CMP_HIDDEN, DH), F32)
    wk2p = jnp.concatenate([w_ck2[0], zpad], axis=1).astype(BF16)
    wv2p = jnp.concatenate([zpad, w_cv2[0]], axis=1).astype(BF16)
    kvc, kvct = _compress(chunks(kcr), chunks(vcr), pos8, w_ck1[0].astype(BF16), row(b_ck1[0]), wk2p,
                          w_cv1[0].astype(BF16), row(b_cv1[0]), wv2p)

    o_nsa = _nsa(qn, ksa, vsat, kvw, kvwt, kvc, kvct, bias_w, bias_c, misc, _overlap(nch, nsel))

    wa = w_a2[0]
    wa_big = jnp.zeros((GLA_HEADS, LANES, LANES), F32)
    for h in range(GLA_HEADS):
        wh = wa[:, h * GLA_DK:(h + 1) * GLA_DK]
        wa_big = wa_big.at[h, MISC_ALOW:MISC_ALOW + GLA_RANK, :].set(jnp.concatenate([wh, wh], axis=1))
    ba = b_a[0].reshape(GLA_HEADS, 1, GLA_DK)
    ba2 = jnp.concatenate([ba, ba], axis=2)
    tri = jnp.asarray(np.tril(np.ones((GLA_CHUNK, GLA_CHUNK), np.float32)))
    o_gla = _gla(qkg, vg, zg, misc, wa_big.astype(BF16), ba2, row(gla_norm_w[0]), tri, 512)

    return _out_projection(x, row(ln0_g), row(ln0_b), o_nsa, zn, o_gla, p[0], w_out[0].astype(BF16),
                           w_pg[0].astype(BF16), row(b_pg[0]), w_pe[0].astype(BF16), row(ln_g[0]),
                           row(ln_b[0]), 256)
```

```python
import math

import numpy as np
import jax
import jax.numpy as jnp
from jax import lax
from jax.experimental import pallas as pl
from jax.experimental.pallas import tpu as pltpu

F32 = jnp.float32
BF16 = jnp.bfloat16

D_MODEL = 1024
PLE_DIM = 256
NSA_HEADS = 8
NSA_G = 2
NSA_R = NSA_HEADS // NSA_G
DH = 64
CMP_LEN = 32
CMP_STRIDE = 16
CMP_HIDDEN = 256
SEL_BLOCK = 64
TOP_N = 16
N_LOCAL = 2
WINDOW = 512
QB = 128
GLA_HEADS = 4
GLA_DK = 64
GLA_DV = 128
GLA_RANK = 16
GLA_TAU = 16.0
GLA_CHUNK = 64
NUM_BUCKETS = 32
MAX_DISTANCE = 128
DEPTH = 1
ALPHA = (2.0 * DEPTH) ** 0.25
EPS = 1e-5
NEG = -1e30
POS = 1e30

LANES = 128
VMEM_LIMIT = 56 * 1024 * 1024

SEL_TILE = 256
SWEEP_GROUP = 4
VROWS = 80
SUPER = 64 * SEL_BLOCK
TW_WIDTH = WINDOW + QB + WINDOW
ROWS = NSA_R * QB

C_QN, C_KC, C_VC, C_KS, C_VS, C_KVW0, C_KVW1, C_ZN, C_QKG, C_VG, C_ZG, C_MISC, C_END = (
    0, 512, 640, 768, 896, 1024, 1152, 1280, 1792, 2304, 2816, 3328, 3456)
MISC_GATES = 0
MISC_ALOW = 24


def _bucket_thresholds():
    d = np.arange(0, 4 * MAX_DISTANCE)
    max_exact = NUM_BUCKETS // 2
    nf = np.maximum(d, 1).astype(np.float32)
    large = max_exact + (np.log(nf / np.float32(max_exact)) / np.float32(math.log(MAX_DISTANCE / max_exact))
                         * np.float32(NUM_BUCKETS - max_exact)).astype(np.int32)
    large = np.minimum(large, NUM_BUCKETS - 1)
    bucket = np.where(d < max_exact, d, large)
    assert np.all(np.diff(bucket) >= 0) and bucket[-1] == NUM_BUCKETS - 1
    return [int(np.argmax(bucket >= k)) for k in range(NUM_BUCKETS)]


_THR = _bucket_thresholds()


def _dot(a, b, **kw):
    return jnp.dot(a, b, preferred_element_type=F32, **kw)


def _dot_nt(a, b, **kw):
    return lax.dot_general(a, b, (((1,), (1,)), ((), ())), preferred_element_type=F32, **kw)


def _dot_tn(a, b, **kw):
    return lax.dot_general(a, b, (((0,), (0,)), ((), ())), preferred_element_type=F32, **kw)


def _layer_norm(x, g, b):
    mu = jnp.mean(x, axis=-1, keepdims=True)
    xc = x - mu
    var = jnp.mean(xc * xc, axis=-1, keepdims=True)
    return xc * lax.rsqrt(var + EPS) * g + b


def _sigmoid(x):
    return 1.0 / (1.0 + jnp.exp(-x))


def _silu(x):
    return x * _sigmoid(x)


def _bias_from_dist(rb_ref, dist, valid, put):
    masks = [dist >= _THR[k] for k in range(1, NUM_BUCKETS)]
    for h in range(NSA_HEADS):
        val = jnp.full(dist.shape, rb_ref[0, h], F32)
        for k in range(1, NUM_BUCKETS):
            val = jnp.where(masks[k - 1], rb_ref[k, h], val)
        val = val - rb_ref[NUM_BUCKETS - 1, h]
        put(h, jnp.where(valid, val, NEG))


def _bias_cmp_kernel(rb_ref, out_ref):
    i = pl.program_id(0)
    shape = out_ref.shape[2:]
    j = lax.broadcasted_iota(jnp.int32, shape, 0)
    a = lax.broadcasted_iota(jnp.int32, shape, 1)
    dist = a + QB * i - CMP_STRIDE * j - (CMP_LEN - 1)

    def put(h, v):
        out_ref[0, h] = v
    _bias_from_dist(rb_ref, dist, dist >= 0, put)


def _bias_win_kernel(rb_ref, out_ref):
    shape = out_ref.shape[1:]
    y = lax.broadcasted_iota(jnp.int32, shape, 0)
    a = lax.broadcasted_iota(jnp.int32, shape, 1)
    dist = a + WINDOW - y

    def put(h, v):
        out_ref[h] = v
    _bias_from_dist(rb_ref, dist, (dist >= 0) & (dist < WINDOW), put)


def _bias_tables(rel_bias, nb, nc):
    smem = pl.BlockSpec(memory_space=pltpu.SMEM)
    bias_c = pl.pallas_call(
        _bias_cmp_kernel,
        grid=(nb,),
        in_specs=[smem],
        out_specs=pl.BlockSpec((1, NSA_HEADS, nc, QB), lambda i: (i, 0, 0, 0)),
        out_shape=jax.ShapeDtypeStruct((nb, NSA_HEADS, nc, QB), F32),
        compiler_params=pltpu.CompilerParams(dimension_semantics=("parallel",)),
        name="bias_cmp",
    )(rel_bias)
    bias_w = pl.pallas_call(
        _bias_win_kernel,
        in_specs=[smem],
        out_shape=jax.ShapeDtypeStruct((NSA_HEADS, TW_WIDTH, QB), F32),
        name="bias_win",
    )(rel_bias)
    return bias_c, bias_w


def _inproj_kernel(x_ref, g_ref, b_ref, w_ref, qn_ref, kcr_ref, vcr_ref, ksa_ref, vsat_ref, kvw_ref, kvwt_ref,
                   zn_ref, qkg_ref, vg_ref, zg_ref, misc_ref):
    t = pl.program_id(1)
    tm = x_ref.shape[1]
    h = _layer_norm(x_ref[0], g_ref[...], b_ref[...]).astype(BF16)

    def mm(c0, c1):
        return _dot(h, w_ref[:, c0:c1])

    qn_ref[0] = mm(C_QN, C_KC).astype(BF16)
    kcr_ref[0] = mm(C_KC, C_VC).astype(BF16)
    vcr_ref[0] = mm(C_VC, C_KS).astype(BF16)
    ks = mm(C_KS, C_VS)
    vs = mm(C_VS, C_KVW0)
    lane = lax.broadcasted_iota(jnp.int32, (tm, LANES), 1)
    row = lax.broadcasted_iota(jnp.int32, (tm, LANES), 0)
    low = lane < DH
    blk = ((t * tm + row) // SEL_BLOCK) % (SUPER // SEL_BLOCK)
    onehot = jnp.where(lane - DH == blk, 1.0, 0.0)
    ones_col = jnp.where(lane == DH, 1.0, 0.0)
    ksa_ref[0, 0] = jnp.where(low, ks, onehot).astype(BF16)
    ksa_ref[0, 1] = jnp.where(low, pltpu.roll(ks, DH, 1), onehot).astype(BF16)
    vsat_ref[0, 0] = jnp.where(low, vs, ones_col).T[0:VROWS].astype(BF16)
    vsat_ref[0, 1] = jnp.where(low, pltpu.roll(vs, DH, 1), ones_col).T[0:VROWS].astype(BF16)
    for g, (c0, c1) in enumerate(((C_KVW0, C_KVW1), (C_KVW1, C_ZN))):
        kvw = mm(c0, c1)
        kvw_ref[0, g] = kvw.astype(BF16)
        kvwt_ref[0, g] = kvw.T.astype(BF16)
    zn_ref[0] = mm(C_ZN, C_QKG)
    qkg_ref[0] = mm(C_QKG, C_VG)
    vg_ref[0] = mm(C_VG, C_ZG)
    zg_ref[0] = mm(C_ZG, C_MISC)
    misc_ref[0] = mm(C_MISC, C_END)


def _in_projection(x, ln0_g, ln0_b, w_perm, tm):
    B, S, D = x.shape
    tok = lambda w: pl.BlockSpec((1, tm, w), lambda b, t: (b, t, 0))
    grp = pl.BlockSpec((1, NSA_G, tm, LANES), lambda b, t: (b, 0, t, 0))
    grp_t = pl.BlockSpec((1, NSA_G, LANES, tm), lambda b, t: (b, 0, 0, t))
    grp_v = pl.BlockSpec((1, NSA_G, VROWS, tm), lambda b, t: (b, 0, 0, t))
    full2 = lambda a: pl.BlockSpec(a.shape, lambda b, t: (0, 0))
    sds = jax.ShapeDtypeStruct
    return pl.pallas_call(
        _inproj_kernel,
        grid=(B, S // tm),
        in_specs=[tok(D), full2(ln0_g), full2(ln0_b), full2(w_perm)],
        out_specs=[tok(512), tok(LANES), tok(LANES), grp, grp_v, grp, grp_t, tok(512), tok(512), tok(512),
                   tok(512), tok(LANES)],
        out_shape=[sds((B, S, 512), BF16), sds((B, S, LANES), BF16), sds((B, S, LANES), BF16),
                   sds((B, NSA_G, S, LANES), BF16), sds((B, NSA_G, VROWS, S), BF16),
                   sds((B, NSA_G, S, LANES), BF16), sds((B, NSA_G, LANES, S), BF16),
                   sds((B, S, 512), F32), sds((B, S, 512), F32), sds((B, S, 512), F32), sds((B, S, 512), F32),
                   sds((B, S, LANES), F32)],
        compiler_params=pltpu.CompilerParams(dimension_semantics=("parallel", "parallel"),
                                             vmem_limit_bytes=VMEM_LIMIT),
        name="ln0_inproj",
    )(x, ln0_g, ln0_b, w_perm)


def _gelu_tanh(x):
    c = math.sqrt(2.0 / math.pi)
    return x * (0.5 * (1.0 + jnp.tanh(c * (x + 0.044715 * (x * x * x)))))


def _compress_kernel(xk_ref, xv_ref, pos_ref, wk1_ref, bk1_ref, wk2_ref, wv1_ref, bv1_ref, wv2_ref,
                     out_ref, outt_ref):
    half = CMP_STRIDE * DH
    nch = xk_ref.shape[2]

    def branch(x_ref, w1_ref, b1_ref, w2_ref):
        x = x_ref[0, 0]
        top = _dot(x, w1_ref[0:half, :])
        bot = _dot(x, w1_ref[half:2 * half, :])
        cpos = _dot(pos_ref[...], w1_ref[...])[0:1, :]
        pre = top + pltpu.roll(bot, nch - 1, 0) + cpos + b1_ref[...]
        return _dot(_gelu_tanh(pre).astype(BF16), w2_ref[...])

    out = branch(xk_ref, wk1_ref, bk1_ref, wk2_ref) + branch(xv_ref, wv1_ref, bv1_ref, wv2_ref)
    out_ref[0, 0] = out.astype(BF16)
    outt_ref[0, 0] = out.T.astype(BF16)


def _compress(xk, xv, pos8, wk1, bk1, wk2p, wv1, bv1, wv2p):
    B, G, nch, width = xk.shape
    blk = pl.BlockSpec((1, 1, nch, width), lambda b, g: (b, g, 0, 0))
    full2 = lambda a: pl.BlockSpec(a.shape, lambda b, g: (0, 0))
    return pl.pallas_call(
        _compress_kernel,
        grid=(B, G),
        in_specs=[blk, blk, full2(pos8), full2(wk1), full2(bk1), full2(wk2p), full2(wv1), full2(bv1),
                  full2(wv2p)],
        out_specs=[pl.BlockSpec((1, 1, nch, LANES), lambda b, g: (b, g, 0, 0)),
                   pl.BlockSpec((1, 1, LANES, nch), lambda b, g: (b, g, 0, 0))],
        out_shape=[jax.ShapeDtypeStruct((B, G, nch, LANES), BF16),
                   jax.ShapeDtypeStruct((B, G, LANES, nch), BF16)],
        compiler_params=pltpu.CompilerParams(dimension_semantics=("parallel", "parallel"),
                                             vmem_limit_bytes=VMEM_LIMIT),
        name="kv_compress",
    )(xk, xv, pos8, wk1, bk1, wk2p, wv1, bv1, wv2p)


def _heads_on_lanes(t4):
    return jnp.concatenate([t4[r] for r in range(NSA_R)], axis=1)


def _nsa_kernel(q_ref, ksa_ref, vsat_ref, kvw_ref, kvwt_ref, kvc_ref, kvct_ref, tw_ref, bc_ref, misc_ref,
                ovl_ref, o_ref, qa_ref, m_ref, acc_ref, s_ref):
    g = pl.program_id(1)
    i = pl.program_id(2)
    t0 = i * QB

    qt = q_ref[0].astype(F32).T
    q_t = jnp.concatenate([qt[r * DH:(r + 1) * DH] for r in range(NSA_R)], axis=1)
    qpad_t = jnp.concatenate([q_t, jnp.zeros((DH, ROWS), F32)], axis=0).astype(BF16)

    bc = _heads_on_lanes(bc_ref[0])
    s = _dot(kvc_ref[0, 0], qpad_t) + bc
    valid = bc > 0.5 * NEG
    m = jnp.max(s, axis=0, keepdims=True)
    p = jnp.where(valid, jnp.exp(s - m), 0.0)
    l = jnp.sum(p, axis=0, keepdims=True)
    pn = p * jnp.where(l > 0.0, 1.0 / l, 0.0)
    o_c = _dot(kvct_ref[0, 0], pn.astype(BF16))[DH:2 * DH]

    wlen = WINDOW + QB
    start = pl.multiple_of(jnp.maximum(t0 - WINDOW, 0), LANES)
    offw = pl.multiple_of(start - t0 + WINDOW, LANES)
    sw = _dot(kvw_ref[0, 0, pl.ds(start, wlen), :], qpad_t) + _heads_on_lanes(tw_ref[:, pl.ds(offw, wlen), :])
    mw = jnp.max(sw, axis=0, keepdims=True)
    pw = jnp.exp(sw - mw)
    lw = jnp.sum(pw, axis=0, keepdims=True)
    o_w = _dot(kvwt_ref[0, 0, :, pl.ds(start, wlen)], pw.astype(BF16))[DH:2 * DH] / lw

    psum = pn[:, 0:QB] + pn[:, QB:2 * QB] + pn[:, 2 * QB:3 * QB] + pn[:, 3 * QB:4 * QB]
    imp = _dot(ovl_ref[...], psum, precision=lax.Precision.HIGHEST)
    nsel = imp.shape[0]
    nidx = lax.broadcasted_iota(jnp.int32, (nsel, QB), 0)
    qidx = lax.broadcasted_iota(jnp.int32, (nsel, QB), 1)
    cur = (t0 + qidx) // SEL_BLOCK
    causal = nidx <= cur
    forced = causal & ((nidx == 0) | (nidx >= cur - (N_LOCAL - 1)))
    w = jnp.where(forced, POS, jnp.where(causal, imp, NEG))
    nf = nidx.astype(F32)
    sel = jnp.zeros((nsel, QB), F32)
    for _ in range(min(TOP_N, nsel)):
        mx = jnp.max(w, axis=0, keepdims=True)
        first = jnp.min(jnp.where(w == mx, nf, float(nsel)), axis=0, keepdims=True)
        pick = nf == first
        sel = jnp.where(pick, 1.0, sel)
        w = jnp.where(pick, -3e38, w)
    sel_bias = jnp.where((sel > 0.5) & causal, 0.0, NEG)

    for u in range(qa_ref.shape[0]):
        sb = sel_bias[u * 64:(u + 1) * 64]
        if sb.shape[0] < 64:
            sb = jnp.concatenate([sb, jnp.full((64 - sb.shape[0], QB), NEG, F32)], axis=0)
        sb4 = jnp.concatenate([sb] * NSA_R, axis=1)
        qa_ref[u] = jnp.concatenate([q_t, sb4], axis=0).astype(BF16)

    jd1 = i // 2
    jd0 = jnp.maximum((i + 1) // 2 - 1, 0)

    def score_tile(j, ntiles, bias_off):
        n = ntiles * SEL_TILE
        k0 = pl.multiple_of(j * SEL_TILE, SEL_TILE)
        sc = _dot(ksa_ref[0, 0, pl.ds(k0, n), :], qa_ref[j // (SUPER // SEL_TILE)])
        if bias_off is not None:
            off = pl.multiple_of(bias_off, LANES)
            sc = sc + _heads_on_lanes(tw_ref[:, pl.ds(off, n), :])
        s_ref[pl.ds(k0, n), :] = sc.astype(s_ref.dtype)
        return jnp.max(sc, axis=0, keepdims=True)

    def far_group(jj, mm):
        for h in range(SWEEP_GROUP // 2):
            mm = jnp.maximum(mm, score_tile(SWEEP_GROUP * jj + 2 * h, 2, None))
        return mm

    n_grp = jd0 // SWEEP_GROUP
    m_far = lax.fori_loop(0, n_grp, far_group, jnp.full((1, ROWS), NEG, F32))
    m_far = lax.fori_loop(n_grp * SWEEP_GROUP, jd0, lambda j, mm: jnp.maximum(mm, score_tile(j, 1, None)), m_far)
    m_ref[...] = jnp.maximum(m_far, score_tile(jd1, 1, WINDOW - (t0 - jd1 * SEL_TILE)))

    @pl.when(jd0 < jd1)
    def _():
        m_ref[...] = jnp.maximum(m_ref[...], score_tile(jd0, 1, WINDOW - (t0 - jd0 * SEL_TILE)))

    m_sel = m_ref[...].astype(BF16)

    def pv_tile(j, ntiles):
        n = ntiles * SEL_TILE
        k0 = pl.multiple_of(j * SEL_TILE, SEL_TILE)
        pt = jnp.exp(s_ref[pl.ds(k0, n), :] - m_sel)
        return _dot(vsat_ref[0, 0, :, pl.ds(k0, n)], pt)

    acc_ref[...] = pv_tile(jd1, 1)

    def pv_group(jj, carry):
        part = pv_tile(SWEEP_GROUP * jj, 2)
        for h in range(1, SWEEP_GROUP // 2):
            part = part + pv_tile(SWEEP_GROUP * jj + 2 * h, 2)
        acc_ref[...] += part
        return carry

    def pv_single(j, carry):
        acc_ref[...] += pv_tile(j, 1)
        return carry
    n_grp2 = jd1 // SWEEP_GROUP
    lax.fori_loop(0, n_grp2, pv_group, 0)
    lax.fori_loop(n_grp2 * SWEEP_GROUP, jd1, pv_single, 0)

    acc = acc_ref[...]
    o_s = acc[0:DH] / acc[DH:DH + 1]

    sg = _sigmoid(misc_ref[0]).T
    outs = []
    for r in range(NSA_R):
        def gate(c):
            c0 = MISC_GATES + 3 * r + c
            c1 = c0 + 3 * NSA_R
            return jnp.where(g == 0, sg[c0:c0 + 1], sg[c1:c1 + 1])
        sl = slice(r * QB, (r + 1) * QB)
        outs.append(gate(0) * o_c[:, sl] + gate(1) * o_s[:, sl] + gate(2) * o_w[:, sl])
    o_ref[0] = jnp.concatenate(outs, axis=0).T.astype(o_ref.dtype)


def _nsa(qn, ksa, vsat, kvw, kvwt, kvc, kvct, bias_w, bias_c, misc, ovl):
    B, S, _ = qn.shape
    nb = S // QB
    nc = kvc.shape[2]
    n_super = -(-S // SUPER)
    seq = lambda a: pl.BlockSpec((1, 1) + a.shape[2:], lambda b, g, i: (b, g, 0, 0))
    return pl.pallas_call(
        _nsa_kernel,
        grid=(B, NSA_G, nb),
        in_specs=[
            pl.BlockSpec((1, QB, NSA_R * DH), lambda b, g, i: (b, i, g)),
            seq(ksa), seq(vsat), seq(kvw), seq(kvwt), seq(kvc), seq(kvct),
            pl.BlockSpec((NSA_R, TW_WIDTH, QB), lambda b, g, i: (g, 0, 0)),
            pl.BlockSpec((1, NSA_R, nc, QB), lambda b, g, i: (i, g, 0, 0)),
            pl.BlockSpec((1, QB, LANES), lambda b, g, i: (b, i, 0)),
            pl.BlockSpec(ovl.shape, lambda b, g, i: (0, 0)),
        ],
        out_specs=pl.BlockSpec((1, QB, NSA_R * DH), lambda b, g, i: (b, i, g)),
        out_shape=jax.ShapeDtypeStruct((B, S, NSA_HEADS * DH), BF16),
        scratch_shapes=[pltpu.VMEM((n_super, LANES, ROWS), BF16),
                        pltpu.VMEM((1, ROWS), F32),
                        pltpu.VMEM((VROWS, ROWS), F32),
                        pltpu.VMEM((S, ROWS), BF16)],
        compiler_params=pltpu.CompilerParams(dimension_semantics=("parallel", "parallel", "arbitrary"),
                                             vmem_limit_bytes=VMEM_LIMIT),
        name="nsa_attention",
    )(qn, ksa, vsat, kvw, kvwt, kvc, kvct, bias_w, bias_c, misc, ovl)


def _gla_kernel(qk_ref, v_ref, z_ref, misc_ref, wa_ref, ba_ref, nw_ref, tri_ref, o_ref, st_ref):
    t = pl.program_id(2)
    T = qk_ref.shape[1]
    C = GLA_CHUNK

    @pl.when(t == 0)
    def _():
        st_ref[...] = jnp.zeros(st_ref.shape, F32)

    lane = lax.broadcasted_iota(jnp.int32, (1, LANES), 1)
    sign = jnp.where(lane < GLA_DK, 1.0 / GLA_TAU, -1.0 / GLA_TAU)
    zz = _dot(misc_ref[0].astype(BF16), wa_ref[0]) + ba_ref[0]
    log_sig = jnp.minimum(zz, 0.0) - jnp.log(1.0 + jnp.exp(-jnp.abs(zz)))
    la2 = log_sig * sign
    rr = lax.broadcasted_iota(jnp.int32, (C, C), 0)
    cc = lax.broadcasted_iota(jnp.int32, (C, C), 1)
    tril = rr >= cc
    scale = GLA_DK ** -0.5

    for n in range(T // C):
        sl = slice(n * C, (n + 1) * C)
        b2 = _dot(tri_ref[...], la2[sl], precision=lax.Precision.HIGHEST)
        e = jnp.exp(b2)
        qke = qk_ref[0, sl, :] * e
        q_e = (qke[:, 0:GLA_DK] * scale).astype(BF16)
        k_e = qke[:, GLA_DK:2 * GLA_DK].astype(BF16)
        v = v_ref[0, sl, :].astype(BF16)
        attn = jnp.where(tril, _dot_nt(q_e, k_e), 0.0)
        st = st_ref[...]
        o = _dot(attn.astype(BF16), v) + _dot_nt(q_e, st.astype(BF16))
        decay = e[C - 1:C, 0:GLA_DK]
        st_ref[...] = (st + _dot_tn(v, k_e)) * decay
        y = o * lax.rsqrt(jnp.mean(o * o, axis=-1, keepdims=True) + EPS) * nw_ref[...]
        o_ref[0, sl, :] = (y * _silu(z_ref[0, sl, :])).astype(o_ref.dtype)


def _gla(qkg, vg, zg, misc, wa_big, ba2, norm_w, tri, T):
    B, S, _ = qkg.shape
    hb = pl.BlockSpec((1, T, LANES), lambda b, h, t: (b, t, h))
    return pl.pallas_call(
        _gla_kernel,
        grid=(B, GLA_HEADS, S // T),
        in_specs=[hb, hb, hb,
                  pl.BlockSpec((1, T, LANES), lambda b, h, t: (b, t, 0)),
                  pl.BlockSpec((1, LANES, LANES), lambda b, h, t: (h, 0, 0)),
                  pl.BlockSpec((1, 1, LANES), lambda b, h, t: (h, 0, 0)),
                  pl.BlockSpec(norm_w.shape, lambda b, h, t: (0, 0)),
                  pl.BlockSpec(tri.shape, lambda b, h, t: (0, 0))],
        out_specs=hb,
        out_shape=jax.ShapeDtypeStruct((B, S, GLA_HEADS * GLA_DV), BF16),
        scratch_shapes=[pltpu.VMEM((GLA_DV, GLA_DK), F32)],
        compiler_params=pltpu.CompilerParams(dimension_semantics=("parallel", "parallel", "arbitrary"),
                                             vmem_limit_bytes=VMEM_LIMIT),
        name="gla_chunked",
    )(qkg, vg, zg, misc, wa_big, ba2, norm_w, tri)


def _out_kernel(x_ref, g0_ref, b0_ref, on_ref, zn_ref, og_ref, p_ref, wo_ref, wpg_ref, bpg_ref, wpe_ref,
                lg_ref, lb_ref, o_ref):
    half = on_ref.shape[2]
    h = _layer_norm(x_ref[0], g0_ref[...], b0_ref[...])
    mix_n = (on_ref[0].astype(F32) * _silu(zn_ref[0])).astype(BF16)
    y = _dot(mix_n, wo_ref[0:half, :]) + _dot(og_ref[0], wo_ref[half:2 * half, :])
    r = ALPHA * h + y
    gate = _sigmoid(_dot(r.astype(BF16), wpg_ref[...]) + bpg_ref[...])
    r = r + gate * _dot(p_ref[0].astype(BF16), wpe_ref[...])
    o_ref[0] = _layer_norm(r, lg_ref[...], lb_ref[...])


def _out_projection(x, ln0_g, ln0_b, o_nsa, zn, o_gla, p, w_out, w_pg, b_pg, w_pe, ln_g, ln_b, tm):
    B, S, D = x.shape
    tok = lambda w: pl.BlockSpec((1, tm, w), lambda b, t: (b, t, 0))
    full2 = lambda a: pl.BlockSpec(a.shape, lambda b, t: (0, 0))
    return pl.pallas_call(
        _out_kernel,
        grid=(B, S // tm),
        in_specs=[tok(D), full2(ln0_g), full2(ln0_b), tok(512), tok(512), tok(512), tok(PLE_DIM),
                  full2(w_out), full2(w_pg), full2(b_pg), full2(w_pe), full2(ln_g), full2(ln_b)],
        out_specs=tok(D),
        out_shape=jax.ShapeDtypeStruct((B, S, D), F32),
        compiler_params=pltpu.CompilerParams(dimension_semantics=("parallel", "parallel"),
                                             vmem_limit_bytes=VMEM_LIMIT),
        name="out_proj_deepnorm",
    )(x, ln0_g, ln0_b, o_nsa, zn, o_gla, p, w_out, w_pg, b_pg, w_pe, ln_g, ln_b)


def _permute_w_in(w):
    widths = (512, 128, 128, 128, 128, 128, 128, 24, 512, 256, 256, 512, 16, 512)
    offs = np.concatenate([[0], np.cumsum(widths)])
    (q_n, kc, vc, ks, vs, kw, vw, gates, z_n, q_g, k_g, v_g, a_low, z_g) = [
        w[:, int(offs[k]):int(offs[k + 1])] for k in range(len(widths))]
    kvw = [jnp.concatenate([kw[:, g * DH:(g + 1) * DH], vw[:, g * DH:(g + 1) * DH]], axis=1)
           for g in range(NSA_G)]
    qk = jnp.concatenate([jnp.concatenate([q_g[:, h * GLA_DK:(h + 1) * GLA_DK],
                                           k_g[:, h * GLA_DK:(h + 1) * GLA_DK]], axis=1)
                          for h in range(GLA_HEADS)], axis=1)
    misc = jnp.concatenate([gates, a_low,
                            jnp.zeros((w.shape[0], LANES - gates.shape[1] - a_low.shape[1]), w.dtype)], axis=1)
    cols = [q_n * (DH ** -0.5), kc, vc, ks, vs, kvw[0], kvw[1], z_n, qk, v_g, z_g, misc]
    return jnp.concatenate(cols, axis=1).astype(BF16)


def _overlap(nc_pad, nsel):
    c_start = np.arange(nc_pad) * CMP_STRIDE
    b_start = np.arange(nsel) * SEL_BLOCK
    ov = np.clip(np.minimum(c_start[:, None] + CMP_LEN, b_start[None, :] + SEL_BLOCK)
                 - np.maximum(c_start[:, None], b_start[None, :]), 0, None).astype(np.float32) / CMP_LEN
    ov[nc_pad - 1, :] = 0.0
    return jnp.asarray(ov.T)


def kernel(x, p, ln0_g, ln0_b, rel_bias, w_in, w_a2, b_a, gla_norm_w, pos_cmp, w_ck1, b_ck1, w_ck2,
           w_cv1, b_cv1, w_cv2, w_out, w_pe, w_pg, b_pg, ln_g, ln_b):
    B, S, D = x.shape
    assert D == D_MODEL and S % 512 == 0 and S >= WINDOW + QB and w_in.shape[0] == DEPTH == 1
    nb = S // QB
    nch = S // CMP_STRIDE
    nsel = S // SEL_BLOCK
    row = lambda a: a.reshape(1, -1)

    bias_c, bias_w = _bias_tables(rel_bias, nb, nch)

    w_perm = _permute_w_in(w_in[0])
    (qn, kcr, vcr, ksa, vsat, kvw, kvwt, zn, qkg, vg, zg, misc) = _in_projection(
        x, row(ln0_g), row(ln0_b), w_perm, 256)

    def chunks(a):
        return a.reshape(B, nch, CMP_STRIDE, NSA_G, DH).transpose(0, 3, 1, 2, 4).reshape(
            B, NSA_G, nch, CMP_STRIDE * DH)
    pos8 = jnp.broadcast_to(pos_cmp[0].reshape(1, CMP_LEN * DH), (8, CMP_LEN * DH)).astype(BF16)
    zpad = jnp.zeros((CMP_HIDDEN, DH), F32)
    wk2p = jnp.concatenate([w_ck2[0], zpad], axis=1).astype(BF16)
    wv2p = jnp.concatenate([zpad, w_cv2[0]], axis=1).astype(BF16)
    kvc, kvct = _compress(chunks(kcr), chunks(vcr), pos8, w_ck1[0].astype(BF16), row(b_ck1[0]), wk2p,
                          w_cv1[0].astype(BF16), row(b_cv1[0]), wv2p)

    o_nsa = _nsa(qn, ksa, vsat, kvw, kvwt, kvc, kvct, bias_w, bias_c, misc, _overlap(nch, nsel))

    wa = w_a2[0]
    wa_big = jnp.zeros((GLA_HEADS, LANES, LANES), F32)
    for h in range(GLA_HEADS):
        wh = wa[:, h * GLA_DK:(h + 1) * GLA_DK]
        wa_big = wa_big.at[h, MISC_ALOW:MISC_ALOW + GLA_RANK, :].set(jnp.concatenate([wh, wh], axis=1))
    ba = b_a[0].reshape(GLA_HEADS, 1, GLA_DK)
    ba2 = jnp.concatenate([ba, ba], axis=2)
    tri = jnp.asarray(np.tril(np.ones((GLA_CHUNK, GLA_CHUNK), np.float32)))
    o_gla = _gla(qkg, vg, zg, misc, wa_big.astype(BF16), ba2, row(gla_norm_w[0]), tri, 512)

    return _out_projection(x, row(ln0_g), row(ln0_b), o_nsa, zn, o_gla, p[0], w_out[0].astype(BF16),
                           w_pg[0].astype(BF16), row(b_pg[0]), w_pe[0].astype(BF16), row(ln_g[0]),
                           row(ln_b[0]), 256)
```

```python
import math

import numpy as np
import jax
import jax.numpy as jnp
from jax import lax
from jax.experimental import pallas as pl
from jax.experimental.pallas import tpu as pltpu

F32 = jnp.float32
BF16 = jnp.bfloat16

D_MODEL = 1024
PLE_DIM = 256
NSA_HEADS = 8
NSA_G = 2
NSA_R = NSA_HEADS // NSA_G
DH = 64
CMP_LEN = 32
CMP_STRIDE = 16
CMP_HIDDEN = 256
SEL_BLOCK = 64
TOP_N = 16
N_LOCAL = 2
WINDOW = 512
QB = 128
GLA_HEADS = 4
GLA_DK = 64
GLA_DV = 128
GLA_RANK = 16
GLA_TAU = 16.0
GLA_CHUNK = 64
NUM_BUCKETS = 32
MAX_DISTANCE = 128
DEPTH = 1
ALPHA = (2.0 * DEPTH) ** 0.25
EPS = 1e-5
NEG = -1e30
POS = 1e30

LANES = 128
VMEM_LIMIT = 56 * 1024 * 1024

SEL_TILE = 256
SWEEP_GROUP = 4
VROWS = 80
SUPER = 64 * SEL_BLOCK
TW_WIDTH = WINDOW + QB + WINDOW
ROWS = NSA_R * QB

C_QN, C_KC, C_VC, C_KS, C_VS, C_KVW0, C_KVW1, C_ZN, C_QKG, C_VG, C_ZG, C_MISC, C_END = (
    0, 512, 640, 768, 896, 1024, 1152, 1280, 1792, 2304, 2816, 3328, 3456)
MISC_GATES = 0
MISC_ALOW = 24


def _bucket_thresholds():
    d = np.arange(0, 4 * MAX_DISTANCE)
    max_exact = NUM_BUCKETS // 2
    nf = np.maximum(d, 1).astype(np.float32)
    large = max_exact + (np.log(nf / np.float32(max_exact)) / np.float32(math.log(MAX_DISTANCE / max_exact))
                         * np.float32(NUM_BUCKETS - max_exact)).astype(np.int32)
    large = np.minimum(large, NUM_BUCKETS - 1)
    bucket = np.where(d < max_exact, d, large)
    assert np.all(np.diff(bucket) >= 0) and bucket[-1] == NUM_BUCKETS - 1
    return [int(np.argmax(bucket >= k)) for k in range(NUM_BUCKETS)]


_THR = _bucket_thresholds()
CMP_BAND = 32
assert (QB - 1 + _THR[NUM_BUCKETS - 1]) // CMP_STRIDE + 1 + 7 <= CMP_BAND


def _dot(a, b, **kw):
    return jnp.dot(a, b, preferred_element_type=F32, **kw)


def _dot_nt(a, b, **kw):
    return lax.dot_general(a, b, (((1,), (1,)), ((), ())), preferred_element_type=F32, **kw)


def _dot_tn(a, b, **kw):
    return lax.dot_general(a, b, (((0,), (0,)), ((), ())), preferred_element_type=F32, **kw)


def _layer_norm(x, g, b):
    mu = jnp.mean(x, axis=-1, keepdims=True)
    xc = x - mu
    var = jnp.mean(xc * xc, axis=-1, keepdims=True)
    return xc * lax.rsqrt(var + EPS) * g + b


def _sigmoid(x):
    return 1.0 / (1.0 + jnp.exp(-x))


def _silu(x):
    return x * _sigmoid(x)


def _bias_from_dist(rb_ref, dist, valid, put):
    masks = [dist >= _THR[k] for k in range(1, NUM_BUCKETS)]
    for h in range(NSA_HEADS):
        val = jnp.full(dist.shape, rb_ref[0, h], F32)
        for k in range(1, NUM_BUCKETS):
            val = jnp.where(masks[k - 1], rb_ref[k, h], val)
        val = val - rb_ref[NUM_BUCKETS - 1, h]
        put(h, jnp.where(valid, val, NEG))


def _bias_cmp_kernel(rb_ref, out_ref):
    i = pl.program_id(0)
    nc = out_ref.shape[2]
    big = CMP_STRIDE * QB
    j_zero_max = (QB * i - (CMP_LEN - 1) - _THR[NUM_BUCKETS - 1] + CMP_STRIDE * big) // CMP_STRIDE - big
    start = jnp.clip((j_zero_max + 1 + 8 * big) // 8 * 8 - 8 * big, 0, nc - CMP_BAND)
    start = pl.multiple_of(start, 8)
    j_all = lax.broadcasted_iota(jnp.int32, (nc, QB), 0)
    fill = jnp.where(j_all <= j_zero_max, 0.0, NEG)
    for h in range(NSA_HEADS):
        out_ref[0, h] = fill
    j = start + lax.broadcasted_iota(jnp.int32, (CMP_BAND, QB), 0)
    a = lax.broadcasted_iota(jnp.int32, (CMP_BAND, QB), 1)
    dist = a + QB * i - CMP_STRIDE * j - (CMP_LEN - 1)

    def put(h, v):
        out_ref[0, h, pl.ds(start, CMP_BAND), :] = v
    _bias_from_dist(rb_ref, dist, dist >= 0, put)


def _bias_win_kernel(rb_ref, out_ref):
    shape = out_ref.shape[1:]
    y = lax.broadcasted_iota(jnp.int32, shape, 0)
    a = lax.broadcasted_iota(jnp.int32, shape, 1)
    dist = a + WINDOW - y

    def put(h, v):
        out_ref[h] = v
    _bias_from_dist(rb_ref, dist, (dist >= 0) & (dist < WINDOW), put)


def _bias_tables(rel_bias, nb, nc):
    smem = pl.BlockSpec(memory_space=pltpu.SMEM)
    bias_c = pl.pallas_call(
        _bias_cmp_kernel,
        grid=(nb,),
        in_specs=[smem],
        out_specs=pl.BlockSpec((1, NSA_HEADS, nc, QB), lambda i: (i, 0, 0, 0)),
        out_shape=jax.ShapeDtypeStruct((nb, NSA_HEADS, nc, QB), F32),
        compiler_params=pltpu.CompilerParams(dimension_semantics=("parallel",)),
        name="bias_cmp",
    )(rel_bias)
    bias_w = pl.pallas_call(
        _bias_win_kernel,
        in_specs=[smem],
        out_shape=jax.ShapeDtypeStruct((NSA_HEADS, TW_WIDTH, QB), F32),
        name="bias_win",
    )(rel_bias)
    return bias_c, bias_w


def _inproj_kernel(x_ref, g_ref, b_ref, w_ref, qn_ref, kcr_ref, vcr_ref, ksa_ref, vsat_ref, kvw_ref, kvwt_ref,
                   zn_ref, qkg_ref, vg_ref, zg_ref, misc_ref):
    t = pl.program_id(1)
    tm = x_ref.shape[1]
    h = _layer_norm(x_ref[0], g_ref[...], b_ref[...]).astype(BF16)

    def mm(c0, c1):
        return _dot(h, w_ref[:, c0:c1])

    qn_ref[0] = mm(C_QN, C_KC).astype(BF16)
    kcr_ref[0] = mm(C_KC, C_VC).astype(BF16)
    vcr_ref[0] = mm(C_VC, C_KS).astype(BF16)
    ks = mm(C_KS, C_VS)
    vs = mm(C_VS, C_KVW0)
    lane = lax.broadcasted_iota(jnp.int32, (tm, LANES), 1)
    row = lax.broadcasted_iota(jnp.int32, (tm, LANES), 0)
    low = lane < DH
    blk = ((t * tm + row) // SEL_BLOCK) % (SUPER // SEL_BLOCK)
    onehot = jnp.where(lane - DH == blk, 1.0, 0.0)
    ones_col = jnp.where(lane == DH, 1.0, 0.0)
    ksa_ref[0, 0] = jnp.where(low, ks, onehot).astype(BF16)
    ksa_ref[0, 1] = jnp.where(low, pltpu.roll(ks, DH, 1), onehot).astype(BF16)
    vsat_ref[0, 0] = jnp.where(low, vs, ones_col).T[0:VROWS].astype(BF16)
    vsat_ref[0, 1] = jnp.where(low, pltpu.roll(vs, DH, 1), ones_col).T[0:VROWS].astype(BF16)
    for g, (c0, c1) in enumerate(((C_KVW0, C_KVW1), (C_KVW1, C_ZN))):
        kvw = mm(c0, c1)
        kvw_ref[0, g] = kvw.astype(BF16)
        kvwt_ref[0, g] = kvw.T.astype(BF16)
    zn_ref[0] = mm(C_ZN, C_QKG)
    qkg_ref[0] = mm(C_QKG, C_VG)
    vg_ref[0] = mm(C_VG, C_ZG)
    zg_ref[0] = mm(C_ZG, C_MISC)
    misc_ref[0] = mm(C_MISC, C_END)


def _in_projection(x, ln0_g, ln0_b, w_perm, tm):
    B, S, D = x.shape
    tok = lambda w: pl.BlockSpec((1, tm, w), lambda b, t: (b, t, 0))
    grp = pl.BlockSpec((1, NSA_G, tm, LANES), lambda b, t: (b, 0, t, 0))
    grp_t = pl.BlockSpec((1, NSA_G, LANES, tm), lambda b, t: (b, 0, 0, t))
    grp_v = pl.BlockSpec((1, NSA_G, VROWS, tm), lambda b, t: (b, 0, 0, t))
    full2 = lambda a: pl.BlockSpec(a.shape, lambda b, t: (0, 0))
    sds = jax.ShapeDtypeStruct
    return pl.pallas_call(
        _inproj_kernel,
        grid=(B, S // tm),
        in_specs=[tok(D), full2(ln0_g), full2(ln0_b), full2(w_perm)],
        out_specs=[tok(512), tok(LANES), tok(LANES), grp, grp_v, grp, grp_t, tok(512), tok(512), tok(512),
                   tok(512), tok(LANES)],
        out_shape=[sds((B, S, 512), BF16), sds((B, S, LANES), BF16), sds((B, S, LANES), BF16),
                   sds((B, NSA_G, S, LANES), BF16), sds((B, NSA_G, VROWS, S), BF16),
                   sds((B, NSA_G, S, LANES), BF16), sds((B, NSA_G, LANES, S), BF16),
                   sds((B, S, 512), F32), sds((B, S, 512), F32), sds((B, S, 512), F32), sds((B, S, 512), F32),
                   sds((B, S, LANES), F32)],
        compiler_params=pltpu.CompilerParams(dimension_semantics=("parallel", "parallel"),
                                             vmem_limit_bytes=VMEM_LIMIT),
        name="ln0_inproj",
    )(x, ln0_g, ln0_b, w_perm)


def _gelu_tanh(x):
    c = math.sqrt(2.0 / math.pi)
    return x * (0.5 * (1.0 + jnp.tanh(c * (x + 0.044715 * (x * x * x)))))


def _compress_kernel(xk_ref, xv_ref, pos_ref, wk1_ref, bk1_ref, wk2_ref, wv1_ref, bv1_ref, wv2_ref,
                     out_ref, outt_ref):
    half = CMP_STRIDE * DH
    nch = xk_ref.shape[2]

    def branch(x_ref, w1_ref, b1_ref, w2_ref):
        x = x_ref[0, 0]
        top = _dot(x, w1_ref[0:half, :])
        bot = _dot(x, w1_ref[half:2 * half, :])
        cpos = _dot(pos_ref[...], w1_ref[...])[0:1, :]
        pre = top + pltpu.roll(bot, nch - 1, 0) + cpos + b1_ref[...]
        return _dot(_gelu_tanh(pre).astype(BF16), w2_ref[...])

    out = branch(xk_ref, wk1_ref, bk1_ref, wk2_ref) + branch(xv_ref, wv1_ref, bv1_ref, wv2_ref)
    out_ref[0, 0] = out.astype(BF16)
    outt_ref[0, 0] = out.T.astype(BF16)


def _compress(xk, xv, pos8, wk1, bk1, wk2p, wv1, bv1, wv2p):
    B, G, nch, width = xk.shape
    blk = pl.BlockSpec((1, 1, nch, width), lambda b, g: (b, g, 0, 0))
    full2 = lambda a: pl.BlockSpec(a.shape, lambda b, g: (0, 0))
    return pl.pallas_call(
        _compress_kernel,
        grid=(B, G),
        in_specs=[blk, blk, full2(pos8), full2(wk1), full2(bk1), full2(wk2p), full2(wv1), full2(bv1),
                  full2(wv2p)],
        out_specs=[pl.BlockSpec((1, 1, nch, LANES), lambda b, g: (b, g, 0, 0)),
                   pl.BlockSpec((1, 1, LANES, nch), lambda b, g: (b, g, 0, 0))],
        out_shape=[jax.ShapeDtypeStruct((B, G, nch, LANES), BF16),
                   jax.ShapeDtypeStruct((B, G, LANES, nch), BF16)],
        compiler_params=pltpu.CompilerParams(dimension_semantics=("parallel", "parallel"),
                                             vmem_limit_bytes=VMEM_LIMIT),
        name="kv_compress",
    )(xk, xv, pos8, wk1, bk1, wk2p, wv1, bv1, wv2p)


def _heads_on_lanes(t4):
    return jnp.concatenate([t4[r] for r in range(NSA_R)], axis=1)


def _nsa_kernel(q_ref, ksa_ref, vsat_ref, kvw_ref, kvwt_ref, kvc_ref, kvct_ref, tw_ref, bc_ref, misc_ref,
                o_ref, qa_ref, m_ref, acc_ref, s_ref, ps_ref):
    g = pl.program_id(1)
    i = pl.program_id(2)
    t0 = i * QB

    qt = q_ref[0].astype(F32).T
    q_t = jnp.concatenate([qt[r * DH:(r + 1) * DH] for r in range(NSA_R)], axis=1)
    qpad_t = jnp.concatenate([q_t, jnp.zeros((DH, ROWS), F32)], axis=0).astype(BF16)

    bc = _heads_on_lanes(bc_ref[0])
    s = _dot(kvc_ref[0, 0], qpad_t) + bc
    valid = bc > 0.5 * NEG
    m = jnp.max(s, axis=0, keepdims=True)
    p = jnp.where(valid, jnp.exp(s - m), 0.0)
    l = jnp.sum(p, axis=0, keepdims=True)
    pn = p * jnp.where(l > 0.0, 1.0 / l, 0.0)
    o_c = _dot(kvct_ref[0, 0], pn.astype(BF16))[DH:2 * DH]

    wlen = WINDOW + QB
    start = pl.multiple_of(jnp.maximum(t0 - WINDOW, 0), LANES)
    offw = pl.multiple_of(start - t0 + WINDOW, LANES)
    sw = _dot(kvw_ref[0, 0, pl.ds(start, wlen), :], qpad_t) + _heads_on_lanes(tw_ref[:, pl.ds(offw, wlen), :])
    mw = jnp.max(sw, axis=0, keepdims=True)
    pw = jnp.exp(sw - mw)
    lw = jnp.sum(pw, axis=0, keepdims=True)
    o_w = _dot(kvwt_ref[0, 0, :, pl.ds(start, wlen)], pw.astype(BF16))[DH:2 * DH] / lw

    nc = pn.shape[0]
    nsel = nc * CMP_STRIDE // SEL_BLOCK
    per_blk = SEL_BLOCK // CMP_STRIDE
    ps_ref[0:8] = jnp.zeros((8, QB), F32)
    ps_ref[8:8 + nc] = pn[:, 0:QB] + pn[:, QB:2 * QB] + pn[:, 2 * QB:3 * QB] + pn[:, 3 * QB:4 * QB]
    tok = lambda k: ps_ref[pl.ds(8 + k, nsel, stride=per_blk), :]
    imp = 0.5 * (tok(-1) + tok(3)) + tok(0) + tok(1) + tok(2)
    nidx = lax.broadcasted_iota(jnp.int32, (nsel, QB), 0)
    qidx = lax.broadcasted_iota(jnp.int32, (nsel, QB), 1)
    cur = (t0 + qidx) // SEL_BLOCK
    causal = nidx <= cur
    forced = causal & ((nidx == 0) | (nidx >= cur - (N_LOCAL - 1)))
    w = jnp.where(forced, POS, jnp.where(causal, imp, NEG))
    nf = nidx.astype(F32)
    sel = jnp.zeros((nsel, QB), F32)
    for _ in range(min(TOP_N, nsel)):
        mx = jnp.max(w, axis=0, keepdims=True)
        first = jnp.min(jnp.where(w == mx, nf, float(nsel)), axis=0, keepdims=True)
        pick = nf == first
        sel = jnp.where(pick, 1.0, sel)
        w = jnp.where(pick, -3e38, w)
    sel_bias = jnp.where((sel > 0.5) & causal, 0.0, NEG)

    for u in range(qa_ref.shape[0]):
        sb = sel_bias[u * 64:(u + 1) * 64]
        if sb.shape[0] < 64:
            sb = jnp.concatenate([sb, jnp.full((64 - sb.shape[0], QB), NEG, F32)], axis=0)
        sb4 = jnp.concatenate([sb] * NSA_R, axis=1)
        qa_ref[u] = jnp.concatenate([q_t, sb4], axis=0).astype(BF16)

    jd1 = i // 2
    jd0 = jnp.maximum((i + 1) // 2 - 1, 0)

    def score_tile(j, ntiles, bias_off):
        n = ntiles * SEL_TILE
        k0 = pl.multiple_of(j * SEL_TILE, SEL_TILE)
        sc = _dot(ksa_ref[0, 0, pl.ds(k0, n), :], qa_ref[j // (SUPER // SEL_TILE)])
        if bias_off is not None:
            off = pl.multiple_of(bias_off, LANES)
            sc = sc + _heads_on_lanes(tw_ref[:, pl.ds(off, n), :])
        s_ref[pl.ds(k0, n), :] = sc.astype(s_ref.dtype)
        return jnp.max(sc, axis=0, keepdims=True)

    def far_group(jj, mm):
        for h in range(SWEEP_GROUP // 2):
            mm = jnp.maximum(mm, score_tile(SWEEP_GROUP * jj + 2 * h, 2, None))
        return mm

    n_grp = jd0 // SWEEP_GROUP
    m_far = lax.fori_loop(0, n_grp, far_group, jnp.full((1, ROWS), NEG, F32))
    m_far = lax.fori_loop(n_grp * SWEEP_GROUP, jd0, lambda j, mm: jnp.maximum(mm, score_tile(j, 1, None)), m_far)
    m_ref[...] = jnp.maximum(m_far, score_tile(jd1, 1, WINDOW - (t0 - jd1 * SEL_TILE)))

    @pl.when(jd0 < jd1)
    def _():
        m_ref[...] = jnp.maximum(m_ref[...], score_tile(jd0, 1, WINDOW - (t0 - jd0 * SEL_TILE)))

    m_sel = m_ref[...].astype(BF16)

    def pv_tile(j, ntiles):
        n = ntiles * SEL_TILE
        k0 = pl.multiple_of(j * SEL_TILE, SEL_TILE)
        pt = jnp.exp(s_ref[pl.ds(k0, n), :] - m_sel)
        return _dot(vsat_ref[0, 0, :, pl.ds(k0, n)], pt)

    acc_ref[...] = pv_tile(jd1, 1)

    def pv_group(jj, carry):
        part = pv_tile(SWEEP_GROUP * jj, 2)
        for h in range(1, SWEEP_GROUP // 2):
            part = part + pv_tile(SWEEP_GROUP * jj + 2 * h, 2)
        acc_ref[...] += part
        return carry

    def pv_single(j, carry):
        acc_ref[...] += pv_tile(j, 1)
        return carry
    n_grp2 = jd1 // SWEEP_GROUP
    lax.fori_loop(0, n_grp2, pv_group, 0)
    lax.fori_loop(n_grp2 * SWEEP_GROUP, jd1, pv_single, 0)

    acc = acc_ref[...]
    o_s = acc[0:DH] / acc[DH:DH + 1]

    sg = _sigmoid(misc_ref[0]).T
    outs = []
    for r in range(NSA_R):
        def gate(c):
            c0 = MISC_GATES + 3 * r + c
            c1 = c0 + 3 * NSA_R
            return jnp.where(g == 0, sg[c0:c0 + 1], sg[c1:c1 + 1])
        sl = slice(r * QB, (r + 1) * QB)
        outs.append(gate(0) * o_c[:, sl] + gate(1) * o_s[:, sl] + gate(2) * o_w[:, sl])
    o_ref[0] = jnp.concatenate(outs, axis=0).T.astype(o_ref.dtype)


def _nsa(qn, ksa, vsat, kvw, kvwt, kvc, kvct, bias_w, bias_c, misc):
    B, S, _ = qn.shape
    nb = S // QB
    nc = kvc.shape[2]
    n_super = -(-S // SUPER)
    seq = lambda a: pl.BlockSpec((1, 1) + a.shape[2:], lambda b, g, i: (b, g, 0, 0))
    return pl.pallas_call(
        _nsa_kernel,
        grid=(B, NSA_G, nb),
        in_specs=[
            pl.BlockSpec((1, QB, NSA_R * DH), lambda b, g, i: (b, i, g)),
            seq(ksa), seq(vsat), seq(kvw), seq(kvwt), seq(kvc), seq(kvct),
            pl.BlockSpec((NSA_R, TW_WIDTH, QB), lambda b, g, i: (g, 0, 0)),
            pl.BlockSpec((1, NSA_R, nc, QB), lambda b, g, i: (i, g, 0, 0)),
            pl.BlockSpec((1, QB, LANES), lambda b, g, i: (b, i, 0)),
        ],
        out_specs=pl.BlockSpec((1, QB, NSA_R * DH), lambda b, g, i: (b, i, g)),
        out_shape=jax.ShapeDtypeStruct((B, S, NSA_HEADS * DH), BF16),
        scratch_shapes=[pltpu.VMEM((n_super, LANES, ROWS), BF16),
                        pltpu.VMEM((1, ROWS), F32),
                        pltpu.VMEM((VROWS, ROWS), F32),
                        pltpu.VMEM((S, ROWS), BF16),
                        pltpu.VMEM((nc + 8, QB), F32)],
        compiler_params=pltpu.CompilerParams(dimension_semantics=("parallel", "parallel", "arbitrary"),
                                             vmem_limit_bytes=VMEM_LIMIT),
        name="nsa_attention",
    )(qn, ksa, vsat, kvw, kvwt, kvc, kvct, bias_w, bias_c, misc)


def _gla_kernel(qk_ref, v_ref, z_ref, misc_ref, wa_ref, ba_ref, nw_ref, tri_ref, o_ref, st_ref):
    t = pl.program_id(2)
    T = qk_ref.shape[1]
    C = GLA_CHUNK

    @pl.when(t == 0)
    def _():
        st_ref[...] = jnp.zeros(st_ref.shape, F32)

    lane = lax.broadcasted_iota(jnp.int32, (1, LANES), 1)
    sign = jnp.where(lane < GLA_DK, 1.0 / GLA_TAU, -1.0 / GLA_TAU)
    zz = _dot(misc_ref[0].astype(BF16), wa_ref[0]) + ba_ref[0]
    log_sig = jnp.minimum(zz, 0.0) - jnp.log(1.0 + jnp.exp(-jnp.abs(zz)))
    la2 = log_sig * sign
    rr = lax.broadcasted_iota(jnp.int32, (C, C), 0)
    cc = lax.broadcasted_iota(jnp.int32, (C, C), 1)
    tril = rr >= cc
    scale = GLA_DK ** -0.5

    chunks = [slice(n * C, (n + 1) * C) for n in range(T // C)]
    hi = la2.astype(BF16)
    rem = la2 - hi.astype(F32)
    mid = rem.astype(BF16)
    lo = (rem - mid.astype(F32)).astype(BF16)
    tri = tri_ref[...]
    b2 = jnp.concatenate([_dot(tri, hi[sl]) + _dot(tri, mid[sl]) + _dot(tri, lo[sl]) for sl in chunks], axis=0)
    e = jnp.exp(b2)
    qke = qk_ref[0] * e
    q_e = (qke[:, 0:GLA_DK] * scale).astype(BF16)
    k_e = qke[:, GLA_DK:2 * GLA_DK].astype(BF16)
    v = v_ref[0].astype(BF16)
    attn = [jnp.where(tril, _dot_nt(q_e[sl], k_e[sl]), 0.0).astype(BF16) for sl in chunks]
    o_intra = [_dot(a, v[sl]) for a, sl in zip(attn, chunks)]
    kv = [_dot_tn(v[sl], k_e[sl]) for sl in chunks]
    st = st_ref[...]
    states = []
    for n, sl in enumerate(chunks):
        states.append(st.astype(BF16))
        st = (st + kv[n]) * e[sl.stop - 1:sl.stop, 0:GLA_DK]
    st_ref[...] = st
    o = jnp.concatenate([oi + _dot_nt(q_e[sl], s_in) for oi, sl, s_in in zip(o_intra, chunks, states)], axis=0)
    y = o * lax.rsqrt(jnp.mean(o * o, axis=-1, keepdims=True) + EPS) * nw_ref[...]
    o_ref[0] = (y * _silu(z_ref[0])).astype(o_ref.dtype)


def _gla(qkg, vg, zg, misc, wa_big, ba2, norm_w, tri, T):
    B, S, _ = qkg.shape
    hb = pl.BlockSpec((1, T, LANES), lambda b, h, t: (b, t, h))
    return pl.pallas_call(
        _gla_kernel,
        grid=(B, GLA_HEADS, S // T),
        in_specs=[hb, hb, hb,
                  pl.BlockSpec((1, T, LANES), lambda b, h, t: (b, t, 0)),
                  pl.BlockSpec((1, LANES, LANES), lambda b, h, t: (h, 0, 0)),
                  pl.BlockSpec((1, 1, LANES), lambda b, h, t: (h, 0, 0)),
                  pl.BlockSpec(norm_w.shape, lambda b, h, t: (0, 0)),
                  pl.BlockSpec(tri.shape, lambda b, h, t: (0, 0))],
        out_specs=hb,
        out_shape=jax.ShapeDtypeStruct((B, S, GLA_HEADS * GLA_DV), BF16),
        scratch_shapes=[pltpu.VMEM((GLA_DV, GLA_DK), F32)],
        compiler_params=pltpu.CompilerParams(dimension_semantics=("parallel", "parallel", "arbitrary"),
                                             vmem_limit_bytes=VMEM_LIMIT),
        name="gla_chunked",
    )(qkg, vg, zg, misc, wa_big, ba2, norm_w, tri)


def _out_kernel(x_ref, g0_ref, b0_ref, on_ref, zn_ref, og_ref, p_ref, wo_ref, wpg_ref, bpg_ref, wpe_ref,
                lg_ref, lb_ref, o_ref):
    half = on_ref.shape[2]
    h = _layer_norm(x_ref[0], g0_ref[...], b0_ref[...])
    mix_n = (on_ref[0].astype(F32) * _silu(zn_ref[0])).astype(BF16)
    y = _dot(mix_n, wo_ref[0:half, :]) + _dot(og_ref[0], wo_ref[half:2 * half, :])
    r = ALPHA * h + y
    gate = _sigmoid(_dot(r.astype(BF16), wpg_ref[...]) + bpg_ref[...])
    r = r + gate * _dot(p_ref[0].astype(BF16), wpe_ref[...])
    o_ref[0] = _layer_norm(r, lg_ref[...], lb_ref[...])


def _out_projection(x, ln0_g, ln0_b, o_nsa, zn, o_gla, p, w_out, w_pg, b_pg, w_pe, ln_g, ln_b, tm):
    B, S, D = x.shape
    tok = lambda w: pl.BlockSpec((1, tm, w), lambda b, t: (b, t, 0))
    full2 = lambda a: pl.BlockSpec(a.shape, lambda b, t: (0, 0))
    return pl.pallas_call(
        _out_kernel,
        grid=(B, S // tm),
        in_specs=[tok(D), full2(ln0_g), full2(ln0_b), tok(512), tok(512), tok(512), tok(PLE_DIM),
                  full2(w_out), full2(w_pg), full2(b_pg), full2(w_pe), full2(ln_g), full2(ln_b)],
        out_specs=tok(D),
        out_shape=jax.ShapeDtypeStruct((B, S, D), F32),
        compiler_params=pltpu.CompilerParams(dimension_semantics=("parallel", "parallel"),
                                             vmem_limit_bytes=VMEM_LIMIT),
        name="out_proj_deepnorm",
    )(x, ln0_g, ln0_b, o_nsa, zn, o_gla, p, w_out, w_pg, b_pg, w_pe, ln_g, ln_b)


def _permute_w_in(w):
    widths = (512, 128, 128, 128, 128, 128, 128, 24, 512, 256, 256, 512, 16, 512)
    offs = np.concatenate([[0], np.cumsum(widths)])
    (q_n, kc, vc, ks, vs, kw, vw, gates, z_n, q_g, k_g, v_g, a_low, z_g) = [
        w[:, int(offs[k]):int(offs[k + 1])] for k in range(len(widths))]
    kvw = [jnp.concatenate([kw[:, g * DH:(g + 1) * DH], vw[:, g * DH:(g + 1) * DH]], axis=1)
           for g in range(NSA_G)]
    qk = jnp.concatenate([jnp.concatenate([q_g[:, h * GLA_DK:(h + 1) * GLA_DK],
                                           k_g[:, h * GLA_DK:(h + 1) * GLA_DK]], axis=1)
                          for h in range(GLA_HEADS)], axis=1)
    misc = jnp.concatenate([gates, a_low,
                            jnp.zeros((w.shape[0], LANES - gates.shape[1] - a_low.shape[1]), w.dtype)], axis=1)
    cols = [q_n * (DH ** -0.5), kc, vc, ks, vs, kvw[0], kvw[1], z_n, qk, v_g, z_g, misc]
    return jnp.concatenate(cols, axis=1).astype(BF16)


def kernel(x, p, ln0_g, ln0_b, rel_bias, w_in, w_a2, b_a, gla_norm_w, pos_cmp, w_ck1, b_ck1, w_ck2,
           w_cv1, b_cv1, w_cv2, w_out, w_pe, w_pg, b_pg, ln_g, ln_b):
    B, S, D = x.shape
    assert D == D_MODEL and S % 512 == 0 and S >= WINDOW + QB and w_in.shape[0] == DEPTH == 1
    nb = S // QB
    nch = S // CMP_STRIDE
    nsel = S // SEL_BLOCK
    row = lambda a: a.reshape(1, -1)

    bias_c, bias_w = _bias_tables(rel_bias, nb, nch)

    w_perm = _permute_w_in(w_in[0])
    (qn, kcr, vcr, ksa, vsat, kvw, kvwt, zn, qkg, vg, zg, misc) = _in_projection(
        x, row(ln0_g), row(ln0_b), w_perm, 256)

    def chunks(a):
        return a.reshape(B, nch, CMP_STRIDE, NSA_G, DH).transpose(0, 3, 1, 2, 4).reshape(
            B, NSA_G, nch, CMP_STRIDE * DH)
    pos8 = jnp.broadcast_to(pos_cmp[0].reshape(1, CMP_LEN * DH), (8, CMP_LEN * DH)).astype(BF16)
    zpad = jnp.zeros((CMP_HIDDEN, DH), F32)
    wk2p = jnp.concatenate([w_ck2[0], zpad], axis=1).astype(BF16)
    wv2p = jnp.concatenate([zpad, w_cv2[0]], axis=1).astype(BF16)
    kvc, kvct = _compress(chunks(kcr), chunks(vcr), pos8, w_ck1[0].astype(BF16), row(b_ck1[0]), wk2p,
                          w_cv1[0].astype(BF16), row(b_cv1[0]), wv2p)

    o_nsa = _nsa(qn, ksa, vsat, kvw, kvwt, kvc, kvct, bias_w, bias_c, misc)

    wa = w_a2[0]
    wa_big = jnp.zeros((GLA_HEADS, LANES, LANES), F32)
    for h in range(GLA_HEADS):
        wh = wa[:, h * GLA_DK:(h + 1) * GLA_DK]
        wa_big = wa_big.at[h, MISC_ALOW:MISC_ALOW + GLA_RANK, :].set(jnp.concatenate([wh, wh], axis=1))
    ba = b_a[0].reshape(GLA_HEADS, 1, GLA_DK)
    ba2 = jnp.concatenate([ba, ba], axis=2)
    tri = jnp.asarray(np.tril(np.ones((GLA_CHUNK, GLA_CHUNK), np.float32))).astype(BF16)
    o_gla = _gla(qkg, vg, zg, misc, wa_big.astype(BF16), ba2, row(gla_norm_w[0]), tri, 512)

    return _out_projection(x, row(ln0_g), row(ln0_b), o_nsa, zn, o_gla, p[0], w_out[0].astype(BF16),
                           w_pg[0].astype(BF16), row(b_pg[0]), w_pe[0].astype(BF16), row(ln_g[0]),
                           row(ln_b[0]), 256)
```

```python
import math

import numpy as np
import jax
import jax.numpy as jnp
from jax import lax
from jax.experimental import pallas as pl
from jax.experimental.pallas import tpu as pltpu

F32 = jnp.float32
BF16 = jnp.bfloat16

D_MODEL = 1024
PLE_DIM = 256
NSA_HEADS = 8
NSA_G = 2
NSA_R = NSA_HEADS // NSA_G
DH = 64
CMP_LEN = 32
CMP_STRIDE = 16
CMP_HIDDEN = 256
SEL_BLOCK = 64
TOP_N = 16
N_LOCAL = 2
WINDOW = 512
QB = 128
GLA_HEADS = 4
GLA_DK = 64
GLA_DV = 128
GLA_RANK = 16
GLA_TAU = 16.0
GLA_CHUNK = 64
NUM_BUCKETS = 32
MAX_DISTANCE = 128
DEPTH = 1
ALPHA = (2.0 * DEPTH) ** 0.25
EPS = 1e-5
NEG = -1e30
POS = 1e30

LANES = 128
VMEM_LIMIT = 56 * 1024 * 1024

SEL_TILE = 256
SWEEP_GROUP = 8
VROWS = 80
SUPER = 64 * SEL_BLOCK
TW_WIDTH = WINDOW + QB + WINDOW
ROWS = NSA_R * QB

C_QN, C_KC, C_VC, C_KS, C_VS, C_KVW0, C_KVW1, C_ZN, C_QKG, C_VG, C_ZG, C_MISC, C_END = (
    0, 512, 640, 768, 896, 1024, 1152, 1280, 1792, 2304, 2816, 3328, 3456)
MISC_GATES = 0
MISC_ALOW = 24


def _bucket_thresholds():
    d = np.arange(0, 4 * MAX_DISTANCE)
    max_exact = NUM_BUCKETS // 2
    nf = np.maximum(d, 1).astype(np.float32)
    large = max_exact + (np.log(nf / np.float32(max_exact)) / np.float32(math.log(MAX_DISTANCE / max_exact))
                         * np.float32(NUM_BUCKETS - max_exact)).astype(np.int32)
    large = np.minimum(large, NUM_BUCKETS - 1)
    bucket = np.where(d < max_exact, d, large)
    assert np.all(np.diff(bucket) >= 0) and bucket[-1] == NUM_BUCKETS - 1
    return [int(np.argmax(bucket >= k)) for k in range(NUM_BUCKETS)]


_THR = _bucket_thresholds()
CMP_BAND = 32
assert (QB - 1 + _THR[NUM_BUCKETS - 1]) // CMP_STRIDE + 1 + 7 <= CMP_BAND


def _dot(a, b, **kw):
    return jnp.dot(a, b, preferred_element_type=F32, **kw)


def _dot_nt(a, b, **kw):
    return lax.dot_general(a, b, (((1,), (1,)), ((), ())), preferred_element_type=F32, **kw)


def _dot_tn(a, b, **kw):
    return lax.dot_general(a, b, (((0,), (0,)), ((), ())), preferred_element_type=F32, **kw)


def _layer_norm(x, g, b):
    mu = jnp.mean(x, axis=-1, keepdims=True)
    xc = x - mu
    var = jnp.mean(xc * xc, axis=-1, keepdims=True)
    return xc * lax.rsqrt(var + EPS) * g + b


def _sigmoid(x):
    return 1.0 / (1.0 + jnp.exp(-x))


def _silu(x):
    return x * _sigmoid(x)


def _bias_from_dist(rb_ref, dist, valid, put):
    masks = [dist >= _THR[k] for k in range(1, NUM_BUCKETS)]
    for h in range(NSA_HEADS):
        val = jnp.full(dist.shape, rb_ref[0, h], F32)
        for k in range(1, NUM_BUCKETS):
            val = jnp.where(masks[k - 1], rb_ref[k, h], val)
        val = val - rb_ref[NUM_BUCKETS - 1, h]
        put(h, jnp.where(valid, val, NEG))


def _bias_cmp_kernel(rb_ref, out_ref):
    i = pl.program_id(0)
    nc = out_ref.shape[2]
    big = CMP_STRIDE * QB
    j_zero_max = (QB * i - (CMP_LEN - 1) - _THR[NUM_BUCKETS - 1] + CMP_STRIDE * big) // CMP_STRIDE - big
    start = jnp.clip((j_zero_max + 1 + 8 * big) // 8 * 8 - 8 * big, 0, nc - CMP_BAND)
    start = pl.multiple_of(start, 8)
    j_all = lax.broadcasted_iota(jnp.int32, (nc, QB), 0)
    fill = jnp.where(j_all <= j_zero_max, 0.0, NEG)
    for h in range(NSA_HEADS):
        out_ref[0, h] = fill
    j = start + lax.broadcasted_iota(jnp.int32, (CMP_BAND, QB), 0)
    a = lax.broadcasted_iota(jnp.int32, (CMP_BAND, QB), 1)
    dist = a + QB * i - CMP_STRIDE * j - (CMP_LEN - 1)

    def put(h, v):
        out_ref[0, h, pl.ds(start, CMP_BAND), :] = v
    _bias_from_dist(rb_ref, dist, dist >= 0, put)


def _bias_win_kernel(rb_ref, out_ref):
    shape = out_ref.shape[1:]
    y = lax.broadcasted_iota(jnp.int32, shape, 0)
    a = lax.broadcasted_iota(jnp.int32, shape, 1)
    dist = a + WINDOW - y

    def put(h, v):
        out_ref[h] = v
    _bias_from_dist(rb_ref, dist, (dist >= 0) & (dist < WINDOW), put)


def _bias_tables(rel_bias, nb, nc):
    smem = pl.BlockSpec(memory_space=pltpu.SMEM)
    bias_c = pl.pallas_call(
        _bias_cmp_kernel,
        grid=(nb,),
        in_specs=[smem],
        out_specs=pl.BlockSpec((1, NSA_HEADS, nc, QB), lambda i: (i, 0, 0, 0)),
        out_shape=jax.ShapeDtypeStruct((nb, NSA_HEADS, nc, QB), F32),
        compiler_params=pltpu.CompilerParams(dimension_semantics=("parallel",)),
        name="bias_cmp",
    )(rel_bias)
    bias_w = pl.pallas_call(
        _bias_win_kernel,
        in_specs=[smem],
        out_shape=jax.ShapeDtypeStruct((NSA_HEADS, TW_WIDTH, QB), F32),
        name="bias_win",
    )(rel_bias)
    return bias_c, bias_w


def _inproj_kernel(x_ref, g_ref, b_ref, w_ref, qn_ref, kcr_ref, vcr_ref, ksa_ref, vsat_ref, kvw_ref, kvwt_ref,
                   zn_ref, qkg_ref, vg_ref, zg_ref, misc_ref):
    t = pl.program_id(1)
    tm = x_ref.shape[1]
    h = _layer_norm(x_ref[0], g_ref[...], b_ref[...]).astype(BF16)

    def mm(c0, c1):
        return _dot(h, w_ref[:, c0:c1])

    qn_ref[0] = mm(C_QN, C_KC).astype(BF16)
    kcr_ref[0] = mm(C_KC, C_VC).astype(BF16)
    vcr_ref[0] = mm(C_VC, C_KS).astype(BF16)
    ks = mm(C_KS, C_VS)
    vs = mm(C_VS, C_KVW0)
    lane = lax.broadcasted_iota(jnp.int32, (tm, LANES), 1)
    row = lax.broadcasted_iota(jnp.int32, (tm, LANES), 0)
    low = lane < DH
    blk = ((t * tm + row) // SEL_BLOCK) % (SUPER // SEL_BLOCK)
    onehot = jnp.where(lane - DH == blk, 1.0, 0.0)
    ones_col = jnp.where(lane == DH, 1.0, 0.0)
    ksa_ref[0, 0] = jnp.where(low, ks, onehot).astype(BF16)
    ksa_ref[0, 1] = jnp.where(low, pltpu.roll(ks, DH, 1), onehot).astype(BF16)
    vsat_ref[0, 0] = jnp.where(low, vs, ones_col).T[0:VROWS].astype(BF16)
    vsat_ref[0, 1] = jnp.where(low, pltpu.roll(vs, DH, 1), ones_col).T[0:VROWS].astype(BF16)
    for g, (c0, c1) in enumerate(((C_KVW0, C_KVW1), (C_KVW1, C_ZN))):
        kvw = mm(c0, c1)
        kvw_ref[0, g] = kvw.astype(BF16)
        kvwt_ref[0, g] = kvw.T.astype(BF16)
    zn_ref[0] = mm(C_ZN, C_QKG)
    qkg_ref[0] = mm(C_QKG, C_VG)
    vg_ref[0] = mm(C_VG, C_ZG)
    zg_ref[0] = mm(C_ZG, C_MISC)
    misc_ref[0] = mm(C_MISC, C_END)


def _in_projection(x, ln0_g, ln0_b, w_perm, tm):
    B, S, D = x.shape
    tok = lambda w: pl.BlockSpec((1, tm, w), lambda b, t: (b, t, 0))
    grp = pl.BlockSpec((1, NSA_G, tm, LANES), lambda b, t: (b, 0, t, 0))
    grp_t = pl.BlockSpec((1, NSA_G, LANES, tm), lambda b, t: (b, 0, 0, t))
    grp_v = pl.BlockSpec((1, NSA_G, VROWS, tm), lambda b, t: (b, 0, 0, t))
    full2 = lambda a: pl.BlockSpec(a.shape, lambda b, t: (0, 0))
    sds = jax.ShapeDtypeStruct
    return pl.pallas_call(
        _inproj_kernel,
        grid=(B, S // tm),
        in_specs=[tok(D), full2(ln0_g), full2(ln0_b), full2(w_perm)],
        out_specs=[tok(512), tok(LANES), tok(LANES), grp, grp_v, grp, grp_t, tok(512), tok(512), tok(512),
                   tok(512), tok(LANES)],
        out_shape=[sds((B, S, 512), BF16), sds((B, S, LANES), BF16), sds((B, S, LANES), BF16),
                   sds((B, NSA_G, S, LANES), BF16), sds((B, NSA_G, VROWS, S), BF16),
                   sds((B, NSA_G, S, LANES), BF16), sds((B, NSA_G, LANES, S), BF16),
                   sds((B, S, 512), F32), sds((B, S, 512), F32), sds((B, S, 512), F32), sds((B, S, 512), F32),
                   sds((B, S, LANES), F32)],
        compiler_params=pltpu.CompilerParams(dimension_semantics=("parallel", "parallel"),
                                             vmem_limit_bytes=VMEM_LIMIT),
        name="ln0_inproj",
    )(x, ln0_g, ln0_b, w_perm)


def _gelu_tanh(x):
    c = math.sqrt(2.0 / math.pi)
    return x * (0.5 * (1.0 + jnp.tanh(c * (x + 0.044715 * (x * x * x)))))


def _compress_kernel(xk_ref, xv_ref, pos_ref, wk1_ref, bk1_ref, wk2_ref, wv1_ref, bv1_ref, wv2_ref,
                     out_ref, outt_ref):
    half = CMP_STRIDE * DH
    nch = xk_ref.shape[2]

    def branch(x_ref, w1_ref, b1_ref, w2_ref):
        x = x_ref[0, 0]
        top = _dot(x, w1_ref[0:half, :])
        bot = _dot(x, w1_ref[half:2 * half, :])
        cpos = _dot(pos_ref[...], w1_ref[...])[0:1, :]
        pre = top + pltpu.roll(bot, nch - 1, 0) + cpos + b1_ref[...]
        return _dot(_gelu_tanh(pre).astype(BF16), w2_ref[...])

    out = branch(xk_ref, wk1_ref, bk1_ref, wk2_ref) + branch(xv_ref, wv1_ref, bv1_ref, wv2_ref)
    out_ref[0, 0] = out.astype(BF16)
    outt_ref[0, 0] = out.T.astype(BF16)


def _compress(xk, xv, pos8, wk1, bk1, wk2p, wv1, bv1, wv2p):
    B, G, nch, width = xk.shape
    blk = pl.BlockSpec((1, 1, nch, width), lambda b, g: (b, g, 0, 0))
    full2 = lambda a: pl.BlockSpec(a.shape, lambda b, g: (0, 0))
    return pl.pallas_call(
        _compress_kernel,
        grid=(B, G),
        in_specs=[blk, blk, full2(pos8), full2(wk1), full2(bk1), full2(wk2p), full2(wv1), full2(bv1),
                  full2(wv2p)],
        out_specs=[pl.BlockSpec((1, 1, nch, LANES), lambda b, g: (b, g, 0, 0)),
                   pl.BlockSpec((1, 1, LANES, nch), lambda b, g: (b, g, 0, 0))],
        out_shape=[jax.ShapeDtypeStruct((B, G, nch, LANES), BF16),
                   jax.ShapeDtypeStruct((B, G, LANES, nch), BF16)],
        compiler_params=pltpu.CompilerParams(dimension_semantics=("parallel", "parallel"),
                                             vmem_limit_bytes=VMEM_LIMIT),
        name="kv_compress",
    )(xk, xv, pos8, wk1, bk1, wk2p, wv1, bv1, wv2p)


def _heads_on_lanes(t4):
    return jnp.concatenate([t4[r] for r in range(NSA_R)], axis=1)


def _nsa_kernel(q_ref, ksa_ref, vsat_ref, kvw_ref, kvwt_ref, kvc_ref, kvct_ref, tw_ref, bc_ref, misc_ref,
                o_ref, qa_ref, m_ref, acc_ref, s_ref, ps_ref):
    g = pl.program_id(1)
    i = pl.program_id(2)
    t0 = i * QB
    jd1 = i // 2
    n_far = jnp.maximum((i + 1) // 2 - 1, 0)

    def score_tile(j, ntiles, bias_off):
        n = ntiles * SEL_TILE
        k0 = pl.multiple_of(j * SEL_TILE, SEL_TILE)
        sc = _dot(ksa_ref[0, 0, pl.ds(k0, n), :], qa_ref[j // (SUPER // SEL_TILE)])
        if bias_off is not None:
            off = pl.multiple_of(bias_off, LANES)
            sc = sc + _heads_on_lanes(tw_ref[:, pl.ds(off, n), :])
        s_ref[pl.ds(k0, n), :] = sc.astype(s_ref.dtype)
        return jnp.max(sc, axis=0, keepdims=True)

    def pv_tile(j, ntiles):
        n = ntiles * SEL_TILE
        k0 = pl.multiple_of(j * SEL_TILE, SEL_TILE)
        pt = jnp.exp(s_ref[pl.ds(k0, n), :] - m_ref[...].astype(s_ref.dtype))
        return _dot(vsat_ref[0, 0, :, pl.ds(k0, n)], pt)

    def sweep(total, block):
        n_grp = total // SWEEP_GROUP

        def body(jj, carry):
            block(SWEEP_GROUP * jj, SWEEP_GROUP)
            return carry
        lax.fori_loop(0, n_grp, body, 0)
        size = SWEEP_GROUP // 2
        while size >= 1:
            first = (total // (2 * size)) * (2 * size)

            @pl.when((total // size) % 2 == 1)
            def _(first=first, size=size):
                block(first, size)
            size //= 2

    def score_block(j, ntiles):
        mm = m_ref[...]
        for h in range(0, ntiles, 2):
            mm = jnp.maximum(mm, score_tile(j + h, min(2, ntiles - h), None))
        m_ref[...] = mm

    def pv_block(j, ntiles):
        part = None
        for h in range(0, ntiles, 2):
            pv = pv_tile(j + h, min(2, ntiles - h))
            part = pv if part is None else part + pv
        acc_ref[...] += part

    def front():
        qt = q_ref[0].astype(F32).T
        q_t = jnp.concatenate([qt[r * DH:(r + 1) * DH] for r in range(NSA_R)], axis=1)
        qpad_t = jnp.concatenate([q_t, jnp.zeros((DH, ROWS), F32)], axis=0).astype(BF16)

        bc = _heads_on_lanes(bc_ref[0])
        s = _dot(kvc_ref[0, 0], qpad_t) + bc
        valid = bc > 0.5 * NEG
        m = jnp.max(s, axis=0, keepdims=True)
        p = jnp.where(valid, jnp.exp(s - m), 0.0)
        l = jnp.sum(p, axis=0, keepdims=True)
        pn = p * jnp.where(l > 0.0, 1.0 / l, 0.0)
        o_c = _dot(kvct_ref[0, 0], pn.astype(BF16))[DH:2 * DH]

        wlen = WINDOW + QB
        start = pl.multiple_of(jnp.maximum(t0 - WINDOW, 0), LANES)
        offw = pl.multiple_of(start - t0 + WINDOW, LANES)
        sw = _dot(kvw_ref[0, 0, pl.ds(start, wlen), :], qpad_t) + _heads_on_lanes(tw_ref[:, pl.ds(offw, wlen), :])
        mw = jnp.max(sw, axis=0, keepdims=True)
        pw = jnp.exp(sw - mw)
        lw = jnp.sum(pw, axis=0, keepdims=True)
        o_w = _dot(kvwt_ref[0, 0, :, pl.ds(start, wlen)], pw.astype(BF16))[DH:2 * DH] / lw

        sg = _sigmoid(misc_ref[0]).T

        def gate_row(c):
            rows = []
            for r in range(NSA_R):
                c0 = MISC_GATES + 3 * r + c
                c1 = c0 + 3 * NSA_R
                rows.append(jnp.where(g == 0, sg[c0:c0 + 1], sg[c1:c1 + 1]))
            return jnp.concatenate(rows, axis=1)
        partial = gate_row(0) * o_c + gate_row(2) * o_w
        gate_sel = gate_row(1)

        nc = pn.shape[0]
        nsel = nc * CMP_STRIDE // SEL_BLOCK
        per_blk = SEL_BLOCK // CMP_STRIDE
        ps_ref[0:8] = jnp.zeros((8, QB), F32)
        ps_ref[8:8 + nc] = pn[:, 0:QB] + pn[:, QB:2 * QB] + pn[:, 2 * QB:3 * QB] + pn[:, 3 * QB:4 * QB]
        tok = lambda k: ps_ref[pl.ds(8 + k, nsel, stride=per_blk), :]
        imp = 0.5 * (tok(-1) + tok(3)) + tok(0) + tok(1) + tok(2)
        nidx = lax.broadcasted_iota(jnp.int32, (nsel, QB), 0)
        qidx = lax.broadcasted_iota(jnp.int32, (nsel, QB), 1)
        cur = (t0 + qidx) // SEL_BLOCK
        causal = nidx <= cur
        forced = causal & ((nidx == 0) | (nidx >= cur - (N_LOCAL - 1)))
        w = jnp.where(forced, POS, jnp.where(causal, imp, NEG))
        nf = nidx.astype(F32)
        sel = jnp.zeros((nsel, QB), F32)
        for _ in range(min(TOP_N, nsel)):
            mx = jnp.max(w, axis=0, keepdims=True)
            first = jnp.min(jnp.where(w == mx, nf, float(nsel)), axis=0, keepdims=True)
            pick = nf == first
            sel = jnp.where(pick, 1.0, sel)
            w = jnp.where(pick, -3e38, w)
        sel_bias = jnp.where((sel > 0.5) & causal, 0.0, NEG)

        for u in range(qa_ref.shape[0]):
            sb = sel_bias[u * 64:(u + 1) * 64]
            if sb.shape[0] < 64:
                sb = jnp.concatenate([sb, jnp.full((64 - sb.shape[0], QB), NEG, F32)], axis=0)
            sb4 = jnp.concatenate([sb] * NSA_R, axis=1)
            qa_ref[u] = jnp.concatenate([q_t, sb4], axis=0).astype(BF16)
        return partial, gate_sel

    partial, gate_sel = front()

    m_ref[...] = jnp.full(m_ref.shape, NEG, F32)
    sweep(n_far, score_block)
    m_ref[...] = jnp.maximum(m_ref[...], score_tile(jd1, 1, WINDOW - (t0 - jd1 * SEL_TILE)))

    @pl.when(n_far < jd1)
    def _():
        m_ref[...] = jnp.maximum(m_ref[...], score_tile(n_far, 1, WINDOW - (t0 - n_far * SEL_TILE)))

    acc_ref[...] = pv_tile(jd1, 1)
    sweep(jd1, pv_block)
    acc = acc_ref[...]
    o_s = acc[0:DH] / acc[DH:DH + 1]
    out_t = partial + gate_sel * o_s
    out_t = jnp.concatenate([out_t[:, r * QB:(r + 1) * QB] for r in range(NSA_R)], axis=0)
    o_ref[0] = out_t.T.astype(o_ref.dtype)


def _nsa(qn, ksa, vsat, kvw, kvwt, kvc, kvct, bias_w, bias_c, misc):
    B, S, _ = qn.shape
    nb = S // QB
    nc = kvc.shape[2]
    n_super = -(-S // SUPER)
    seq = lambda a: pl.BlockSpec((1, 1) + a.shape[2:], lambda b, g, i: (b, g, 0, 0))
    return pl.pallas_call(
        _nsa_kernel,
        grid=(B, NSA_G, nb),
        in_specs=[
            pl.BlockSpec((1, QB, NSA_R * DH), lambda b, g, i: (b, i, g)),
            seq(ksa), seq(vsat), seq(kvw), seq(kvwt), seq(kvc), seq(kvct),
            pl.BlockSpec((NSA_R, TW_WIDTH, QB), lambda b, g, i: (g, 0, 0)),
            pl.BlockSpec((1, NSA_R, nc, QB), lambda b, g, i: (i, g, 0, 0)),
            pl.BlockSpec((1, QB, LANES), lambda b, g, i: (b, i, 0)),
        ],
        out_specs=pl.BlockSpec((1, QB, NSA_R * DH), lambda b, g, i: (b, i, g)),
        out_shape=jax.ShapeDtypeStruct((B, S, NSA_HEADS * DH), BF16),
        scratch_shapes=[pltpu.VMEM((n_super, LANES, ROWS), BF16),
                        pltpu.VMEM((1, ROWS), F32),
                        pltpu.VMEM((VROWS, ROWS), F32),
                        pltpu.VMEM((S, ROWS), BF16),
                        pltpu.VMEM((nc + 8, QB), F32)],
        compiler_params=pltpu.CompilerParams(dimension_semantics=("parallel", "parallel", "arbitrary"),
                                             vmem_limit_bytes=VMEM_LIMIT),
        name="nsa_attention",
    )(qn, ksa, vsat, kvw, kvwt, kvc, kvct, bias_w, bias_c, misc)


def _gla_kernel(qk_ref, v_ref, z_ref, misc_ref, wa_ref, ba_ref, nw_ref, tri_ref, o_ref, st_ref):
    t = pl.program_id(2)
    T = qk_ref.shape[1]
    C = GLA_CHUNK

    @pl.when(t == 0)
    def _():
        st_ref[...] = jnp.zeros(st_ref.shape, F32)

    lane = lax.broadcasted_iota(jnp.int32, (1, LANES), 1)
    sign = jnp.where(lane < GLA_DK, 1.0 / GLA_TAU, -1.0 / GLA_TAU)
    zz = _dot(misc_ref[0].astype(BF16), wa_ref[0]) + ba_ref[0]
    log_sig = jnp.minimum(zz, 0.0) - jnp.log(1.0 + jnp.exp(-jnp.abs(zz)))
    la2 = log_sig * sign
    rr = lax.broadcasted_iota(jnp.int32, (C, C), 0)
    cc = lax.broadcasted_iota(jnp.int32, (C, C), 1)
    tril = rr >= cc
    scale = GLA_DK ** -0.5

    chunks = [slice(n * C, (n + 1) * C) for n in range(T // C)]
    hi = la2.astype(BF16)
    rem = la2 - hi.astype(F32)
    mid = rem.astype(BF16)
    lo = (rem - mid.astype(F32)).astype(BF16)
    tri = tri_ref[...]
    b2 = jnp.concatenate([_dot(tri, hi[sl]) + _dot(tri, mid[sl]) + _dot(tri, lo[sl]) for sl in chunks], axis=0)
    e = jnp.exp(b2)
    qke = qk_ref[0] * e
    q_e = (qke[:, 0:GLA_DK] * scale).astype(BF16)
    k_e = qke[:, GLA_DK:2 * GLA_DK].astype(BF16)
    v = v_ref[0].astype(BF16)
    attn = [jnp.where(tril, _dot_nt(q_e[sl], k_e[sl]), 0.0).astype(BF16) for sl in chunks]
    o_intra = [_dot(a, v[sl]) for a, sl in zip(attn, chunks)]
    kv = [_dot_tn(v[sl], k_e[sl]) for sl in chunks]
    st = st_ref[...]
    states = []
    for n, sl in enumerate(chunks):
        states.append(st.astype(BF16))
        st = (st + kv[n]) * e[sl.stop - 1:sl.stop, 0:GLA_DK]
    st_ref[...] = st
    o = jnp.concatenate([oi + _dot_nt(q_e[sl], s_in) for oi, sl, s_in in zip(o_intra, chunks, states)], axis=0)
    y = o * lax.rsqrt(jnp.mean(o * o, axis=-1, keepdims=True) + EPS) * nw_ref[...]
    o_ref[0] = (y * _silu(z_ref[0])).astype(o_ref.dtype)


def _gla(qkg, vg, zg, misc, wa_big, ba2, norm_w, tri, T):
    B, S, _ = qkg.shape
    hb = pl.BlockSpec((1, T, LANES), lambda b, h, t: (b, t, h))
    return pl.pallas_call(
        _gla_kernel,
        grid=(B, GLA_HEADS, S // T),
        in_specs=[hb, hb, hb,
                  pl.BlockSpec((1, T, LANES), lambda b, h, t: (b, t, 0)),
                  pl.BlockSpec((1, LANES, LANES), lambda b, h, t: (h, 0, 0)),
                  pl.BlockSpec((1, 1, LANES), lambda b, h, t: (h, 0, 0)),
                  pl.BlockSpec(norm_w.shape, lambda b, h, t: (0, 0)),
                  pl.BlockSpec(tri.shape, lambda b, h, t: (0, 0))],
        out_specs=hb,
        out_shape=jax.ShapeDtypeStruct((B, S, GLA_HEADS * GLA_DV), BF16),
        scratch_shapes=[pltpu.VMEM((GLA_DV, GLA_DK), F32)],
        compiler_params=pltpu.CompilerParams(dimension_semantics=("parallel", "parallel", "arbitrary"),
                                             vmem_limit_bytes=VMEM_LIMIT),
        name="gla_chunked",
    )(qkg, vg, zg, misc, wa_big, ba2, norm_w, tri)


def _out_kernel(x_ref, g0_ref, b0_ref, on_ref, zn_ref, og_ref, p_ref, wo_ref, wpg_ref, bpg_ref, wpe_ref,
                lg_ref, lb_ref, o_ref):
    half = on_ref.shape[2]
    h = _layer_norm(x_ref[0], g0_ref[...], b0_ref[...])
    mix_n = (on_ref[0].astype(F32) * _silu(zn_ref[0])).astype(BF16)
    y = _dot(mix_n, wo_ref[0:half, :]) + _dot(og_ref[0], wo_ref[half:2 * half, :])
    r = ALPHA * h + y
    gate = _sigmoid(_dot(r.astype(BF16), wpg_ref[...]) + bpg_ref[...])
    r = r + gate * _dot(p_ref[0].astype(BF16), wpe_ref[...])
    o_ref[0] = _layer_norm(r, lg_ref[...], lb_ref[...])


def _out_projection(x, ln0_g, ln0_b, o_nsa, zn, o_gla, p, w_out, w_pg, b_pg, w_pe, ln_g, ln_b, tm):
    B, S, D = x.shape
    tok = lambda w: pl.BlockSpec((1, tm, w), lambda b, t: (b, t, 0))
    full2 = lambda a: pl.BlockSpec(a.shape, lambda b, t: (0, 0))
    return pl.pallas_call(
        _out_kernel,
        grid=(B, S // tm),
        in_specs=[tok(D), full2(ln0_g), full2(ln0_b), tok(512), tok(512), tok(512), tok(PLE_DIM),
                  full2(w_out), full2(w_pg), full2(b_pg), full2(w_pe), full2(ln_g), full2(ln_b)],
        out_specs=tok(D),
        out_shape=jax.ShapeDtypeStruct((B, S, D), F32),
        compiler_params=pltpu.CompilerParams(dimension_semantics=("parallel", "parallel"),
                                             vmem_limit_bytes=VMEM_LIMIT),
        name="out_proj_deepnorm",
    )(x, ln0_g, ln0_b, o_nsa, zn, o_gla, p, w_out, w_pg, b_pg, w_pe, ln_g, ln_b)


def _permute_w_in(w):
    widths = (512, 128, 128, 128, 128, 128, 128, 24, 512, 256, 256, 512, 16, 512)
    offs = np.concatenate([[0], np.cumsum(widths)])
    (q_n, kc, vc, ks, vs, kw, vw, gates, z_n, q_g, k_g, v_g, a_low, z_g) = [
        w[:, int(offs[k]):int(offs[k + 1])] for k in range(len(widths))]
    kvw = [jnp.concatenate([kw[:, g * DH:(g + 1) * DH], vw[:, g * DH:(g + 1) * DH]], axis=1)
           for g in range(NSA_G)]
    qk = jnp.concatenate([jnp.concatenate([q_g[:, h * GLA_DK:(h + 1) * GLA_DK],
                                           k_g[:, h * GLA_DK:(h + 1) * GLA_DK]], axis=1)
                          for h in range(GLA_HEADS)], axis=1)
    misc = jnp.concatenate([gates, a_low,
                            jnp.zeros((w.shape[0], LANES - gates.shape[1] - a_low.shape[1]), w.dtype)], axis=1)
    cols = [q_n * (DH ** -0.5), kc, vc, ks, vs, kvw[0], kvw[1], z_n, qk, v_g, z_g, misc]
    return jnp.concatenate(cols, axis=1).astype(BF16)


def kernel(x, p, ln0_g, ln0_b, rel_bias, w_in, w_a2, b_a, gla_norm_w, pos_cmp, w_ck1, b_ck1, w_ck2,
           w_cv1, b_cv1, w_cv2, w_out, w_pe, w_pg, b_pg, ln_g, ln_b):
    B, S, D = x.shape
    assert D == D_MODEL and S % 512 == 0 and S >= WINDOW + QB and w_in.shape[0] == DEPTH == 1
    nb = S // QB
    nch = S // CMP_STRIDE
    row = lambda a: a.reshape(1, -1)

    bias_c, bias_w = _bias_tables(rel_bias, nb, nch)

    w_perm = _permute_w_in(w_in[0])
    (qn, kcr, vcr, ksa, vsat, kvw, kvwt, zn, qkg, vg, zg, misc) = _in_projection(
        x, row(ln0_g), row(ln0_b), w_perm, 256)

    def chunks(a):
        return a.reshape(B, nch, CMP_STRIDE, NSA_G, DH).transpose(0, 3, 1, 2, 4).reshape(
            B, NSA_G, nch, CMP_STRIDE * DH)
    pos8 = jnp.broadcast_to(pos_cmp[0].reshape(1, CMP_LEN * DH), (8, CMP_LEN * DH)).astype(BF16)
    zpad = jnp.zeros((CMP_HIDDEN, DH), F32)
    wk2p = jnp.concatenate([w_ck2[0], zpad], axis=1).astype(BF16)
    wv2p = jnp.concatenate([zpad, w_cv2[0]], axis=1).astype(BF16)
    kvc, kvct = _compress(chunks(kcr), chunks(vcr), pos8, w_ck1[0].astype(BF16), row(b_ck1[0]), wk2p,
                          w_cv1[0].astype(BF16), row(b_cv1[0]), wv2p)

    o_nsa = _nsa(qn, ksa, vsat, kvw, kvwt, kvc, kvct, bias_w, bias_c, misc)

    wa = w_a2[0]
    wa_big = jnp.zeros((GLA_HEADS, LANES, LANES), F32)
    for h in range(GLA_HEADS):
        wh = wa[:, h * GLA_DK:(h + 1) * GLA_DK]
        wa_big = wa_big.at[h, MISC_ALOW:MISC_ALOW + GLA_RANK, :].set(jnp.concatenate([wh, wh], axis=1))
    ba = b_a[0].reshape(GLA_HEADS, 1, GLA_DK)
    ba2 = jnp.concatenate([ba, ba], axis=2)
    tri = jnp.asarray(np.tril(np.ones((GLA_CHUNK, GLA_CHUNK), np.float32))).astype(BF16)
    o_gla = _gla(qkg, vg, zg, misc, wa_big.astype(BF16), ba2, row(gla_norm_w[0]), tri, 512)

    return _out_projection(x, row(ln0_g), row(ln0_b), o_nsa, zn, o_gla, p[0], w_out[0].astype(BF16),
                           w_pg[0].astype(BF16), row(b_pg[0]), w_pe[0].astype(BF16), row(ln_g[0]),
                           row(ln_b[0]), 256)
```

```python
import math

import numpy as np
import jax
import jax.numpy as jnp
from jax import lax
from jax.experimental import pallas as pl
from jax.experimental.pallas import tpu as pltpu

F32 = jnp.float32
BF16 = jnp.bfloat16

D_MODEL = 1024
PLE_DIM = 256
NSA_HEADS = 8
NSA_G = 2
NSA_R = NSA_HEADS // NSA_G
DH = 64
CMP_LEN = 32
CMP_STRIDE = 16
CMP_HIDDEN = 256
SEL_BLOCK = 64
TOP_N = 16
N_LOCAL = 2
WINDOW = 512
QB = 256
GLA_HEADS = 4
GLA_DK = 64
GLA_DV = 128
GLA_RANK = 16
GLA_TAU = 16.0
GLA_CHUNK = 64
NUM_BUCKETS = 32
MAX_DISTANCE = 128
DEPTH = 1
ALPHA = (2.0 * DEPTH) ** 0.25
EPS = 1e-5
NEG = -1e30
POS = 1e30

LANES = 128
VMEM_LIMIT = 56 * 1024 * 1024

TOKEN_TILE = 512
SEL_TILE = 256
SWEEP_GROUP = 8
VROWS = 80
SUPER = 64 * SEL_BLOCK
TW_WIDTH = WINDOW + QB + WINDOW
ROWS = NSA_R * QB

C_QN, C_KC, C_VC, C_KS, C_VS, C_KVW0, C_KVW1, C_ZN, C_QKG, C_VG, C_ZG, C_MISC, C_END = (
    0, 512, 640, 768, 896, 1024, 1152, 1280, 1792, 2304, 2816, 3328, 3456)
MISC_GATES = 0
MISC_ALOW = 24


def _bucket_thresholds():
    d = np.arange(0, 4 * MAX_DISTANCE)
    max_exact = NUM_BUCKETS // 2
    nf = np.maximum(d, 1).astype(np.float32)
    large = max_exact + (np.log(nf / np.float32(max_exact)) / np.float32(math.log(MAX_DISTANCE / max_exact))
                         * np.float32(NUM_BUCKETS - max_exact)).astype(np.int32)
    large = np.minimum(large, NUM_BUCKETS - 1)
    bucket = np.where(d < max_exact, d, large)
    assert np.all(np.diff(bucket) >= 0) and bucket[-1] == NUM_BUCKETS - 1
    return [int(np.argmax(bucket >= k)) for k in range(NUM_BUCKETS)]


_THR = _bucket_thresholds()
CMP_BAND = 32
assert (QB - 1 + _THR[NUM_BUCKETS - 1]) // CMP_STRIDE + 1 + 7 <= CMP_BAND


def _dot(a, b, **kw):
    return jnp.dot(a, b, preferred_element_type=F32, **kw)


def _dot_nt(a, b, **kw):
    return lax.dot_general(a, b, (((1,), (1,)), ((), ())), preferred_element_type=F32, **kw)


def _dot_tn(a, b, **kw):
    return lax.dot_general(a, b, (((0,), (0,)), ((), ())), preferred_element_type=F32, **kw)


def _layer_norm(x, g, b):
    mu = jnp.mean(x, axis=-1, keepdims=True)
    xc = x - mu
    var = jnp.mean(xc * xc, axis=-1, keepdims=True)
    return xc * lax.rsqrt(var + EPS) * g + b


def _sigmoid(x):
    return 1.0 / (1.0 + jnp.exp(-x))


def _silu(x):
    return x * _sigmoid(x)


def _bias_from_dist(rb_ref, dist, valid, put):
    masks = [dist >= _THR[k] for k in range(1, NUM_BUCKETS)]
    for h in range(NSA_HEADS):
        val = jnp.full(dist.shape, rb_ref[0, h], F32)
        for k in range(1, NUM_BUCKETS):
            val = jnp.where(masks[k - 1], rb_ref[k, h], val)
        val = val - rb_ref[NUM_BUCKETS - 1, h]
        put(h, jnp.where(valid, val, NEG))


def _bias_cmp_kernel(rb_ref, out_ref):
    i = pl.program_id(0)
    nc = out_ref.shape[2]
    big = CMP_STRIDE * QB
    j_zero_max = (QB * i - (CMP_LEN - 1) - _THR[NUM_BUCKETS - 1] + CMP_STRIDE * big) // CMP_STRIDE - big
    start = jnp.clip((j_zero_max + 1 + 8 * big) // 8 * 8 - 8 * big, 0, nc - CMP_BAND)
    start = pl.multiple_of(start, 8)
    j_all = lax.broadcasted_iota(jnp.int32, (nc, QB), 0)
    fill = jnp.where(j_all <= j_zero_max, 0.0, NEG)
    for h in range(NSA_HEADS):
        out_ref[0, h] = fill
    j = start + lax.broadcasted_iota(jnp.int32, (CMP_BAND, QB), 0)
    a = lax.broadcasted_iota(jnp.int32, (CMP_BAND, QB), 1)
    dist = a + QB * i - CMP_STRIDE * j - (CMP_LEN - 1)

    def put(h, v):
        out_ref[0, h, pl.ds(start, CMP_BAND), :] = v
    _bias_from_dist(rb_ref, dist, dist >= 0, put)


def _bias_win_kernel(rb_ref, out_ref):
    shape = out_ref.shape[1:]
    y = lax.broadcasted_iota(jnp.int32, shape, 0)
    a = lax.broadcasted_iota(jnp.int32, shape, 1)
    dist = a + WINDOW - y

    def put(h, v):
        out_ref[h] = v
    _bias_from_dist(rb_ref, dist, (dist >= 0) & (dist < WINDOW), put)


def _bias_tables(rel_bias, nb, nc):
    smem = pl.BlockSpec(memory_space=pltpu.SMEM)
    bias_c = pl.pallas_call(
        _bias_cmp_kernel,
        grid=(nb,),
        in_specs=[smem],
        out_specs=pl.BlockSpec((1, NSA_HEADS, nc, QB), lambda i: (i, 0, 0, 0)),
        out_shape=jax.ShapeDtypeStruct((nb, NSA_HEADS, nc, QB), F32),
        compiler_params=pltpu.CompilerParams(dimension_semantics=("parallel",)),
        name="bias_cmp",
    )(rel_bias)
    bias_w = pl.pallas_call(
        _bias_win_kernel,
        in_specs=[smem],
        out_shape=jax.ShapeDtypeStruct((NSA_HEADS, TW_WIDTH, QB), F32),
        name="bias_win",
    )(rel_bias)
    return bias_c, bias_w


def _inproj_kernel(x_ref, g_ref, b_ref, w_ref, qn_ref, kcr_ref, vcr_ref, ksa_ref, vsat_ref, kvw_ref, kvwt_ref,
                   zn_ref, qkg_ref, vg_ref, zg_ref, misc_ref):
    t = pl.program_id(1)
    tm = x_ref.shape[1]
    h = _layer_norm(x_ref[0], g_ref[...], b_ref[...]).astype(BF16)

    def mm(c0, c1):
        return _dot(h, w_ref[:, c0:c1])

    qn_ref[0] = mm(C_QN, C_KC).astype(BF16)
    kcr_ref[0] = mm(C_KC, C_VC).astype(BF16)
    vcr_ref[0] = mm(C_VC, C_KS).astype(BF16)
    ks = mm(C_KS, C_VS)
    vs = mm(C_VS, C_KVW0)
    lane = lax.broadcasted_iota(jnp.int32, (tm, LANES), 1)
    row = lax.broadcasted_iota(jnp.int32, (tm, LANES), 0)
    low = lane < DH
    blk = ((t * tm + row) // SEL_BLOCK) % (SUPER // SEL_BLOCK)
    onehot = jnp.where(lane - DH == blk, 1.0, 0.0)
    ones_col = jnp.where(lane == DH, 1.0, 0.0)
    ksa_ref[0, 0] = jnp.where(low, ks, onehot).astype(BF16)
    ksa_ref[0, 1] = jnp.where(low, pltpu.roll(ks, DH, 1), onehot).astype(BF16)
    vsat_ref[0, 0] = jnp.where(low, vs, ones_col).T[0:VROWS].astype(BF16)
    vsat_ref[0, 1] = jnp.where(low, pltpu.roll(vs, DH, 1), ones_col).T[0:VROWS].astype(BF16)
    for g, (c0, c1) in enumerate(((C_KVW0, C_KVW1), (C_KVW1, C_ZN))):
        kvw = mm(c0, c1)
        kvw_ref[0, g] = kvw.astype(BF16)
        kvwt_ref[0, g] = kvw.T.astype(BF16)
    zn_ref[0] = mm(C_ZN, C_QKG)
    qkg_ref[0] = mm(C_QKG, C_VG)
    vg_ref[0] = mm(C_VG, C_ZG)
    zg_ref[0] = mm(C_ZG, C_MISC)
    misc_ref[0] = mm(C_MISC, C_END)


def _in_projection(x, ln0_g, ln0_b, w_perm, tm):
    B, S, D = x.shape
    tok = lambda w: pl.BlockSpec((1, tm, w), lambda b, t: (b, t, 0))
    grp = pl.BlockSpec((1, NSA_G, tm, LANES), lambda b, t: (b, 0, t, 0))
    grp_t = pl.BlockSpec((1, NSA_G, LANES, tm), lambda b, t: (b, 0, 0, t))
    grp_v = pl.BlockSpec((1, NSA_G, VROWS, tm), lambda b, t: (b, 0, 0, t))
    full2 = lambda a: pl.BlockSpec(a.shape, lambda b, t: (0, 0))
    sds = jax.ShapeDtypeStruct
    return pl.pallas_call(
        _inproj_kernel,
        grid=(B, S // tm),
        in_specs=[tok(D), full2(ln0_g), full2(ln0_b), full2(w_perm)],
        out_specs=[tok(512), tok(LANES), tok(LANES), grp, grp_v, grp, grp_t, tok(512), tok(512), tok(512),
                   tok(512), tok(LANES)],
        out_shape=[sds((B, S, 512), BF16), sds((B, S, LANES), BF16), sds((B, S, LANES), BF16),
                   sds((B, NSA_G, S, LANES), BF16), sds((B, NSA_G, VROWS, S), BF16),
                   sds((B, NSA_G, S, LANES), BF16), sds((B, NSA_G, LANES, S), BF16),
                   sds((B, S, 512), F32), sds((B, S, 512), F32), sds((B, S, 512), F32), sds((B, S, 512), F32),
                   sds((B, S, LANES), F32)],
        compiler_params=pltpu.CompilerParams(dimension_semantics=("parallel", "parallel"),
                                             vmem_limit_bytes=VMEM_LIMIT),
        name="ln0_inproj",
    )(x, ln0_g, ln0_b, w_perm)


def _gelu_tanh(x):
    c = math.sqrt(2.0 / math.pi)
    return x * (0.5 * (1.0 + jnp.tanh(c * (x + 0.044715 * (x * x * x)))))


def _compress_kernel(xk_ref, xv_ref, pos_ref, wk1_ref, bk1_ref, wk2_ref, wv1_ref, bv1_ref, wv2_ref,
                     out_ref, outt_ref):
    half = CMP_STRIDE * DH
    nch = xk_ref.shape[2]

    def branch(x_ref, w1_ref, b1_ref, w2_ref):
        x = x_ref[0, 0]
        top = _dot(x, w1_ref[0:half, :])
        bot = _dot(x, w1_ref[half:2 * half, :])
        cpos = _dot(pos_ref[...], w1_ref[...])[0:1, :]
        pre = top + pltpu.roll(bot, nch - 1, 0) + cpos + b1_ref[...]
        return _dot(_gelu_tanh(pre).astype(BF16), w2_ref[...])

    out = branch(xk_ref, wk1_ref, bk1_ref, wk2_ref) + branch(xv_ref, wv1_ref, bv1_ref, wv2_ref)
    out_ref[0, 0] = out.astype(BF16)
    outt_ref[0, 0] = out.T.astype(BF16)


def _compress(xk, xv, pos8, wk1, bk1, wk2p, wv1, bv1, wv2p):
    B, G, nch, width = xk.shape
    blk = pl.BlockSpec((1, 1, nch, width), lambda b, g: (b, g, 0, 0))
    full2 = lambda a: pl.BlockSpec(a.shape, lambda b, g: (0, 0))
    return pl.pallas_call(
        _compress_kernel,
        grid=(B, G),
        in_specs=[blk, blk, full2(pos8), full2(wk1), full2(bk1), full2(wk2p), full2(wv1), full2(bv1),
                  full2(wv2p)],
        out_specs=[pl.BlockSpec((1, 1, nch, LANES), lambda b, g: (b, g, 0, 0)),
                   pl.BlockSpec((1, 1, LANES, nch), lambda b, g: (b, g, 0, 0))],
        out_shape=[jax.ShapeDtypeStruct((B, G, nch, LANES), BF16),
                   jax.ShapeDtypeStruct((B, G, LANES, nch), BF16)],
        compiler_params=pltpu.CompilerParams(dimension_semantics=("parallel", "parallel"),
                                             vmem_limit_bytes=VMEM_LIMIT),
        name="kv_compress",
    )(xk, xv, pos8, wk1, bk1, wk2p, wv1, bv1, wv2p)


def _heads_on_lanes(t4):
    return jnp.concatenate([t4[r] for r in range(NSA_R)], axis=1)


def _nsa_kernel(q_ref, ksa_ref, vsat_ref, kvw_ref, kvwt_ref, kvc_ref, kvct_ref, tw_ref, bc_ref, misc_ref,
                o_ref, qa_ref, m_ref, acc_ref, s_ref, ps_ref):
    g = pl.program_id(1)
    i = pl.program_id(2)
    t0 = i * QB
    jd1 = (t0 + QB - 1) // SEL_TILE
    n_far = jnp.maximum(t0 - (_THR[NUM_BUCKETS - 1] - 1), 0) // SEL_TILE

    def score_tile(j, ntiles, bias_off):
        n = ntiles * SEL_TILE
        k0 = pl.multiple_of(j * SEL_TILE, SEL_TILE)
        sc = _dot(ksa_ref[0, 0, pl.ds(k0, n), :], qa_ref[j // (SUPER // SEL_TILE)])
        if bias_off is not None:
            off = pl.multiple_of(bias_off, LANES)
            sc = sc + _heads_on_lanes(tw_ref[:, pl.ds(off, n), :])
        s_ref[pl.ds(k0, n), :] = sc.astype(s_ref.dtype)
        return jnp.max(sc, axis=0, keepdims=True)

    def pv_tile(j, ntiles):
        n = ntiles * SEL_TILE
        k0 = pl.multiple_of(j * SEL_TILE, SEL_TILE)
        pt = jnp.exp(s_ref[pl.ds(k0, n), :] - m_ref[...].astype(s_ref.dtype))
        return _dot(vsat_ref[0, 0, :, pl.ds(k0, n)], pt)

    def sweep(total, block):
        n_grp = total // SWEEP_GROUP

        def body(jj, carry):
            block(SWEEP_GROUP * jj, SWEEP_GROUP)
            return carry
        lax.fori_loop(0, n_grp, body, 0)
        size = SWEEP_GROUP // 2
        while size >= 1:
            first = (total // (2 * size)) * (2 * size)

            @pl.when((total // size) % 2 == 1)
            def _(first=first, size=size):
                block(first, size)
            size //= 2

    def score_block(j, ntiles):
        mm = m_ref[...]
        for h in range(0, ntiles, 2):
            mm = jnp.maximum(mm, score_tile(j + h, min(2, ntiles - h), None))
        m_ref[...] = mm

    def pv_block(j, ntiles):
        part = None
        for h in range(0, ntiles, 2):
            pv = pv_tile(j + h, min(2, ntiles - h))
            part = pv if part is None else part + pv
        acc_ref[...] += part

    def front():
        qt = q_ref[0].astype(F32).T
        q_t = jnp.concatenate([qt[r * DH:(r + 1) * DH] for r in range(NSA_R)], axis=1)
        qpad_t = jnp.concatenate([q_t, jnp.zeros((DH, ROWS), F32)], axis=0).astype(BF16)

        bc = _heads_on_lanes(bc_ref[0])
        s = _dot(kvc_ref[0, 0], qpad_t) + bc
        valid = bc > 0.5 * NEG
        m = jnp.max(s, axis=0, keepdims=True)
        p = jnp.where(valid, jnp.exp(s - m), 0.0)
        l = jnp.sum(p, axis=0, keepdims=True)
        pn = p * jnp.where(l > 0.0, 1.0 / l, 0.0)
        o_c = _dot(kvct_ref[0, 0], pn.astype(BF16))[DH:2 * DH]

        wlen = WINDOW + QB
        start = pl.multiple_of(jnp.maximum(t0 - WINDOW, 0), LANES)
        offw = pl.multiple_of(start - t0 + WINDOW, LANES)
        sw = _dot(kvw_ref[0, 0, pl.ds(start, wlen), :], qpad_t) + _heads_on_lanes(tw_ref[:, pl.ds(offw, wlen), :])
        mw = jnp.max(sw, axis=0, keepdims=True)
        pw = jnp.exp(sw - mw)
        lw = jnp.sum(pw, axis=0, keepdims=True)
        o_w = _dot(kvwt_ref[0, 0, :, pl.ds(start, wlen)], pw.astype(BF16))[DH:2 * DH] / lw

        sg = _sigmoid(misc_ref[0]).T

        def gate_row(c):
            rows = []
            for r in range(NSA_R):
                c0 = MISC_GATES + 3 * r + c
                c1 = c0 + 3 * NSA_R
                rows.append(jnp.where(g == 0, sg[c0:c0 + 1], sg[c1:c1 + 1]))
            return jnp.concatenate(rows, axis=1)
        partial = gate_row(0) * o_c + gate_row(2) * o_w
        gate_sel = gate_row(1)

        nc = pn.shape[0]
        nsel = nc * CMP_STRIDE // SEL_BLOCK
        per_blk = SEL_BLOCK // CMP_STRIDE
        psum = pn[:, 0:QB] + pn[:, QB:2 * QB] + pn[:, 2 * QB:3 * QB] + pn[:, 3 * QB:4 * QB]
        for c in range(QB // LANES):
            ps_ref[c, 0:8] = jnp.zeros((8, LANES), F32)
            ps_ref[c, 8:8 + nc] = psum[:, c * LANES:(c + 1) * LANES]
        tok = lambda k: jnp.concatenate(
            [ps_ref.at[c][pl.ds(8 + k, nsel, stride=per_blk), :] for c in range(QB // LANES)], axis=1)
        imp = 0.5 * (tok(-1) + tok(3)) + tok(0) + tok(1) + tok(2)
        nidx = lax.broadcasted_iota(jnp.int32, (nsel, QB), 0)
        qidx = lax.broadcasted_iota(jnp.int32, (nsel, QB), 1)
        cur = (t0 + qidx) // SEL_BLOCK
        causal = nidx <= cur
        forced = causal & ((nidx == 0) | (nidx >= cur - (N_LOCAL - 1)))
        w = jnp.where(forced, POS, jnp.where(causal, imp, NEG))
        nf = nidx.astype(F32)
        sel = jnp.zeros((nsel, QB), F32)
        for _ in range(min(TOP_N, nsel)):
            mx = jnp.max(w, axis=0, keepdims=True)
            first = jnp.min(jnp.where(w == mx, nf, float(nsel)), axis=0, keepdims=True)
            pick = nf == first
            sel = jnp.where(pick, 1.0, sel)
            w = jnp.where(pick, -3e38, w)
        sel_bias = jnp.where((sel > 0.5) & causal, 0.0, NEG)

        for u in range(qa_ref.shape[0]):
            sb = sel_bias[u * 64:(u + 1) * 64]
            if sb.shape[0] < 64:
                sb = jnp.concatenate([sb, jnp.full((64 - sb.shape[0], QB), NEG, F32)], axis=0)
            sb4 = jnp.concatenate([sb] * NSA_R, axis=1)
            qa_ref[u] = jnp.concatenate([q_t, sb4], axis=0).astype(BF16)
        return partial, gate_sel

    partial, gate_sel = front()

    m_ref[...] = jnp.full(m_ref.shape, NEG, F32)
    sweep(n_far, score_block)
    m_ref[...] = jnp.maximum(m_ref[...], score_tile(jd1, 1, WINDOW - (t0 - jd1 * SEL_TILE)))

    @pl.when(n_far < jd1)
    def _():
        m_ref[...] = jnp.maximum(m_ref[...], score_tile(n_far, 1, WINDOW - (t0 - n_far * SEL_TILE)))

    acc_ref[...] = pv_tile(jd1, 1)
    sweep(jd1, pv_block)
    acc = acc_ref[...]
    o_s = acc[0:DH] / acc[DH:DH + 1]
    out_t = partial + gate_sel * o_s
    out_t = jnp.concatenate([out_t[:, r * QB:(r + 1) * QB] for r in range(NSA_R)], axis=0)
    o_ref[0] = out_t.T.astype(o_ref.dtype)


def _nsa(qn, ksa, vsat, kvw, kvwt, kvc, kvct, bias_w, bias_c, misc):
    B, S, _ = qn.shape
    nb = S // QB
    nc = kvc.shape[2]
    n_super = -(-S // SUPER)
    seq = lambda a: pl.BlockSpec((1, 1) + a.shape[2:], lambda b, g, i: (b, g, 0, 0))
    return pl.pallas_call(
        _nsa_kernel,
        grid=(B, NSA_G, nb),
        in_specs=[
            pl.BlockSpec((1, QB, NSA_R * DH), lambda b, g, i: (b, i, g)),
            seq(ksa), seq(vsat), seq(kvw), seq(kvwt), seq(kvc), seq(kvct),
            pl.BlockSpec((NSA_R, TW_WIDTH, QB), lambda b, g, i: (g, 0, 0)),
            pl.BlockSpec((1, NSA_R, nc, QB), lambda b, g, i: (i, g, 0, 0)),
            pl.BlockSpec((1, QB, LANES), lambda b, g, i: (b, i, 0)),
        ],
        out_specs=pl.BlockSpec((1, QB, NSA_R * DH), lambda b, g, i: (b, i, g)),
        out_shape=jax.ShapeDtypeStruct((B, S, NSA_HEADS * DH), BF16),
        scratch_shapes=[pltpu.VMEM((n_super, LANES, ROWS), BF16),
                        pltpu.VMEM((1, ROWS), F32),
                        pltpu.VMEM((VROWS, ROWS), F32),
                        pltpu.VMEM((S, ROWS), BF16),
                        pltpu.VMEM((QB // LANES, nc + 8, LANES), F32)],
        compiler_params=pltpu.CompilerParams(dimension_semantics=("parallel", "parallel", "arbitrary"),
                                             vmem_limit_bytes=VMEM_LIMIT),
        name="nsa_attention",
    )(qn, ksa, vsat, kvw, kvwt, kvc, kvct, bias_w, bias_c, misc)


def _gla_kernel(qk_ref, v_ref, z_ref, misc_ref, wa_ref, ba_ref, nw_ref, tri_ref, o_ref, st_ref):
    t = pl.program_id(2)
    T = qk_ref.shape[1]
    C = GLA_CHUNK

    @pl.when(t == 0)
    def _():
        st_ref[...] = jnp.zeros(st_ref.shape, F32)

    lane = lax.broadcasted_iota(jnp.int32, (1, LANES), 1)
    sign = jnp.where(lane < GLA_DK, 1.0 / GLA_TAU, -1.0 / GLA_TAU)
    zz = _dot(misc_ref[0].astype(BF16), wa_ref[0]) + ba_ref[0]
    log_sig = jnp.minimum(zz, 0.0) - jnp.log(1.0 + jnp.exp(-jnp.abs(zz)))
    la2 = log_sig * sign
    rr = lax.broadcasted_iota(jnp.int32, (C, C), 0)
    cc = lax.broadcasted_iota(jnp.int32, (C, C), 1)
    tril = rr >= cc
    scale = GLA_DK ** -0.5

    chunks = [slice(n * C, (n + 1) * C) for n in range(T // C)]
    hi = la2.astype(BF16)
    rem = la2 - hi.astype(F32)
    mid = rem.astype(BF16)
    lo = (rem - mid.astype(F32)).astype(BF16)
    tri = tri_ref[...]
    b2 = jnp.concatenate([_dot(tri, hi[sl]) + _dot(tri, mid[sl]) + _dot(tri, lo[sl]) for sl in chunks], axis=0)
    e = jnp.exp(b2)
    qke = qk_ref[0] * e
    q_e = (qke[:, 0:GLA_DK] * scale).astype(BF16)
    k_e = qke[:, GLA_DK:2 * GLA_DK].astype(BF16)
    v = v_ref[0].astype(BF16)
    attn = [jnp.where(tril, _dot_nt(q_e[sl], k_e[sl]), 0.0).astype(BF16) for sl in chunks]
    o_intra = [_dot(a, v[sl]) for a, sl in zip(attn, chunks)]
    kv = [_dot_tn(v[sl], k_e[sl]) for sl in chunks]
    st = st_ref[...]
    states = []
    for n, sl in enumerate(chunks):
        states.append(st.astype(BF16))
        st = (st + kv[n]) * e[sl.stop - 1:sl.stop, 0:GLA_DK]
    st_ref[...] = st
    o = jnp.concatenate([oi + _dot_nt(q_e[sl], s_in) for oi, sl, s_in in zip(o_intra, chunks, states)], axis=0)
    y = o * lax.rsqrt(jnp.mean(o * o, axis=-1, keepdims=True) + EPS) * nw_ref[...]
    o_ref[0] = (y * _silu(z_ref[0])).astype(o_ref.dtype)


def _gla(qkg, vg, zg, misc, wa_big, ba2, norm_w, tri, T):
    B, S, _ = qkg.shape
    hb = pl.BlockSpec((1, T, LANES), lambda b, h, t: (b, t, h))
    return pl.pallas_call(
        _gla_kernel,
        grid=(B, GLA_HEADS, S // T),
        in_specs=[hb, hb, hb,
                  pl.BlockSpec((1, T, LANES), lambda b, h, t: (b, t, 0)),
                  pl.BlockSpec((1, LANES, LANES), lambda b, h, t: (h, 0, 0)),
                  pl.BlockSpec((1, 1, LANES), lambda b, h, t: (h, 0, 0)),
                  pl.BlockSpec(norm_w.shape, lambda b, h, t: (0, 0)),
                  pl.BlockSpec(tri.shape, lambda b, h, t: (0, 0))],
        out_specs=hb,
        out_shape=jax.ShapeDtypeStruct((B, S, GLA_HEADS * GLA_DV), BF16),
        scratch_shapes=[pltpu.VMEM((GLA_DV, GLA_DK), F32)],
        compiler_params=pltpu.CompilerParams(dimension_semantics=("parallel", "parallel", "arbitrary"),
                                             vmem_limit_bytes=VMEM_LIMIT),
        name="gla_chunked",
    )(qkg, vg, zg, misc, wa_big, ba2, norm_w, tri)


def _out_kernel(x_ref, g0_ref, b0_ref, on_ref, zn_ref, og_ref, p_ref, wo_ref, wpg_ref, bpg_ref, wpe_ref,
                lg_ref, lb_ref, o_ref):
    half = on_ref.shape[2]
    h = _layer_norm(x_ref[0], g0_ref[...], b0_ref[...])
    mix_n = (on_ref[0].astype(F32) * _silu(zn_ref[0])).astype(BF16)
    y = _dot(mix_n, wo_ref[0:half, :]) + _dot(og_ref[0], wo_ref[half:2 * half, :])
    r = ALPHA * h + y
    gate = _sigmoid(_dot(r.astype(BF16), wpg_ref[...]) + bpg_ref[...])
    r = r + gate * _dot(p_ref[0].astype(BF16), wpe_ref[...])
    o_ref[0] = _layer_norm(r, lg_ref[...], lb_ref[...])


def _out_projection(x, ln0_g, ln0_b, o_nsa, zn, o_gla, p, w_out, w_pg, b_pg, w_pe, ln_g, ln_b, tm):
    B, S, D = x.shape
    tok = lambda w: pl.BlockSpec((1, tm, w), lambda b, t: (b, t, 0))
    full2 = lambda a: pl.BlockSpec(a.shape, lambda b, t: (0, 0))
    return pl.pallas_call(
        _out_kernel,
        grid=(B, S // tm),
        in_specs=[tok(D), full2(ln0_g), full2(ln0_b), tok(512), tok(512), tok(512), tok(PLE_DIM),
                  full2(w_out), full2(w_pg), full2(b_pg), full2(w_pe), full2(ln_g), full2(ln_b)],
        out_specs=tok(D),
        out_shape=jax.ShapeDtypeStruct((B, S, D), F32),
        compiler_params=pltpu.CompilerParams(dimension_semantics=("parallel", "parallel"),
                                             vmem_limit_bytes=VMEM_LIMIT),
        name="out_proj_deepnorm",
    )(x, ln0_g, ln0_b, o_nsa, zn, o_gla, p, w_out, w_pg, b_pg, w_pe, ln_g, ln_b)


def _permute_w_in(w):
    widths = (512, 128, 128, 128, 128, 128, 128, 24, 512, 256, 256, 512, 16, 512)
    offs = np.concatenate([[0], np.cumsum(widths)])
    (q_n, kc, vc, ks, vs, kw, vw, gates, z_n, q_g, k_g, v_g, a_low, z_g) = [
        w[:, int(offs[k]):int(offs[k + 1])] for k in range(len(widths))]
    kvw = [jnp.concatenate([kw[:, g * DH:(g + 1) * DH], vw[:, g * DH:(g + 1) * DH]], axis=1)
           for g in range(NSA_G)]
    qk = jnp.concatenate([jnp.concatenate([q_g[:, h * GLA_DK:(h + 1) * GLA_DK],
                                           k_g[:, h * GLA_DK:(h + 1) * GLA_DK]], axis=1)
                          for h in range(GLA_HEADS)], axis=1)
    misc = jnp.concatenate([gates, a_low,
                            jnp.zeros((w.shape[0], LANES - gates.shape[1] - a_low.shape[1]), w.dtype)], axis=1)
    cols = [q_n * (DH ** -0.5), kc, vc, ks, vs, kvw[0], kvw[1], z_n, qk, v_g, z_g, misc]
    return jnp.concatenate(cols, axis=1).astype(BF16)


def kernel(x, p, ln0_g, ln0_b, rel_bias, w_in, w_a2, b_a, gla_norm_w, pos_cmp, w_ck1, b_ck1, w_ck2,
           w_cv1, b_cv1, w_cv2, w_out, w_pe, w_pg, b_pg, ln_g, ln_b):
    B, S, D = x.shape
    assert D == D_MODEL and S % 512 == 0 and S >= WINDOW + QB and w_in.shape[0] == DEPTH == 1
    nb = S // QB
    nch = S // CMP_STRIDE
    row = lambda a: a.reshape(1, -1)

    bias_c, bias_w = _bias_tables(rel_bias, nb, nch)

    w_perm = _permute_w_in(w_in[0])
    (qn, kcr, vcr, ksa, vsat, kvw, kvwt, zn, qkg, vg, zg, misc) = _in_projection(
        x, row(ln0_g), row(ln0_b), w_perm, TOKEN_TILE)

    def chunks(a):
        return a.reshape(B, nch, CMP_STRIDE, NSA_G, DH).transpose(0, 3, 1, 2, 4).reshape(
            B, NSA_G, nch, CMP_STRIDE * DH)
    pos8 = jnp.broadcast_to(pos_cmp[0].reshape(1, CMP_LEN * DH), (8, CMP_LEN * DH)).astype(BF16)
    zpad = jnp.zeros((CMP_HIDDEN, DH), F32)
    wk2p = jnp.concatenate([w_ck2[0], zpad], axis=1).astype(BF16)
    wv2p = jnp.concatenate([zpad, w_cv2[0]], axis=1).astype(BF16)
    kvc, kvct = _compress(chunks(kcr), chunks(vcr), pos8, w_ck1[0].astype(BF16), row(b_ck1[0]), wk2p,
                          w_cv1[0].astype(BF16), row(b_cv1[0]), wv2p)

    o_nsa = _nsa(qn, ksa, vsat, kvw, kvwt, kvc, kvct, bias_w, bias_c, misc)

    wa = w_a2[0]
    wa_big = jnp.zeros((GLA_HEADS, LANES, LANES), F32)
    for h in range(GLA_HEADS):
        wh = wa[:, h * GLA_DK:(h + 1) * GLA_DK]
        wa_big = wa_big.at[h, MISC_ALOW:MISC_ALOW + GLA_RANK, :].set(jnp.concatenate([wh, wh], axis=1))
    ba = b_a[0].reshape(GLA_HEADS, 1, GLA_DK)
    ba2 = jnp.concatenate([ba, ba], axis=2)
    tri = jnp.asarray(np.tril(np.ones((GLA_CHUNK, GLA_CHUNK), np.float32))).astype(BF16)
    o_gla = _gla(qkg, vg, zg, misc, wa_big.astype(BF16), ba2, row(gla_norm_w[0]), tri, 512)

    return _out_projection(x, row(ln0_g), row(ln0_b), o_nsa, zn, o_gla, p[0], w_out[0].astype(BF16),
                           w_pg[0].astype(BF16), row(b_pg[0]), w_pe[0].astype(BF16), row(ln_g[0]),
                           row(ln_b[0]), TOKEN_TILE)
```

```python
import math

import numpy as np
import jax
import jax.numpy as jnp
from jax import lax
from jax.experimental import pallas as pl
from jax.experimental.pallas import tpu as pltpu

F32 = jnp.float32
BF16 = jnp.bfloat16

D_MODEL = 1024
PLE_DIM = 256
NSA_HEADS = 8
NSA_G = 2
NSA_R = NSA_HEADS // NSA_G
DH = 64
CMP_LEN = 32
CMP_STRIDE = 16
CMP_HIDDEN = 256
SEL_BLOCK = 64
TOP_N = 16
N_LOCAL = 2
WINDOW = 512
QB = 256
GLA_HEADS = 4
GLA_DK = 64
GLA_DV = 128
GLA_RANK = 16
GLA_TAU = 16.0
GLA_CHUNK = 64
NUM_BUCKETS = 32
MAX_DISTANCE = 128
DEPTH = 1
ALPHA = (2.0 * DEPTH) ** 0.25
EPS = 1e-5
NEG = -1e30
POS = 1e30

LANES = 128
VMEM_LIMIT = 56 * 1024 * 1024

TOKEN_TILE = 512
GLA_TILE = 2048
SEL_TILE = 256
SWEEP_GROUP = 8
VROWS = 80
SUPER = 64 * SEL_BLOCK
TW_WIDTH = WINDOW + QB + WINDOW
ROWS = NSA_R * QB

C_QN, C_KVC, C_KSVS, C_KVW, C_ZN, C_QKG, C_VG, C_ZG, C_MISC, C_END = (
    0, 512, 768, 1024, 1280, 1792, 2304, 2816, 3328, 3456)
MISC_GATES = 0
MISC_ALOW = 24


def _bucket_thresholds():
    d = np.arange(0, 4 * MAX_DISTANCE)
    max_exact = NUM_BUCKETS // 2
    nf = np.maximum(d, 1).astype(np.float32)
    large = max_exact + (np.log(nf / np.float32(max_exact)) / np.float32(math.log(MAX_DISTANCE / max_exact))
                         * np.float32(NUM_BUCKETS - max_exact)).astype(np.int32)
    large = np.minimum(large, NUM_BUCKETS - 1)
    bucket = np.where(d < max_exact, d, large)
    assert np.all(np.diff(bucket) >= 0) and bucket[-1] == NUM_BUCKETS - 1
    return [int(np.argmax(bucket >= k)) for k in range(NUM_BUCKETS)]


_THR = _bucket_thresholds()
CMP_BAND = 32
assert (QB - 1 + _THR[NUM_BUCKETS - 1]) // CMP_STRIDE + 1 + 7 <= CMP_BAND
assert SEL_TILE % QB == 0 and _THR[NUM_BUCKETS - 1] <= QB


def _dot(a, b, **kw):
    return jnp.dot(a, b, preferred_element_type=F32, **kw)


def _dot_nt(a, b, **kw):
    return lax.dot_general(a, b, (((1,), (1,)), ((), ())), preferred_element_type=F32, **kw)


def _dot_tn(a, b, **kw):
    return lax.dot_general(a, b, (((0,), (0,)), ((), ())), preferred_element_type=F32, **kw)


def _layer_norm(x, g, b):
    mu = jnp.mean(x, axis=-1, keepdims=True)
    xc = x - mu
    var = jnp.mean(xc * xc, axis=-1, keepdims=True)
    return xc * lax.rsqrt(var + EPS) * g + b


def _sigmoid(x):
    return 1.0 / (1.0 + jnp.exp(-x))


def _silu(x):
    return x * _sigmoid(x)


def _bias_from_dist(rb_ref, dist, valid, put):
    masks = [dist >= _THR[k] for k in range(1, NUM_BUCKETS)]
    for h in range(NSA_HEADS):
        val = jnp.full(dist.shape, rb_ref[0, h], F32)
        for k in range(1, NUM_BUCKETS):
            val = jnp.where(masks[k - 1], rb_ref[k, h], val)
        val = val - rb_ref[NUM_BUCKETS - 1, h]
        put(h, jnp.where(valid, val, NEG))


def _bias_cmp_kernel(rb_ref, out_ref):
    i = pl.program_id(0)
    nc = out_ref.shape[2]
    big = CMP_STRIDE * QB
    j_zero_max = (QB * i - (CMP_LEN - 1) - _THR[NUM_BUCKETS - 1] + CMP_STRIDE * big) // CMP_STRIDE - big
    start = jnp.clip((j_zero_max + 1 + 8 * big) // 8 * 8 - 8 * big, 0, nc - CMP_BAND)
    start = pl.multiple_of(start, 8)
    j_all = lax.broadcasted_iota(jnp.int32, (nc, QB), 0)
    fill = jnp.where(j_all <= j_zero_max, 0.0, NEG)
    for h in range(NSA_HEADS):
        out_ref[0, h] = fill
    j = start + lax.broadcasted_iota(jnp.int32, (CMP_BAND, QB), 0)
    a = lax.broadcasted_iota(jnp.int32, (CMP_BAND, QB), 1)
    dist = a + QB * i - CMP_STRIDE * j - (CMP_LEN - 1)

    def put(h, v):
        out_ref[0, h, pl.ds(start, CMP_BAND), :] = v
    _bias_from_dist(rb_ref, dist, dist >= 0, put)


def _bias_win_kernel(rb_ref, out_ref):
    shape = out_ref.shape[1:]
    y = lax.broadcasted_iota(jnp.int32, shape, 0)
    a = lax.broadcasted_iota(jnp.int32, shape, 1)
    dist = a + WINDOW - y

    def put(h, v):
        out_ref[h] = v
    _bias_from_dist(rb_ref, dist, (dist >= 0) & (dist < WINDOW), put)


def _bias_tables(rel_bias, nb, nc):
    smem = pl.BlockSpec(memory_space=pltpu.SMEM)
    bias_c = pl.pallas_call(
        _bias_cmp_kernel,
        grid=(nb,),
        in_specs=[smem],
        out_specs=pl.BlockSpec((1, NSA_HEADS, nc, QB), lambda i: (i, 0, 0, 0)),
        out_shape=jax.ShapeDtypeStruct((nb, NSA_HEADS, nc, QB), F32),
        compiler_params=pltpu.CompilerParams(dimension_semantics=("parallel",)),
        name="bias_cmp",
    )(rel_bias)
    bias_w = pl.pallas_call(
        _bias_win_kernel,
        in_specs=[smem],
        out_shape=jax.ShapeDtypeStruct((NSA_HEADS, TW_WIDTH, QB), F32),
        name="bias_win",
    )(rel_bias)
    return bias_c, bias_w


def _inproj_kernel(x_ref, g_ref, b_ref, w_ref, hn_ref, qn_ref, xkv_ref, ksa_ref, vsat_ref, kvw_ref, vwat_ref,
                   zs_ref, qkg_ref, vg_ref, zg_ref, misc_ref, slab_ref):
    t = pl.program_id(1)
    tm = x_ref.shape[1]
    hf = _layer_norm(x_ref[0], g_ref[...], b_ref[...])
    hn_ref[0] = hf
    h = hf.astype(BF16)

    def mm(c0, c1):
        return _dot(h, w_ref[:, c0:c1])

    qn_ref[0] = mm(C_QN, C_KVC).astype(BF16)

    kvc = mm(C_KVC, C_KSVS)
    for g in range(NSA_G):
        slab_ref[...] = kvc[:, g * LANES:(g + 1) * LANES]
        for tok in range(CMP_STRIDE):
            rows = slab_ref[pl.ds(tok, tm // CMP_STRIDE, stride=CMP_STRIDE), :]
            xkv_ref[0, g, :, tok * LANES:(tok + 1) * LANES] = rows.astype(BF16)

    lane = lax.broadcasted_iota(jnp.int32, (tm, LANES), 1)
    row = lax.broadcasted_iota(jnp.int32, (tm, LANES), 0)
    low = lane < DH
    blk = ((t * tm + row) // SEL_BLOCK) % (SUPER // SEL_BLOCK)
    onehot = jnp.where(lane - DH == blk, 1.0, 0.0)
    ones_col = jnp.where(lane == DH, 1.0, 0.0)

    def values_t(v_low):
        return jnp.where(low, v_low, ones_col).T[0:VROWS].astype(BF16)

    ksvs = mm(C_KSVS, C_KVW)
    ks, vs = ksvs[:, 0:LANES], ksvs[:, LANES:2 * LANES]
    ksa_ref[0, 0] = jnp.where(low, ks, onehot).astype(BF16)
    ksa_ref[0, 1] = jnp.where(low, pltpu.roll(ks, DH, 1), onehot).astype(BF16)
    vsat_ref[0, 0] = values_t(vs)
    vsat_ref[0, 1] = values_t(pltpu.roll(vs, DH, 1))
    kvw = mm(C_KVW, C_ZN)
    for g in range(NSA_G):
        kvw_g = kvw[:, g * LANES:(g + 1) * LANES]
        kvw_ref[0, g] = kvw_g.astype(BF16)
        vwat_ref[0, g] = values_t(pltpu.roll(kvw_g, DH, 1))
    zs_ref[0] = _silu(mm(C_ZN, C_QKG)).astype(BF16)
    qkg_ref[0] = mm(C_QKG, C_VG)
    vg_ref[0] = mm(C_VG, C_ZG)
    zg_ref[0] = mm(C_ZG, C_MISC)
    misc_ref[0] = mm(C_MISC, C_END)


def _in_projection(x, ln0_g, ln0_b, w_perm, tm):
    B, S, D = x.shape
    tok = lambda w: pl.BlockSpec((1, tm, w), lambda b, t: (b, t, 0))
    grp = pl.BlockSpec((1, NSA_G, tm, LANES), lambda b, t: (b, 0, t, 0))
    grp_v = pl.BlockSpec((1, NSA_G, VROWS, tm), lambda b, t: (b, 0, 0, t))
    grp_x = pl.BlockSpec((1, NSA_G, tm // CMP_STRIDE, CMP_STRIDE * LANES), lambda b, t: (b, 0, t, 0))
    full2 = lambda a: pl.BlockSpec(a.shape, lambda b, t: (0, 0))
    sds = jax.ShapeDtypeStruct
    return pl.pallas_call(
        _inproj_kernel,
        grid=(B, S // tm),
        in_specs=[tok(D), full2(ln0_g), full2(ln0_b), full2(w_perm)],
        out_specs=[tok(D), tok(512), grp_x, grp, grp_v, grp, grp_v, tok(512), tok(512), tok(512), tok(512),
                   tok(LANES)],
        out_shape=[sds((B, S, D), F32), sds((B, S, 512), BF16),
                   sds((B, NSA_G, S // CMP_STRIDE, CMP_STRIDE * LANES), BF16),
                   sds((B, NSA_G, S, LANES), BF16), sds((B, NSA_G, VROWS, S), BF16),
                   sds((B, NSA_G, S, LANES), BF16), sds((B, NSA_G, VROWS, S), BF16),
                   sds((B, S, 512), BF16), sds((B, S, 512), F32), sds((B, S, 512), F32), sds((B, S, 512), F32),
                   sds((B, S, LANES), F32)],
        scratch_shapes=[pltpu.VMEM((tm, LANES), F32)],
        compiler_params=pltpu.CompilerParams(dimension_semantics=("parallel", "parallel"),
                                             vmem_limit_bytes=VMEM_LIMIT),
        name="ln0_inproj",
    )(x, ln0_g, ln0_b, w_perm)


def _gelu_tanh(x):
    c = math.sqrt(2.0 / math.pi)
    return x * (0.5 * (1.0 + jnp.tanh(c * (x + 0.044715 * (x * x * x)))))


def _compress_kernel(x_ref, ptop_ref, pbot_ref, wtop_ref, wbot_ref, b1_ref, w2_ref, out_ref, outt_ref):
    nch = x_ref.shape[2]
    x = x_ref[0, 0]
    top = _dot(x, wtop_ref[...])
    bot = _dot(x, wbot_ref[...])
    cpos = (_dot(ptop_ref[...], wtop_ref[...]) + _dot(pbot_ref[...], wbot_ref[...]))[0:1, :]
    pre = top + pltpu.roll(bot, nch - 1, 0) + cpos + b1_ref[...]
    out = _dot(_gelu_tanh(pre).astype(BF16), w2_ref[...])
    out_ref[0, 0] = out.astype(BF16)
    outt_ref[0, 0] = out.T.astype(BF16)


def _compress(xkv, ptop, pbot, wtop, wbot, b1, w2):
    B, G, nch, width = xkv.shape
    full2 = lambda a: pl.BlockSpec(a.shape, lambda b, g: (0, 0))
    return pl.pallas_call(
        _compress_kernel,
        grid=(B, G),
        in_specs=[pl.BlockSpec((1, 1, nch, width), lambda b, g: (b, g, 0, 0)),
                  full2(ptop), full2(pbot), full2(wtop), full2(wbot), full2(b1), full2(w2)],
        out_specs=[pl.BlockSpec((1, 1, nch, LANES), lambda b, g: (b, g, 0, 0)),
                   pl.BlockSpec((1, 1, LANES, nch), lambda b, g: (b, g, 0, 0))],
        out_shape=[jax.ShapeDtypeStruct((B, G, nch, LANES), BF16),
                   jax.ShapeDtypeStruct((B, G, LANES, nch), BF16)],
        compiler_params=pltpu.CompilerParams(dimension_semantics=("parallel", "parallel"),
                                             vmem_limit_bytes=VMEM_LIMIT),
        name="kv_compress",
    )(xkv, ptop, pbot, wtop, wbot, b1, w2)


def _compress_params(pos, wk1, bk1, wk2, wv1, bv1, wv2):
    half = CMP_STRIDE * DH
    zero = jnp.zeros((CMP_STRIDE, DH, CMP_HIDDEN), F32)

    def rows(w_k, w_v):
        k_rows = jnp.concatenate([w_k.reshape(CMP_STRIDE, DH, CMP_HIDDEN), zero], axis=2)
        v_rows = jnp.concatenate([zero, w_v.reshape(CMP_STRIDE, DH, CMP_HIDDEN)], axis=2)
        return jnp.concatenate([k_rows, v_rows], axis=1).reshape(CMP_STRIDE * LANES, 2 * CMP_HIDDEN).astype(BF16)

    def pos_row(p):
        r = jnp.concatenate([p, p], axis=1).reshape(1, CMP_STRIDE * LANES)
        return jnp.broadcast_to(r, (8, CMP_STRIDE * LANES)).astype(BF16)

    zpad = jnp.zeros((CMP_HIDDEN, DH), F32)
    w2 = jnp.concatenate([jnp.concatenate([wk2, zpad], axis=1),
                          jnp.concatenate([zpad, wv2], axis=1)], axis=0).astype(BF16)
    b1 = jnp.concatenate([bk1, bv1]).reshape(1, -1)
    return (pos_row(pos[:CMP_STRIDE]), pos_row(pos[CMP_STRIDE:]), rows(wk1[:half], wv1[:half]),
            rows(wk1[half:], wv1[half:]), b1, w2)


def _heads_on_lanes(t4):
    return jnp.concatenate([t4[r] for r in range(NSA_R)], axis=1)


def _nsa_kernel(q_ref, ksa_ref, vsat_ref, kvw_ref, vwat_ref, kvc_ref, kvct_ref, tw_ref, bc_ref, misc_ref,
                o_ref, qa_ref, m_ref, acc_ref, s_ref, ps_ref):
    g = pl.program_id(1)
    i = pl.program_id(2)
    t0 = i * QB
    jd1 = (t0 + QB - 1) // SEL_TILE
    n_far = jnp.maximum(t0 - (_THR[NUM_BUCKETS - 1] - 1), 0) // SEL_TILE

    def score_tile(j, ntiles, bias_off):
        n = ntiles * SEL_TILE
        k0 = pl.multiple_of(j * SEL_TILE, SEL_TILE)
        sc = _dot(ksa_ref[0, 0, pl.ds(k0, n), :], qa_ref[j // (SUPER // SEL_TILE)])
        if bias_off is not None:
            off = pl.multiple_of(bias_off, LANES)
            sc = sc + _heads_on_lanes(tw_ref[:, pl.ds(off, n), :])
        s_ref[pl.ds(k0, n), :] = sc.astype(s_ref.dtype)
        return jnp.max(sc, axis=0, keepdims=True)

    def pv_tile(j, ntiles):
        n = ntiles * SEL_TILE
        k0 = pl.multiple_of(j * SEL_TILE, SEL_TILE)
        pt = jnp.exp(s_ref[pl.ds(k0, n), :] - m_ref[...].astype(s_ref.dtype))
        return _dot(vsat_ref[0, 0, :, pl.ds(k0, n)], pt)

    def sweep(total, block):
        n_grp = total // SWEEP_GROUP

        def body(jj, carry):
            block(SWEEP_GROUP * jj, SWEEP_GROUP)
            return carry
        lax.fori_loop(0, n_grp, body, 0)
        size = SWEEP_GROUP // 2
        while size >= 1:
            first = (total // (2 * size)) * (2 * size)

            @pl.when((total // size) % 2 == 1)
            def _(first=first, size=size):
                block(first, size)
            size //= 2

    def score_block(j, ntiles):
        mm = m_ref[...]
        for h in range(0, ntiles, 2):
            mm = jnp.maximum(mm, score_tile(j + h, min(2, ntiles - h), None))
        m_ref[...] = mm

    def pv_block(j, ntiles):
        part = None
        for h in range(0, ntiles, 2):
            pv = pv_tile(j + h, min(2, ntiles - h))
            part = pv if part is None else part + pv
        acc_ref[...] += part

    def front():
        qt = q_ref[0].astype(F32).T
        q_t = jnp.concatenate([qt[r * DH:(r + 1) * DH] for r in range(NSA_R)], axis=1)
        qpad_t = jnp.concatenate([q_t, jnp.zeros((DH, ROWS), F32)], axis=0).astype(BF16)

        bc = _heads_on_lanes(bc_ref[0])
        s = _dot(kvc_ref[0, 0], qpad_t) + bc
        m = jnp.max(s, axis=0, keepdims=True)
        p = jnp.exp(s - m)
        l = jnp.sum(p, axis=0, keepdims=True)
        pn = p * jnp.where(m > 0.5 * NEG, 1.0 / l, 0.0)
        o_c = _dot(kvct_ref[0, 0], pn.astype(BF16))[DH:2 * DH]

        wlen = WINDOW + QB
        start = pl.multiple_of(jnp.maximum(t0 - WINDOW, 0), LANES)
        offw = pl.multiple_of(start - t0 + WINDOW, LANES)
        sw = _dot(kvw_ref[0, 0, pl.ds(start, wlen), :], qpad_t) + _heads_on_lanes(tw_ref[:, pl.ds(offw, wlen), :])
        mw = jnp.max(sw, axis=0, keepdims=True)
        pw = jnp.exp((sw - mw).astype(BF16))
        acc_w = _dot(vwat_ref[0, 0, :, pl.ds(start, wlen)], pw)
        o_w = acc_w[0:DH] / acc_w[DH:DH + 1]

        sg = _sigmoid(misc_ref[0]).T

        def gate_row(c):
            rows = []
            for r in range(NSA_R):
                c0 = MISC_GATES + 3 * r + c
                c1 = c0 + 3 * NSA_R
                rows.append(jnp.where(g == 0, sg[c0:c0 + 1], sg[c1:c1 + 1]))
            return jnp.concatenate(rows, axis=1)
        partial = gate_row(0) * o_c + gate_row(2) * o_w
        gate_sel = gate_row(1)

        nc = pn.shape[0]
        nsel = nc * CMP_STRIDE // SEL_BLOCK
        per_blk = SEL_BLOCK // CMP_STRIDE
        psum = pn[:, 0:QB] + pn[:, QB:2 * QB] + pn[:, 2 * QB:3 * QB] + pn[:, 3 * QB:4 * QB]
        for c in range(QB // LANES):
            ps_ref[c, 0:8] = jnp.zeros((8, LANES), F32)
            ps_ref[c, 8:8 + nc] = psum[:, c * LANES:(c + 1) * LANES]
        tok = lambda k: jnp.concatenate(
            [ps_ref.at[c][pl.ds(8 + k, nsel, stride=per_blk), :] for c in range(QB // LANES)], axis=1)
        imp = 0.5 * (tok(-1) + tok(3)) + tok(0) + tok(1) + tok(2)
        nidx = lax.broadcasted_iota(jnp.int32, (nsel, QB), 0)
        qidx = lax.broadcasted_iota(jnp.int32, (nsel, QB), 1)
        cur = (t0 + qidx) // SEL_BLOCK
        causal = nidx <= cur
        forced = causal & ((nidx == 0) | (nidx >= cur - (N_LOCAL - 1)))
        n_forced = N_LOCAL + 1
        w = jnp.where(causal & jnp.logical_not(forced), imp, -3e38)
        nf = nidx.astype(F32)
        sel = jnp.where(forced | (causal & (cur < TOP_N)), 1.0, 0.0)
        for _ in range(min(TOP_N - n_forced, nsel)):
            mx = jnp.max(w, axis=0, keepdims=True)
            first = jnp.min(jnp.where(w == mx, nf, float(nsel)), axis=0, keepdims=True)
            pick = nf == first
            sel = jnp.where(pick, 1.0, sel)
            w = jnp.where(pick, -3e38, w)
        sel_bias = jnp.where((sel > 0.5) & causal, 0.0, NEG)

        for u in range(qa_ref.shape[0]):
            sb = sel_bias[u * 64:(u + 1) * 64]
            if sb.shape[0] < 64:
                sb = jnp.concatenate([sb, jnp.full((64 - sb.shape[0], QB), NEG, F32)], axis=0)
            sb4 = jnp.concatenate([sb] * NSA_R, axis=1)
            qa_ref[u] = jnp.concatenate([q_t, sb4], axis=0).astype(BF16)
        return partial, gate_sel

    partial, gate_sel = front()

    m_ref[...] = jnp.full(m_ref.shape, NEG, F32)
    sweep(n_far, score_block)
    m_lo = score_tile(n_far, 1, WINDOW - (t0 - n_far * SEL_TILE))
    m_hi = score_tile(jd1, 1, WINDOW - (t0 - jd1 * SEL_TILE))
    m_ref[...] = jnp.maximum(m_ref[...], jnp.maximum(m_lo, m_hi))

    acc_ref[...] = jnp.zeros(acc_ref.shape, F32)
    sweep(jd1 + 1, pv_block)
    acc = acc_ref[...]
    o_s = acc[0:DH] / acc[DH:DH + 1]
    out_t = partial + gate_sel * o_s
    out_t = jnp.concatenate([out_t[:, r * QB:(r + 1) * QB] for r in range(NSA_R)], axis=0)
    o_ref[0] = out_t.T.astype(o_ref.dtype)


def _nsa(qn, ksa, vsat, kvw, vwat, kvc, kvct, bias_w, bias_c, misc):
    B, S, _ = qn.shape
    nb = S // QB
    nc = kvc.shape[2]
    n_super = -(-S // SUPER)
    seq = lambda a: pl.BlockSpec((1, 1) + a.shape[2:], lambda b, g, i: (b, g, 0, 0))
    return pl.pallas_call(
        _nsa_kernel,
        grid=(B, NSA_G, nb),
        in_specs=[
            pl.BlockSpec((1, QB, NSA_R * DH), lambda b, g, i: (b, i, g)),
            seq(ksa), seq(vsat), seq(kvw), seq(vwat), seq(kvc), seq(kvct),
            pl.BlockSpec((NSA_R, TW_WIDTH, QB), lambda b, g, i: (g, 0, 0)),
            pl.BlockSpec((1, NSA_R, nc, QB), lambda b, g, i: (i, g, 0, 0)),
            pl.BlockSpec((1, QB, LANES), lambda b, g, i: (b, i, 0)),
        ],
        out_specs=pl.BlockSpec((1, QB, NSA_R * DH), lambda b, g, i: (b, i, g)),
        out_shape=jax.ShapeDtypeStruct((B, S, NSA_HEADS * DH), BF16),
        scratch_shapes=[pltpu.VMEM((n_super, LANES, ROWS), BF16),
                        pltpu.VMEM((1, ROWS), F32),
                        pltpu.VMEM((VROWS, ROWS), F32),
                        pltpu.VMEM((S, ROWS), BF16),
                        pltpu.VMEM((QB // LANES, nc + 8, LANES), F32)],
        compiler_params=pltpu.CompilerParams(dimension_semantics=("parallel", "parallel", "arbitrary"),
                                             vmem_limit_bytes=VMEM_LIMIT),
        name="nsa_attention",
    )(qn, ksa, vsat, kvw, vwat, kvc, kvct, bias_w, bias_c, misc)


def _gla_kernel(qk_ref, v_ref, z_ref, misc_ref, wa_ref, ba_ref, nw_ref, tri_ref, o_ref, st_ref):
    t = pl.program_id(2)
    T = qk_ref.shape[1]
    C = GLA_CHUNK

    @pl.when(t == 0)
    def _():
        st_ref[...] = jnp.zeros(st_ref.shape, F32)

    lane = lax.broadcasted_iota(jnp.int32, (1, LANES), 1)
    sign = jnp.where(lane < GLA_DK, 1.0 / GLA_TAU, -1.0 / GLA_TAU)
    zz = _dot(misc_ref[0].astype(BF16), wa_ref[0]) + ba_ref[0]
    log_sig = jnp.minimum(zz, 0.0) - jnp.log(1.0 + jnp.exp(-jnp.abs(zz)))
    la2 = log_sig * sign
    rr = lax.broadcasted_iota(jnp.int32, (C, C), 0)
    cc = lax.broadcasted_iota(jnp.int32, (C, C), 1)
    tril = rr >= cc
    scale = GLA_DK ** -0.5

    chunks = [slice(n * C, (n + 1) * C) for n in range(T // C)]
    hi = la2.astype(BF16)
    rem = la2 - hi.astype(F32)
    mid = rem.astype(BF16)
    lo = (rem - mid.astype(F32)).astype(BF16)
    tri = tri_ref[...]
    b2 = jnp.concatenate([_dot(tri, hi[sl]) + _dot(tri, mid[sl]) + _dot(tri, lo[sl]) for sl in chunks], axis=0)
    e = jnp.exp(b2)
    qke = qk_ref[0] * e
    q_e = (qke[:, 0:GLA_DK] * scale).astype(BF16)
    k_e = qke[:, GLA_DK:2 * GLA_DK].astype(BF16)
    v = v_ref[0].astype(BF16)
    attn = [jnp.where(tril, _dot_nt(q_e[sl], k_e[sl]), 0.0).astype(BF16) for sl in chunks]
    o_intra = [_dot(a, v[sl]) for a, sl in zip(attn, chunks)]
    kv = [_dot_tn(v[sl], k_e[sl]) for sl in chunks]
    st = st_ref[...]
    states = []
    for n, sl in enumerate(chunks):
        states.append(st.astype(BF16))
        st = (st + kv[n]) * e[sl.stop - 1:sl.stop, 0:GLA_DK]
    st_ref[...] = st
    o = jnp.concatenate([oi + _dot_nt(q_e[sl], s_in) for oi, sl, s_in in zip(o_intra, chunks, states)], axis=0)
    y = o * lax.rsqrt(jnp.mean(o * o, axis=-1, keepdims=True) + EPS) * nw_ref[...]
    o_ref[0] = (y * _silu(z_ref[0])).astype(o_ref.dtype)


def _gla(qkg, vg, zg, misc, wa_big, ba2, norm_w, tri, T):
    B, S, _ = qkg.shape
    hb = pl.BlockSpec((1, T, LANES), lambda b, h, t: (b, t, h))
    return pl.pallas_call(
        _gla_kernel,
        grid=(B, GLA_HEADS, S // T),
        in_specs=[hb, hb, hb,
                  pl.BlockSpec((1, T, LANES), lambda b, h, t: (b, t, 0)),
                  pl.BlockSpec((1, LANES, LANES), lambda b, h, t: (h, 0, 0)),
                  pl.BlockSpec((1, 1, LANES), lambda b, h, t: (h, 0, 0)),
                  pl.BlockSpec(norm_w.shape, lambda b, h, t: (0, 0)),
                  pl.BlockSpec(tri.shape, lambda b, h, t: (0, 0))],
        out_specs=hb,
        out_shape=jax.ShapeDtypeStruct((B, S, GLA_HEADS * GLA_DV), BF16),
        scratch_shapes=[pltpu.VMEM((GLA_DV, GLA_DK), F32)],
        compiler_params=pltpu.CompilerParams(dimension_semantics=("parallel", "parallel", "arbitrary"),
                                             vmem_limit_bytes=VMEM_LIMIT),
        name="gla_chunked",
    )(qkg, vg, zg, misc, wa_big, ba2, norm_w, tri)


def _out_kernel(hn_ref, on_ref, zs_ref, og_ref, p_ref, wo_ref, wpg_ref, bpg_ref, wpe_ref, lg_ref, lb_ref, o_ref):
    half = on_ref.shape[2]
    mix_n = on_ref[0] * zs_ref[0]
    y = _dot(mix_n, wo_ref[0:half, :]) + _dot(og_ref[0], wo_ref[half:2 * half, :])
    r = ALPHA * hn_ref[0] + y
    gate = _sigmoid(_dot(r.astype(BF16), wpg_ref[...]) + bpg_ref[...])
    r = r + gate * _dot(p_ref[0].astype(BF16), wpe_ref[...])
    o_ref[0] = _layer_norm(r, lg_ref[...], lb_ref[...])


def _out_projection(hn, o_nsa, zs, o_gla, p, w_out, w_pg, b_pg, w_pe, ln_g, ln_b, tm):
    B, S, D = hn.shape
    tok = lambda w: pl.BlockSpec((1, tm, w), lambda b, t: (b, t, 0))
    full2 = lambda a: pl.BlockSpec(a.shape, lambda b, t: (0, 0))
    return pl.pallas_call(
        _out_kernel,
        grid=(B, S // tm),
        in_specs=[tok(D), tok(512), tok(512), tok(512), tok(PLE_DIM),
                  full2(w_out), full2(w_pg), full2(b_pg), full2(w_pe), full2(ln_g), full2(ln_b)],
        out_specs=tok(D),
        out_shape=jax.ShapeDtypeStruct((B, S, D), F32),
        compiler_params=pltpu.CompilerParams(dimension_semantics=("parallel", "parallel"),
                                             vmem_limit_bytes=VMEM_LIMIT),
        name="out_proj_deepnorm",
    )(hn, o_nsa, zs, o_gla, p, w_out, w_pg, b_pg, w_pe, ln_g, ln_b)


def _permute_w_in(w):
    widths = (512, 128, 128, 128, 128, 128, 128, 24, 512, 256, 256, 512, 16, 512)
    offs = np.concatenate([[0], np.cumsum(widths)])
    (q_n, kc, vc, ks, vs, kw, vw, gates, z_n, q_g, k_g, v_g, a_low, z_g) = [
        w[:, int(offs[k]):int(offs[k + 1])] for k in range(len(widths))]
    pair = lambda a, b: [jnp.concatenate([a[:, g * DH:(g + 1) * DH], b[:, g * DH:(g + 1) * DH]], axis=1)
                         for g in range(NSA_G)]
    kvc = pair(kc, vc)
    kvw = pair(kw, vw)
    qk = jnp.concatenate([jnp.concatenate([q_g[:, h * GLA_DK:(h + 1) * GLA_DK],
                                           k_g[:, h * GLA_DK:(h + 1) * GLA_DK]], axis=1)
                          for h in range(GLA_HEADS)], axis=1)
    misc = jnp.concatenate([gates, a_low,
                            jnp.zeros((w.shape[0], LANES - gates.shape[1] - a_low.shape[1]), w.dtype)], axis=1)
    cols = [q_n * (DH ** -0.5), kvc[0], kvc[1], ks, vs, kvw[0], kvw[1], z_n, qk, v_g, z_g, misc]
    return jnp.concatenate(cols, axis=1).astype(BF16)


def kernel(x, p, ln0_g, ln0_b, rel_bias, w_in, w_a2, b_a, gla_norm_w, pos_cmp, w_ck1, b_ck1, w_ck2,
           w_cv1, b_cv1, w_cv2, w_out, w_pe, w_pg, b_pg, ln_g, ln_b):
    B, S, D = x.shape
    assert D == D_MODEL and S % 512 == 0 and S >= WINDOW + QB and w_in.shape[0] == DEPTH == 1
    nb = S // QB
    nch = S // CMP_STRIDE
    row = lambda a: a.reshape(1, -1)

    bias_c, bias_w = _bias_tables(rel_bias, nb, nch)

    w_perm = _permute_w_in(w_in[0])
    (hn, qn, xkv, ksa, vsat, kvw, vwat, zs, qkg, vg, zg, misc) = _in_projection(
        x, row(ln0_g), row(ln0_b), w_perm, TOKEN_TILE)

    kvc, kvct = _compress(xkv, *_compress_params(pos_cmp[0], w_ck1[0], b_ck1[0], w_ck2[0],
                                                 w_cv1[0], b_cv1[0], w_cv2[0]))

    o_nsa = _nsa(qn, ksa, vsat, kvw, vwat, kvc, kvct, bias_w, bias_c, misc)

    wa = w_a2[0]
    wa_big = jnp.zeros((GLA_HEADS, LANES, LANES), F32)
    for h in range(GLA_HEADS):
        wh = wa[:, h * GLA_DK:(h + 1) * GLA_DK]
        wa_big = wa_big.at[h, MISC_ALOW:MISC_ALOW + GLA_RANK, :].set(jnp.concatenate([wh, wh], axis=1))
    ba = b_a[0].reshape(GLA_HEADS, 1, GLA_DK)
    ba2 = jnp.concatenate([ba, ba], axis=2)
    tri = jnp.asarray(np.tril(np.ones((GLA_CHUNK, GLA_CHUNK), np.float32))).astype(BF16)
    gla_tile = math.gcd(S, GLA_TILE)
    o_gla = _gla(qkg, vg, zg, misc, wa_big.astype(BF16), ba2, row(gla_norm_w[0]), tri, gla_tile)

    return _out_projection(hn, o_nsa, zs, o_gla, p[0], w_out[0].astype(BF16), w_pg[0].astype(BF16),
                           row(b_pg[0]), w_pe[0].astype(BF16), row(ln_g[0]), row(ln_b[0]), TOKEN_TILE)
```

```python
import math

import numpy as np
import jax
import jax.numpy as jnp
from jax import lax
from jax.experimental import pallas as pl
from jax.experimental.pallas import tpu as pltpu

F32 = jnp.float32
BF16 = jnp.bfloat16

D_MODEL = 1024
PLE_DIM = 256
NSA_HEADS = 8
NSA_G = 2
NSA_R = NSA_HEADS // NSA_G
DH = 64
CMP_LEN = 32
CMP_STRIDE = 16
CMP_HIDDEN = 256
SEL_BLOCK = 64
TOP_N = 16
N_LOCAL = 2
WINDOW = 512
QB = 256
GLA_HEADS = 4
GLA_DK = 64
GLA_DV = 128
GLA_RANK = 16
GLA_TAU = 16.0
GLA_CHUNK = 64
NUM_BUCKETS = 32
MAX_DISTANCE = 128
DEPTH = 1
ALPHA = (2.0 * DEPTH) ** 0.25
EPS = 1e-5
NEG = -1e30
POS = 1e30

LANES = 128
VMEM_LIMIT = 56 * 1024 * 1024

TOKEN_TILE = 512
GLA_TILE = 4096
SEL_TILE = 256
SWEEP_GROUP = 16
VROWS = 80
SUPER = 64 * SEL_BLOCK
TW_WIDTH = WINDOW + QB + WINDOW
ROWS = NSA_R * QB

C_QN, C_KVC, C_KSVS, C_KVW, C_ZN, C_QKG, C_VG, C_ZG, C_MISC, C_END = (
    0, 512, 768, 1024, 1280, 1792, 2304, 2816, 3328, 3456)
MISC_GATES = 0
MISC_ALOW = 24


def _bucket_thresholds():
    d = np.arange(0, 4 * MAX_DISTANCE)
    max_exact = NUM_BUCKETS // 2
    nf = np.maximum(d, 1).astype(np.float32)
    large = max_exact + (np.log(nf / np.float32(max_exact)) / np.float32(math.log(MAX_DISTANCE / max_exact))
                         * np.float32(NUM_BUCKETS - max_exact)).astype(np.int32)
    large = np.minimum(large, NUM_BUCKETS - 1)
    bucket = np.where(d < max_exact, d, large)
    assert np.all(np.diff(bucket) >= 0) and bucket[-1] == NUM_BUCKETS - 1
    return [int(np.argmax(bucket >= k)) for k in range(NUM_BUCKETS)]


_THR = _bucket_thresholds()
CMP_BAND = 32
assert (QB - 1 + _THR[NUM_BUCKETS - 1]) // CMP_STRIDE + 1 + 7 <= CMP_BAND
assert SEL_TILE % QB == 0 and _THR[NUM_BUCKETS - 1] <= QB


def _dot(a, b, **kw):
    return jnp.dot(a, b, preferred_element_type=F32, **kw)


def _dot_nt(a, b, **kw):
    return lax.dot_general(a, b, (((1,), (1,)), ((), ())), preferred_element_type=F32, **kw)


def _dot_tn(a, b, **kw):
    return lax.dot_general(a, b, (((0,), (0,)), ((), ())), preferred_element_type=F32, **kw)


def _layer_norm(x, g, b):
    mu = jnp.mean(x, axis=-1, keepdims=True)
    xc = x - mu
    var = jnp.mean(xc * xc, axis=-1, keepdims=True)
    return xc * lax.rsqrt(var + EPS) * g + b


def _sigmoid(x):
    return 1.0 / (1.0 + jnp.exp(-x))


def _silu(x):
    return x * _sigmoid(x)


def _bias_from_dist(rb_ref, dist, valid, put):
    masks = [dist >= _THR[k] for k in range(1, NUM_BUCKETS)]
    for h in range(NSA_HEADS):
        val = jnp.full(dist.shape, rb_ref[0, h], F32)
        for k in range(1, NUM_BUCKETS):
            val = jnp.where(masks[k - 1], rb_ref[k, h], val)
        val = val - rb_ref[NUM_BUCKETS - 1, h]
        put(h, jnp.where(valid, val, NEG))


def _bias_cmp_kernel(rb_ref, out_ref):
    i = pl.program_id(0)
    nc = out_ref.shape[2]
    big = CMP_STRIDE * QB
    j_zero_max = (QB * i - (CMP_LEN - 1) - _THR[NUM_BUCKETS - 1] + CMP_STRIDE * big) // CMP_STRIDE - big
    start = jnp.clip((j_zero_max + 1 + 8 * big) // 8 * 8 - 8 * big, 0, nc - CMP_BAND)
    start = pl.multiple_of(start, 8)
    j_all = lax.broadcasted_iota(jnp.int32, (nc, QB), 0)
    fill = jnp.where(j_all <= j_zero_max, 0.0, NEG)
    for h in range(NSA_HEADS):
        out_ref[0, h] = fill
    j = start + lax.broadcasted_iota(jnp.int32, (CMP_BAND, QB), 0)
    a = lax.broadcasted_iota(jnp.int32, (CMP_BAND, QB), 1)
    dist = a + QB * i - CMP_STRIDE * j - (CMP_LEN - 1)

    def put(h, v):
        out_ref[0, h, pl.ds(start, CMP_BAND), :] = v
    _bias_from_dist(rb_ref, dist, dist >= 0, put)


def _bias_win_kernel(rb_ref, out_ref):
    shape = out_ref.shape[1:]
    y = lax.broadcasted_iota(jnp.int32, shape, 0)
    a = lax.broadcasted_iota(jnp.int32, shape, 1)
    dist = a + WINDOW - y

    def put(h, v):
        out_ref[h] = v
    _bias_from_dist(rb_ref, dist, (dist >= 0) & (dist < WINDOW), put)


def _bias_tables(rel_bias, nb, nc):
    smem = pl.BlockSpec(memory_space=pltpu.SMEM)
    bias_c = pl.pallas_call(
        _bias_cmp_kernel,
        grid=(nb,),
        in_specs=[smem],
        out_specs=pl.BlockSpec((1, NSA_HEADS, nc, QB), lambda i: (i, 0, 0, 0)),
        out_shape=jax.ShapeDtypeStruct((nb, NSA_HEADS, nc, QB), F32),
        compiler_params=pltpu.CompilerParams(dimension_semantics=("parallel",)),
        name="bias_cmp",
    )(rel_bias)
    bias_w = pl.pallas_call(
        _bias_win_kernel,
        in_specs=[smem],
        out_shape=jax.ShapeDtypeStruct((NSA_HEADS, TW_WIDTH, QB), F32),
        name="bias_win",
    )(rel_bias)
    return bias_c, bias_w


def _inproj_kernel(x_ref, g_ref, b_ref, w_ref, hn_ref, qn_ref, xkv_ref, ksa_ref, vsat_ref, kvw_ref, vwat_ref,
                   zs_ref, qkg_ref, vg_ref, zg_ref, misc_ref, slab_ref):
    t = pl.program_id(1)
    tm = x_ref.shape[1]
    hf = _layer_norm(x_ref[0], g_ref[...], b_ref[...])
    hn_ref[0] = ALPHA * hf
    h = hf.astype(BF16)

    def mm(c0, c1):
        return _dot(h, w_ref[:, c0:c1])

    qn_ref[0] = mm(C_QN, C_KVC).astype(BF16)

    kvc = mm(C_KVC, C_KSVS)
    for g in range(NSA_G):
        slab_ref[...] = kvc[:, g * LANES:(g + 1) * LANES]
        for tok in range(CMP_STRIDE):
            rows = slab_ref[pl.ds(tok, tm // CMP_STRIDE, stride=CMP_STRIDE), :]
            xkv_ref[0, g, :, tok * LANES:(tok + 1) * LANES] = rows.astype(BF16)

    lane = lax.broadcasted_iota(jnp.int32, (tm, LANES), 1)
    row = lax.broadcasted_iota(jnp.int32, (tm, LANES), 0)
    low = lane < DH
    blk = ((t * tm + row) // SEL_BLOCK) % (SUPER // SEL_BLOCK)
    onehot = jnp.where(lane - DH == blk, 1.0, 0.0)
    ones_col = jnp.where(lane == DH, 1.0, 0.0)

    def values_t(v_low):
        return jnp.where(low, v_low, ones_col).T[0:VROWS].astype(BF16)

    ksvs = mm(C_KSVS, C_KVW)
    ks, vs = ksvs[:, 0:LANES], ksvs[:, LANES:2 * LANES]
    ksa_ref[0, 0] = jnp.where(low, ks, onehot).astype(BF16)
    ksa_ref[0, 1] = jnp.where(low, pltpu.roll(ks, DH, 1), onehot).astype(BF16)
    vsat_ref[0, 0] = values_t(vs)
    vsat_ref[0, 1] = values_t(pltpu.roll(vs, DH, 1))
    kvw = mm(C_KVW, C_ZN)
    for g in range(NSA_G):
        kvw_g = kvw[:, g * LANES:(g + 1) * LANES]
        kvw_ref[0, g] = kvw_g.astype(BF16)
        vwat_ref[0, g] = values_t(pltpu.roll(kvw_g, DH, 1))
    zs_ref[0] = _silu(mm(C_ZN, C_QKG)).astype(BF16)
    qkg_ref[0] = mm(C_QKG, C_VG)
    vg_ref[0] = mm(C_VG, C_ZG)
    zg_ref[0] = mm(C_ZG, C_MISC)
    misc_ref[0] = mm(C_MISC, C_END)


def _in_projection(x, ln0_g, ln0_b, w_perm, tm):
    B, S, D = x.shape
    tok = lambda w: pl.BlockSpec((1, tm, w), lambda b, t: (b, t, 0))
    grp = pl.BlockSpec((1, NSA_G, tm, LANES), lambda b, t: (b, 0, t, 0))
    grp_v = pl.BlockSpec((1, NSA_G, VROWS, tm), lambda b, t: (b, 0, 0, t))
    grp_x = pl.BlockSpec((1, NSA_G, tm // CMP_STRIDE, CMP_STRIDE * LANES), lambda b, t: (b, 0, t, 0))
    full2 = lambda a: pl.BlockSpec(a.shape, lambda b, t: (0, 0))
    sds = jax.ShapeDtypeStruct
    return pl.pallas_call(
        _inproj_kernel,
        grid=(B, S // tm),
        in_specs=[tok(D), full2(ln0_g), full2(ln0_b), full2(w_perm)],
        out_specs=[tok(D), tok(512), grp_x, grp, grp_v, grp, grp_v, tok(512), tok(512), tok(512), tok(512),
                   tok(LANES)],
        out_shape=[sds((B, S, D), F32), sds((B, S, 512), BF16),
                   sds((B, NSA_G, S // CMP_STRIDE, CMP_STRIDE * LANES), BF16),
                   sds((B, NSA_G, S, LANES), BF16), sds((B, NSA_G, VROWS, S), BF16),
                   sds((B, NSA_G, S, LANES), BF16), sds((B, NSA_G, VROWS, S), BF16),
                   sds((B, S, 512), BF16), sds((B, S, 512), F32), sds((B, S, 512), F32), sds((B, S, 512), F32),
                   sds((B, S, LANES), F32)],
        scratch_shapes=[pltpu.VMEM((tm, LANES), F32)],
        compiler_params=pltpu.CompilerParams(dimension_semantics=("parallel", "parallel"),
                                             vmem_limit_bytes=VMEM_LIMIT),
        name="ln0_inproj",
    )(x, ln0_g, ln0_b, w_perm)


def _gelu_tanh(x):
    c = math.sqrt(2.0 / math.pi)
    return x * (0.5 * (1.0 + jnp.tanh(c * (x + 0.044715 * (x * x * x)))))


def _compress_kernel(x_ref, ptop_ref, pbot_ref, wtop_ref, wbot_ref, b1_ref, w2_ref, out_ref, outt_ref):
    nch = x_ref.shape[2]
    x = x_ref[0, 0]
    top = _dot(x, wtop_ref[...])
    bot = _dot(x, wbot_ref[...])
    cpos = (_dot(ptop_ref[...], wtop_ref[...]) + _dot(pbot_ref[...], wbot_ref[...]))[0:1, :]
    pre = top + pltpu.roll(bot, nch - 1, 0) + cpos + b1_ref[...]
    out = _dot(_gelu_tanh(pre).astype(BF16), w2_ref[...])
    out_ref[0, 0] = out.astype(BF16)
    outt_ref[0, 0] = out.T.astype(BF16)


def _compress(xkv, ptop, pbot, wtop, wbot, b1, w2):
    B, G, nch, width = xkv.shape
    full2 = lambda a: pl.BlockSpec(a.shape, lambda b, g: (0, 0))
    return pl.pallas_call(
        _compress_kernel,
        grid=(B, G),
        in_specs=[pl.BlockSpec((1, 1, nch, width), lambda b, g: (b, g, 0, 0)),
                  full2(ptop), full2(pbot), full2(wtop), full2(wbot), full2(b1), full2(w2)],
        out_specs=[pl.BlockSpec((1, 1, nch, LANES), lambda b, g: (b, g, 0, 0)),
                   pl.BlockSpec((1, 1, LANES, nch), lambda b, g: (b, g, 0, 0))],
        out_shape=[jax.ShapeDtypeStruct((B, G, nch, LANES), BF16),
                   jax.ShapeDtypeStruct((B, G, LANES, nch), BF16)],
        compiler_params=pltpu.CompilerParams(dimension_semantics=("parallel", "parallel"),
                                             vmem_limit_bytes=VMEM_LIMIT),
        name="kv_compress",
    )(xkv, ptop, pbot, wtop, wbot, b1, w2)


def _compress_params(pos, wk1, bk1, wk2, wv1, bv1, wv2):
    half = CMP_STRIDE * DH
    zero = jnp.zeros((CMP_STRIDE, DH, CMP_HIDDEN), F32)

    def rows(w_k, w_v):
        k_rows = jnp.concatenate([w_k.reshape(CMP_STRIDE, DH, CMP_HIDDEN), zero], axis=2)
        v_rows = jnp.concatenate([zero, w_v.reshape(CMP_STRIDE, DH, CMP_HIDDEN)], axis=2)
        return jnp.concatenate([k_rows, v_rows], axis=1).reshape(CMP_STRIDE * LANES, 2 * CMP_HIDDEN).astype(BF16)

    def pos_row(p):
        r = jnp.concatenate([p, p], axis=1).reshape(1, CMP_STRIDE * LANES)
        return jnp.broadcast_to(r, (8, CMP_STRIDE * LANES)).astype(BF16)

    zpad = jnp.zeros((CMP_HIDDEN, DH), F32)
    w2 = jnp.concatenate([jnp.concatenate([wk2, zpad], axis=1),
                          jnp.concatenate([zpad, wv2], axis=1)], axis=0).astype(BF16)
    b1 = jnp.concatenate([bk1, bv1]).reshape(1, -1)
    return (pos_row(pos[:CMP_STRIDE]), pos_row(pos[CMP_STRIDE:]), rows(wk1[:half], wv1[:half]),
            rows(wk1[half:], wv1[half:]), b1, w2)


def _heads_on_lanes(t4):
    return jnp.concatenate([t4[r] for r in range(NSA_R)], axis=1)


def _nsa_kernel(q_ref, ksa_ref, vsat_ref, kvw_ref, vwat_ref, kvc_ref, kvct_ref, tw_ref, bc_ref, misc_ref,
                o_ref, qa_ref, m_ref, acc_ref, s_ref, ps_ref):
    g = pl.program_id(1)
    i = pl.program_id(2)
    t0 = i * QB
    jd1 = (t0 + QB - 1) // SEL_TILE
    n_far = jnp.maximum(t0 - (_THR[NUM_BUCKETS - 1] - 1), 0) // SEL_TILE

    def score_tile(j, ntiles, bias_off):
        n = ntiles * SEL_TILE
        k0 = pl.multiple_of(j * SEL_TILE, SEL_TILE)
        sc = _dot(ksa_ref[0, 0, pl.ds(k0, n), :], qa_ref[j // (SUPER // SEL_TILE)])
        if bias_off is not None:
            off = pl.multiple_of(bias_off, LANES)
            sc = sc + _heads_on_lanes(tw_ref[:, pl.ds(off, n), :])
        s_ref[pl.ds(k0, n), :] = sc.astype(s_ref.dtype)
        return jnp.max(sc, axis=0, keepdims=True)

    def pv_tile(j, ntiles):
        n = ntiles * SEL_TILE
        k0 = pl.multiple_of(j * SEL_TILE, SEL_TILE)
        pt = jnp.exp(s_ref[pl.ds(k0, n), :] - m_ref[...].astype(s_ref.dtype))
        return _dot(vsat_ref[0, 0, :, pl.ds(k0, n)], pt)

    def sweep(total, block):
        n_grp = total // SWEEP_GROUP

        def body(jj, carry):
            block(SWEEP_GROUP * jj, SWEEP_GROUP)
            return carry
        lax.fori_loop(0, n_grp, body, 0)
        size = SWEEP_GROUP // 2
        while size >= 1:
            first = (total // (2 * size)) * (2 * size)

            @pl.when((total // size) % 2 == 1)
            def _(first=first, size=size):
                block(first, size)
            size //= 2

    def score_block(j, ntiles):
        mm = m_ref[...]
        for h in range(0, ntiles, 2):
            mm = jnp.maximum(mm, score_tile(j + h, min(2, ntiles - h), None))
        m_ref[...] = mm

    def pv_block(j, ntiles):
        part = None
        for h in range(0, ntiles, 2):
            pv = pv_tile(j + h, min(2, ntiles - h))
            part = pv if part is None else part + pv
        acc_ref[...] += part

    def front():
        qt = q_ref[0].astype(F32).T
        q_t = jnp.concatenate([qt[r * DH:(r + 1) * DH] for r in range(NSA_R)], axis=1)
        qpad_t = jnp.concatenate([q_t, jnp.zeros((DH, ROWS), F32)], axis=0).astype(BF16)

        bc = _heads_on_lanes(bc_ref[0])
        s = _dot(kvc_ref[0, 0], qpad_t) + bc
        m = jnp.max(s, axis=0, keepdims=True)
        p = jnp.exp(s - m)
        l = jnp.sum(p, axis=0, keepdims=True)
        pn = p * jnp.where(m > 0.5 * NEG, 1.0 / l, 0.0)
        o_c = _dot(kvct_ref[0, 0], pn.astype(BF16))[DH:2 * DH]

        wlen = WINDOW + QB
        start = pl.multiple_of(jnp.maximum(t0 - WINDOW, 0), LANES)
        offw = pl.multiple_of(start - t0 + WINDOW, LANES)
        sw = _dot(kvw_ref[0, 0, pl.ds(start, wlen), :], qpad_t) + _heads_on_lanes(tw_ref[:, pl.ds(offw, wlen), :])
        mw = jnp.max(sw, axis=0, keepdims=True)
        pw = jnp.exp((sw - mw).astype(BF16))
        acc_w = _dot(vwat_ref[0, 0, :, pl.ds(start, wlen)], pw)
        o_w = acc_w[0:DH] / acc_w[DH:DH + 1]

        sg = _sigmoid(misc_ref[0]).T

        def gate_row(c):
            rows = []
            for r in range(NSA_R):
                c0 = MISC_GATES + 3 * r + c
                c1 = c0 + 3 * NSA_R
                rows.append(jnp.where(g == 0, sg[c0:c0 + 1], sg[c1:c1 + 1]))
            return jnp.concatenate(rows, axis=1)
        partial = gate_row(0) * o_c + gate_row(2) * o_w
        gate_sel = gate_row(1)

        nc = pn.shape[0]
        nsel = nc * CMP_STRIDE // SEL_BLOCK
        per_blk = SEL_BLOCK // CMP_STRIDE
        psum = pn[:, 0:QB] + pn[:, QB:2 * QB] + pn[:, 2 * QB:3 * QB] + pn[:, 3 * QB:4 * QB]
        for c in range(QB // LANES):
            ps_ref[c, 0:8] = jnp.zeros((8, LANES), F32)
            ps_ref[c, 8:8 + nc] = psum[:, c * LANES:(c + 1) * LANES]
        tok = lambda k: jnp.concatenate(
            [ps_ref.at[c][pl.ds(8 + k, nsel, stride=per_blk), :] for c in range(QB // LANES)], axis=1)
        imp = 0.5 * (tok(-1) + tok(3)) + tok(0) + tok(1) + tok(2)
        nidx = lax.broadcasted_iota(jnp.int32, (nsel, QB), 0)
        qidx = lax.broadcasted_iota(jnp.int32, (nsel, QB), 1)
        qblk = (t0 + qidx) // SEL_BLOCK
        causal = nidx <= qblk
        forced = causal & ((nidx == 0) | (nidx >= qblk - (N_LOCAL - 1)))
        n_forced = N_LOCAL + 1
        w = jnp.where(causal & jnp.logical_not(forced), imp, -3e38)
        nf = nidx.astype(F32)
        sel = jnp.where(forced | (causal & (qblk < TOP_N)), 1.0, 0.0)
        for _ in range(min(TOP_N - n_forced, nsel)):
            mx = jnp.max(w, axis=0, keepdims=True)
            first = jnp.min(jnp.where(w == mx, nf, float(nsel)), axis=0, keepdims=True)
            pick = nf == first
            sel = jnp.where(pick, 1.0, sel)
            w = jnp.where(pick, -3e38, w)
        sel_bias = jnp.where((sel > 0.5) & causal, 0.0, NEG)

        for u in range(qa_ref.shape[0]):
            sb = sel_bias[u * 64:(u + 1) * 64]
            if sb.shape[0] < 64:
                sb = jnp.concatenate([sb, jnp.full((64 - sb.shape[0], QB), NEG, F32)], axis=0)
            sb4 = jnp.concatenate([sb] * NSA_R, axis=1)
            qa_ref[u] = jnp.concatenate([q_t, sb4], axis=0).astype(BF16)
        return partial, gate_sel

    partial, gate_sel = front()

    m_ref[...] = jnp.full(m_ref.shape, NEG, F32)
    sweep(n_far, score_block)
    m_lo = score_tile(n_far, 1, WINDOW - (t0 - n_far * SEL_TILE))
    m_hi = score_tile(jd1, 1, WINDOW - (t0 - jd1 * SEL_TILE))
    m_ref[...] = jnp.maximum(m_ref[...], jnp.maximum(m_lo, m_hi))

    acc_ref[...] = jnp.zeros(acc_ref.shape, F32)
    sweep(jd1 + 1, pv_block)
    acc = acc_ref[...]
    o_s = acc[0:DH] / acc[DH:DH + 1]
    out_t = partial + gate_sel * o_s
    out_t = jnp.concatenate([out_t[:, r * QB:(r + 1) * QB] for r in range(NSA_R)], axis=0)
    o_ref[0] = out_t.T.astype(o_ref.dtype)


def _nsa(qn, ksa, vsat, kvw, vwat, kvc, kvct, bias_w, bias_c, misc):
    B, S, _ = qn.shape
    nb = S // QB
    nc = kvc.shape[2]
    n_super = -(-S // SUPER)
    seq = lambda a: pl.BlockSpec((1, 1) + a.shape[2:], lambda b, g, i: (b, g, 0, 0))
    return pl.pallas_call(
        _nsa_kernel,
        grid=(B, NSA_G, nb),
        in_specs=[
            pl.BlockSpec((1, QB, NSA_R * DH), lambda b, g, i: (b, i, g)),
            seq(ksa), seq(vsat), seq(kvw), seq(vwat), seq(kvc), seq(kvct),
            pl.BlockSpec((NSA_R, TW_WIDTH, QB), lambda b, g, i: (g, 0, 0)),
            pl.BlockSpec((1, NSA_R, nc, QB), lambda b, g, i: (i, g, 0, 0)),
            pl.BlockSpec((1, QB, LANES), lambda b, g, i: (b, i, 0)),
        ],
        out_specs=pl.BlockSpec((1, QB, NSA_R * DH), lambda b, g, i: (b, i, g)),
        out_shape=jax.ShapeDtypeStruct((B, S, NSA_HEADS * DH), BF16),
        scratch_shapes=[pltpu.VMEM((n_super, LANES, ROWS), BF16),
                        pltpu.VMEM((1, ROWS), F32),
                        pltpu.VMEM((VROWS, ROWS), F32),
                        pltpu.VMEM((S, ROWS), BF16),
                        pltpu.VMEM((QB // LANES, nc + 8, LANES), F32)],
        compiler_params=pltpu.CompilerParams(dimension_semantics=("parallel", "parallel", "arbitrary"),
                                             vmem_limit_bytes=VMEM_LIMIT),
        name="nsa_attention",
    )(qn, ksa, vsat, kvw, vwat, kvc, kvct, bias_w, bias_c, misc)


def _gla_kernel(qk_ref, v_ref, z_ref, misc_ref, wa_ref, ba_ref, nw_ref, tri_ref, o_ref, st_ref):
    t = pl.program_id(2)
    T = qk_ref.shape[1]
    C = GLA_CHUNK

    @pl.when(t == 0)
    def _():
        st_ref[...] = jnp.zeros(st_ref.shape, F32)

    lane = lax.broadcasted_iota(jnp.int32, (1, LANES), 1)
    sign = jnp.where(lane < GLA_DK, 1.0 / GLA_TAU, -1.0 / GLA_TAU)
    zz = _dot(misc_ref[0].astype(BF16), wa_ref[0]) + ba_ref[0]
    log_sig = jnp.minimum(zz, 0.0) - jnp.log(1.0 + jnp.exp(-jnp.abs(zz)))
    la2 = log_sig * sign
    rr = lax.broadcasted_iota(jnp.int32, (C, C), 0)
    cc = lax.broadcasted_iota(jnp.int32, (C, C), 1)
    tril = rr >= cc
    scale = GLA_DK ** -0.5

    chunks = [slice(n * C, (n + 1) * C) for n in range(T // C)]
    hi = la2.astype(BF16)
    rem = la2 - hi.astype(F32)
    mid = rem.astype(BF16)
    lo = (rem - mid.astype(F32)).astype(BF16)
    tri = tri_ref[...]
    b2 = jnp.concatenate([_dot(tri, hi[sl]) + _dot(tri, mid[sl]) + _dot(tri, lo[sl]) for sl in chunks], axis=0)
    e = jnp.exp(b2)
    qke = qk_ref[0] * e
    q_e = (qke[:, 0:GLA_DK] * scale).astype(BF16)
    k_e = qke[:, GLA_DK:2 * GLA_DK].astype(BF16)
    v = v_ref[0].astype(BF16)
    attn = [jnp.where(tril, _dot_nt(q_e[sl], k_e[sl]), 0.0).astype(BF16) for sl in chunks]
    o_intra = [_dot(a, v[sl]) for a, sl in zip(attn, chunks)]
    kv = [_dot_tn(v[sl], k_e[sl]) for sl in chunks]
    st = st_ref[...]
    states = []
    for n, sl in enumerate(chunks):
        states.append(st.astype(BF16))
        st = (st + kv[n]) * e[sl.stop - 1:sl.stop, 0:GLA_DK]
    st_ref[...] = st
    o = jnp.concatenate([oi + _dot_nt(q_e[sl], s_in) for oi, sl, s_in in zip(o_intra, chunks, states)], axis=0)
    y = o * lax.rsqrt(jnp.mean(o * o, axis=-1, keepdims=True) + EPS) * nw_ref[...]
    o_ref[0] = (y * _silu(z_ref[0])).astype(o_ref.dtype)


def _gla(qkg, vg, zg, misc, wa_big, ba2, norm_w, tri, T):
    B, S, _ = qkg.shape
    hb = pl.BlockSpec((1, T, LANES), lambda b, h, t: (b, t, h))
    return pl.pallas_call(
        _gla_kernel,
        grid=(B, GLA_HEADS, S // T),
        in_specs=[hb, hb, hb,
                  pl.BlockSpec((1, T, LANES), lambda b, h, t: (b, t, 0)),
                  pl.BlockSpec((1, LANES, LANES), lambda b, h, t: (h, 0, 0)),
                  pl.BlockSpec((1, 1, LANES), lambda b, h, t: (h, 0, 0)),
                  pl.BlockSpec(norm_w.shape, lambda b, h, t: (0, 0)),
                  pl.BlockSpec(tri.shape, lambda b, h, t: (0, 0))],
        out_specs=hb,
        out_shape=jax.ShapeDtypeStruct((B, S, GLA_HEADS * GLA_DV), BF16),
        scratch_shapes=[pltpu.VMEM((GLA_DV, GLA_DK), F32)],
        compiler_params=pltpu.CompilerParams(dimension_semantics=("parallel", "parallel", "arbitrary"),
                                             vmem_limit_bytes=VMEM_LIMIT),
        name="gla_chunked",
    )(qkg, vg, zg, misc, wa_big, ba2, norm_w, tri)


def _out_kernel(hn_ref, on_ref, zs_ref, og_ref, p_ref, wo_ref, wpg_ref, bpg_ref, wpe_ref, lg_ref, lb_ref, o_ref):
    half = on_ref.shape[2]
    mix_n = on_ref[0] * zs_ref[0]
    y = _dot(mix_n, wo_ref[0:half, :]) + _dot(og_ref[0], wo_ref[half:2 * half, :])
    r = hn_ref[0] + y
    gate = _sigmoid(_dot(r.astype(BF16), wpg_ref[...]) + bpg_ref[...])
    r = r + gate * _dot(p_ref[0].astype(BF16), wpe_ref[...])
    o_ref[0] = _layer_norm(r, lg_ref[...], lb_ref[...])


def _out_projection(hn, o_nsa, zs, o_gla, p, w_out, w_pg, b_pg, w_pe, ln_g, ln_b, tm):
    B, S, D = hn.shape
    tok = lambda w: pl.BlockSpec((1, tm, w), lambda b, t: (b, t, 0))
    full2 = lambda a: pl.BlockSpec(a.shape, lambda b, t: (0, 0))
    return pl.pallas_call(
        _out_kernel,
        grid=(B, S // tm),
        in_specs=[tok(D), tok(512), tok(512), tok(512), tok(PLE_DIM),
                  full2(w_out), full2(w_pg), full2(b_pg), full2(w_pe), full2(ln_g), full2(ln_b)],
        out_specs=tok(D),
        out_shape=jax.ShapeDtypeStruct((B, S, D), F32),
        compiler_params=pltpu.CompilerParams(dimension_semantics=("parallel", "parallel"),
                                             vmem_limit_bytes=VMEM_LIMIT),
        name="out_proj_deepnorm",
    )(hn, o_nsa, zs, o_gla, p, w_out, w_pg, b_pg, w_pe, ln_g, ln_b)


def _permute_w_in(w):
    widths = (512, 128, 128, 128, 128, 128, 128, 24, 512, 256, 256, 512, 16, 512)
    offs = np.concatenate([[0], np.cumsum(widths)])
    (q_n, kc, vc, ks, vs, kw, vw, gates, z_n, q_g, k_g, v_g, a_low, z_g) = [
        w[:, int(offs[k]):int(offs[k + 1])] for k in range(len(widths))]
    pair = lambda a, b: [jnp.concatenate([a[:, g * DH:(g + 1) * DH], b[:, g * DH:(g + 1) * DH]], axis=1)
                         for g in range(NSA_G)]
    kvc = pair(kc, vc)
    kvw = pair(kw, vw)
    qk = jnp.concatenate([jnp.concatenate([q_g[:, h * GLA_DK:(h + 1) * GLA_DK],
                                           k_g[:, h * GLA_DK:(h + 1) * GLA_DK]], axis=1)
                          for h in range(GLA_HEADS)], axis=1)
    misc = jnp.concatenate([gates, a_low,
                            jnp.zeros((w.shape[0], LANES - gates.shape[1] - a_low.shape[1]), w.dtype)], axis=1)
    cols = [q_n * (DH ** -0.5), kvc[0], kvc[1], ks, vs, kvw[0], kvw[1], z_n, qk, v_g, z_g, misc]
    return jnp.concatenate(cols, axis=1).astype(BF16)


def kernel(x, p, ln0_g, ln0_b, rel_bias, w_in, w_a2, b_a, gla_norm_w, pos_cmp, w_ck1, b_ck1, w_ck2,
           w_cv1, b_cv1, w_cv2, w_out, w_pe, w_pg, b_pg, ln_g, ln_b):
    B, S, D = x.shape
    assert D == D_MODEL and S % 512 == 0 and S >= WINDOW + QB and w_in.shape[0] == DEPTH == 1
    nb = S // QB
    nch = S // CMP_STRIDE
    row = lambda a: a.reshape(1, -1)

    bias_c, bias_w = _bias_tables(rel_bias, nb, nch)

    w_perm = _permute_w_in(w_in[0])
    (hn, qn, xkv, ksa, vsat, kvw, vwat, zs, qkg, vg, zg, misc) = _in_projection(
        x, row(ln0_g), row(ln0_b), w_perm, TOKEN_TILE)

    kvc, kvct = _compress(xkv, *_compress_params(pos_cmp[0], w_ck1[0], b_ck1[0], w_ck2[0],
                                                 w_cv1[0], b_cv1[0], w_cv2[0]))

    o_nsa = _nsa(qn, ksa, vsat, kvw, vwat, kvc, kvct, bias_w, bias_c, misc)

    wa = w_a2[0]
    wa_big = jnp.zeros((GLA_HEADS, LANES, LANES), F32)
    for h in range(GLA_HEADS):
        wh = wa[:, h * GLA_DK:(h + 1) * GLA_DK]
        wa_big = wa_big.at[h, MISC_ALOW:MISC_ALOW + GLA_RANK, :].set(jnp.concatenate([wh, wh], axis=1))
    ba = b_a[0].reshape(GLA_HEADS, 1, GLA_DK)
    ba2 = jnp.concatenate([ba, ba], axis=2)
    tri = jnp.asarray(np.tril(np.ones((GLA_CHUNK, GLA_CHUNK), np.float32))).astype(BF16)
    gla_tile = math.gcd(S, GLA_TILE)
    o_gla = _gla(qkg, vg, zg, misc, wa_big.astype(BF16), ba2, row(gla_norm_w[0]), tri, gla_tile)

    return _out_projection(hn, o_nsa, zs, o_gla, p[0], w_out[0].astype(BF16), w_pg[0].astype(BF16),
                           row(b_pg[0]), w_pe[0].astype(BF16), row(ln_g[0]), row(ln_b[0]), TOKEN_TILE)
```

```python
import math

import numpy as np
import jax
import jax.numpy as jnp
from jax import lax
from jax.experimental import pallas as pl
from jax.experimental.pallas import tpu as pltpu

F32 = jnp.float32
BF16 = jnp.bfloat16

D_MODEL = 1024
PLE_DIM = 256
NSA_HEADS = 8
NSA_G = 2
NSA_R = NSA_HEADS // NSA_G
DH = 64
CMP_LEN = 32
CMP_STRIDE = 16
CMP_HIDDEN = 256
SEL_BLOCK = 64
TOP_N = 16
N_LOCAL = 2
WINDOW = 512
QB = 256
GLA_HEADS = 4
GLA_DK = 64
GLA_DV = 128
GLA_RANK = 16
GLA_TAU = 16.0
GLA_CHUNK = 64
NUM_BUCKETS = 32
MAX_DISTANCE = 128
DEPTH = 1
ALPHA = (2.0 * DEPTH) ** 0.25
EPS = 1e-5
NEG = -1e30
POS = 1e30

LANES = 128
VMEM_LIMIT = 56 * 1024 * 1024

TOKEN_TILE = 512
GLA_TILE = 4096
SEL_TILE = 256
SWEEP_GROUP = 16
VROWS = 80
SUPER = 64 * SEL_BLOCK
TW_WIDTH = WINDOW + QB + WINDOW
ROWS = NSA_R * QB

C_QN, C_KVC, C_KSVS, C_KVW, C_ZN, C_QKG, C_VG, C_ZG, C_MISC, C_END = (
    0, 512, 768, 1024, 1280, 1792, 2304, 2816, 3328, 3456)
MISC_GATES = 0
MISC_ALOW = 24


def _bucket_thresholds():
    d = np.arange(0, 4 * MAX_DISTANCE)
    max_exact = NUM_BUCKETS // 2
    nf = np.maximum(d, 1).astype(np.float32)
    large = max_exact + (np.log(nf / np.float32(max_exact)) / np.float32(math.log(MAX_DISTANCE / max_exact))
                         * np.float32(NUM_BUCKETS - max_exact)).astype(np.int32)
    large = np.minimum(large, NUM_BUCKETS - 1)
    bucket = np.where(d < max_exact, d, large)
    assert np.all(np.diff(bucket) >= 0) and bucket[-1] == NUM_BUCKETS - 1
    return [int(np.argmax(bucket >= k)) for k in range(NUM_BUCKETS)]


_THR = _bucket_thresholds()
CMP_BAND = 32
assert (QB - 1 + _THR[NUM_BUCKETS - 1]) // CMP_STRIDE + 1 + 7 <= CMP_BAND
assert SEL_TILE % QB == 0 and _THR[NUM_BUCKETS - 1] <= QB


def _dot(a, b, **kw):
    return jnp.dot(a, b, preferred_element_type=F32, **kw)


def _dot_nt(a, b, **kw):
    return lax.dot_general(a, b, (((1,), (1,)), ((), ())), preferred_element_type=F32, **kw)


def _dot_tn(a, b, **kw):
    return lax.dot_general(a, b, (((0,), (0,)), ((), ())), preferred_element_type=F32, **kw)


def _layer_norm(x, g, b):
    mu = jnp.mean(x, axis=-1, keepdims=True)
    xc = x - mu
    var = jnp.mean(xc * xc, axis=-1, keepdims=True)
    return xc * lax.rsqrt(var + EPS) * g + b


def _sigmoid(x):
    return 1.0 / (1.0 + jnp.exp(-x))


def _silu(x):
    return x * _sigmoid(x)


def _bias_from_dist(rb_ref, dist, valid, put):
    masks = [dist >= _THR[k] for k in range(1, NUM_BUCKETS)]
    for h in range(NSA_HEADS):
        val = jnp.full(dist.shape, rb_ref[0, h], F32)
        for k in range(1, NUM_BUCKETS):
            val = jnp.where(masks[k - 1], rb_ref[k, h], val)
        val = val - rb_ref[NUM_BUCKETS - 1, h]
        put(h, jnp.where(valid, val, NEG))


def _bias_cmp_kernel(rb_ref, out_ref):
    i = pl.program_id(0)
    nc = out_ref.shape[2]
    big = CMP_STRIDE * QB
    j_zero_max = (QB * i - (CMP_LEN - 1) - _THR[NUM_BUCKETS - 1] + CMP_STRIDE * big) // CMP_STRIDE - big
    start = jnp.clip((j_zero_max + 1 + 8 * big) // 8 * 8 - 8 * big, 0, nc - CMP_BAND)
    start = pl.multiple_of(start, 8)
    j_all = lax.broadcasted_iota(jnp.int32, (nc, QB), 0)
    fill = jnp.where(j_all <= j_zero_max, 0.0, NEG)
    for h in range(NSA_HEADS):
        out_ref[0, h] = fill
    j = start + lax.broadcasted_iota(jnp.int32, (CMP_BAND, QB), 0)
    a = lax.broadcasted_iota(jnp.int32, (CMP_BAND, QB), 1)
    dist = a + QB * i - CMP_STRIDE * j - (CMP_LEN - 1)

    def put(h, v):
        out_ref[0, h, pl.ds(start, CMP_BAND), :] = v
    _bias_from_dist(rb_ref, dist, dist >= 0, put)


def _bias_win_kernel(rb_ref, out_ref):
    shape = out_ref.shape[1:]
    y = lax.broadcasted_iota(jnp.int32, shape, 0)
    a = lax.broadcasted_iota(jnp.int32, shape, 1)
    dist = a + WINDOW - y

    def put(h, v):
        out_ref[h] = v
    _bias_from_dist(rb_ref, dist, (dist >= 0) & (dist < WINDOW), put)


def _bias_tables(rel_bias, nb, nc):
    smem = pl.BlockSpec(memory_space=pltpu.SMEM)
    bias_c = pl.pallas_call(
        _bias_cmp_kernel,
        grid=(nb,),
        in_specs=[smem],
        out_specs=pl.BlockSpec((1, NSA_HEADS, nc, QB), lambda i: (i, 0, 0, 0)),
        out_shape=jax.ShapeDtypeStruct((nb, NSA_HEADS, nc, QB), F32),
        compiler_params=pltpu.CompilerParams(dimension_semantics=("parallel",)),
        name="bias_cmp",
    )(rel_bias)
    bias_w = pl.pallas_call(
        _bias_win_kernel,
        in_specs=[smem],
        out_shape=jax.ShapeDtypeStruct((NSA_HEADS, TW_WIDTH, QB), F32),
        name="bias_win",
    )(rel_bias)
    return bias_c, bias_w


def _inproj_kernel(x_ref, g_ref, b_ref, w_ref, hn_ref, qn_ref, xkv_ref, ksa_ref, vsat_ref, kvw_ref, vwat_ref,
                   zs_ref, qkg_ref, vg_ref, zg_ref, misc_ref, slab_ref):
    t = pl.program_id(1)
    tm = x_ref.shape[1]
    hf = _layer_norm(x_ref[0], g_ref[...], b_ref[...])
    hn_ref[0] = ALPHA * hf
    h = hf.astype(BF16)

    def mm(c0, c1):
        return _dot(h, w_ref[:, c0:c1])

    qn_ref[0] = mm(C_QN, C_KVC).astype(BF16)

    kvc = mm(C_KVC, C_KSVS)
    for g in range(NSA_G):
        slab_ref[...] = kvc[:, g * LANES:(g + 1) * LANES]
        for tok in range(CMP_STRIDE):
            rows = slab_ref[pl.ds(tok, tm // CMP_STRIDE, stride=CMP_STRIDE), :]
            xkv_ref[0, g, :, tok * LANES:(tok + 1) * LANES] = rows.astype(BF16)

    lane = lax.broadcasted_iota(jnp.int32, (tm, LANES), 1)
    row = lax.broadcasted_iota(jnp.int32, (tm, LANES), 0)
    low = lane < DH
    blk = ((t * tm + row) // SEL_BLOCK) % (SUPER // SEL_BLOCK)
    onehot = jnp.where(lane - DH == blk, 1.0, 0.0)
    ones_col = jnp.where(lane == DH, 1.0, 0.0)

    def values_t(v_low):
        return jnp.where(low, v_low, ones_col).T[0:VROWS].astype(BF16)

    ksvs = mm(C_KSVS, C_KVW)
    ks, vs = ksvs[:, 0:LANES], ksvs[:, LANES:2 * LANES]
    ksa_ref[0, 0] = jnp.where(low, ks, onehot).astype(BF16)
    ksa_ref[0, 1] = jnp.where(low, pltpu.roll(ks, DH, 1), onehot).astype(BF16)
    vsat_ref[0, 0] = values_t(vs)
    vsat_ref[0, 1] = values_t(pltpu.roll(vs, DH, 1))
    kvw = mm(C_KVW, C_ZN)
    for g in range(NSA_G):
        kvw_g = kvw[:, g * LANES:(g + 1) * LANES]
        kvw_ref[0, g] = kvw_g.astype(BF16)
        vwat_ref[0, g] = values_t(pltpu.roll(kvw_g, DH, 1))
    zs_ref[0] = _silu(mm(C_ZN, C_QKG)).astype(BF16)
    qkg_ref[0] = mm(C_QKG, C_VG)
    vg_ref[0] = mm(C_VG, C_ZG)
    zg_ref[0] = mm(C_ZG, C_MISC)
    misc_ref[0] = mm(C_MISC, C_END)


def _in_projection(x, ln0_g, ln0_b, w_perm, tm):
    B, S, D = x.shape
    tok = lambda w: pl.BlockSpec((1, tm, w), lambda b, t: (b, t, 0))
    grp = pl.BlockSpec((1, NSA_G, tm, LANES), lambda b, t: (b, 0, t, 0))
    grp_v = pl.BlockSpec((1, NSA_G, VROWS, tm), lambda b, t: (b, 0, 0, t))
    grp_x = pl.BlockSpec((1, NSA_G, tm // CMP_STRIDE, CMP_STRIDE * LANES), lambda b, t: (b, 0, t, 0))
    full2 = lambda a: pl.BlockSpec(a.shape, lambda b, t: (0, 0))
    sds = jax.ShapeDtypeStruct
    return pl.pallas_call(
        _inproj_kernel,
        grid=(B, S // tm),
        in_specs=[tok(D), full2(ln0_g), full2(ln0_b), full2(w_perm)],
        out_specs=[tok(D), tok(512), grp_x, grp, grp_v, grp, grp_v, tok(512), tok(512), tok(512), tok(512),
                   tok(LANES)],
        out_shape=[sds((B, S, D), F32), sds((B, S, 512), BF16),
                   sds((B, NSA_G, S // CMP_STRIDE, CMP_STRIDE * LANES), BF16),
                   sds((B, NSA_G, S, LANES), BF16), sds((B, NSA_G, VROWS, S), BF16),
                   sds((B, NSA_G, S, LANES), BF16), sds((B, NSA_G, VROWS, S), BF16),
                   sds((B, S, 512), BF16), sds((B, S, 512), F32), sds((B, S, 512), F32), sds((B, S, 512), F32),
                   sds((B, S, LANES), F32)],
        scratch_shapes=[pltpu.VMEM((tm, LANES), F32)],
        compiler_params=pltpu.CompilerParams(dimension_semantics=("parallel", "parallel"),
                                             vmem_limit_bytes=VMEM_LIMIT),
        name="ln0_inproj",
    )(x, ln0_g, ln0_b, w_perm)


def _gelu_tanh(x):
    c = math.sqrt(2.0 / math.pi)
    return x * (0.5 * (1.0 + jnp.tanh(c * (x + 0.044715 * (x * x * x)))))


def _compress_kernel(x_ref, ptop_ref, pbot_ref, wtop_ref, wbot_ref, b1_ref, w2_ref, out_ref, outt_ref):
    nch = x_ref.shape[2]
    x = x_ref[0, 0]
    top = _dot(x, wtop_ref[...])
    bot = _dot(x, wbot_ref[...])
    cpos = (_dot(ptop_ref[...], wtop_ref[...]) + _dot(pbot_ref[...], wbot_ref[...]))[0:1, :]
    pre = top + pltpu.roll(bot, nch - 1, 0) + cpos + b1_ref[...]
    out = _dot(_gelu_tanh(pre).astype(BF16), w2_ref[...])
    out_ref[0, 0] = out.astype(BF16)
    outt_ref[0, 0] = out.T.astype(BF16)


def _compress(xkv, ptop, pbot, wtop, wbot, b1, w2):
    B, G, nch, width = xkv.shape
    full2 = lambda a: pl.BlockSpec(a.shape, lambda b, g: (0, 0))
    return pl.pallas_call(
        _compress_kernel,
        grid=(B, G),
        in_specs=[pl.BlockSpec((1, 1, nch, width), lambda b, g: (b, g, 0, 0)),
                  full2(ptop), full2(pbot), full2(wtop), full2(wbot), full2(b1), full2(w2)],
        out_specs=[pl.BlockSpec((1, 1, nch, LANES), lambda b, g: (b, g, 0, 0)),
                   pl.BlockSpec((1, 1, LANES, nch), lambda b, g: (b, g, 0, 0))],
        out_shape=[jax.ShapeDtypeStruct((B, G, nch, LANES), BF16),
                   jax.ShapeDtypeStruct((B, G, LANES, nch), BF16)],
        compiler_params=pltpu.CompilerParams(dimension_semantics=("parallel", "parallel"),
                                             vmem_limit_bytes=VMEM_LIMIT),
        name="kv_compress",
    )(xkv, ptop, pbot, wtop, wbot, b1, w2)


def _compress_params(pos, wk1, bk1, wk2, wv1, bv1, wv2):
    half = CMP_STRIDE * DH
    zero = jnp.zeros((CMP_STRIDE, DH, CMP_HIDDEN), F32)

    def rows(w_k, w_v):
        k_rows = jnp.concatenate([w_k.reshape(CMP_STRIDE, DH, CMP_HIDDEN), zero], axis=2)
        v_rows = jnp.concatenate([zero, w_v.reshape(CMP_STRIDE, DH, CMP_HIDDEN)], axis=2)
        return jnp.concatenate([k_rows, v_rows], axis=1).reshape(CMP_STRIDE * LANES, 2 * CMP_HIDDEN).astype(BF16)

    def pos_row(p):
        r = jnp.concatenate([p, p], axis=1).reshape(1, CMP_STRIDE * LANES)
        return jnp.broadcast_to(r, (8, CMP_STRIDE * LANES)).astype(BF16)

    zpad = jnp.zeros((CMP_HIDDEN, DH), F32)
    w2 = jnp.concatenate([jnp.concatenate([wk2, zpad], axis=1),
                          jnp.concatenate([zpad, wv2], axis=1)], axis=0).astype(BF16)
    b1 = jnp.concatenate([bk1, bv1]).reshape(1, -1)
    return (pos_row(pos[:CMP_STRIDE]), pos_row(pos[CMP_STRIDE:]), rows(wk1[:half], wv1[:half]),
            rows(wk1[half:], wv1[half:]), b1, w2)


def _heads_on_lanes(t4):
    return jnp.concatenate([t4[r] for r in range(NSA_R)], axis=1)


def _nsa_kernel(q_ref, ksa_ref, vsat_ref, kvw_ref, vwat_ref, kvc_ref, kvct_ref, tw_ref, bc_ref, misc_ref,
                o_ref, qa_ref, m_ref, acc_ref, s_ref, ps_ref):
    g = pl.program_id(1)
    i = pl.program_id(2)
    t0 = i * QB
    jd1 = (t0 + QB - 1) // SEL_TILE
    n_far = jnp.maximum(t0 - (_THR[NUM_BUCKETS - 1] - 1), 0) // SEL_TILE

    def score_tile(j, ntiles, bias_off):
        n = ntiles * SEL_TILE
        k0 = pl.multiple_of(j * SEL_TILE, SEL_TILE)
        sc = _dot(ksa_ref[0, 0, pl.ds(k0, n), :], qa_ref[j // (SUPER // SEL_TILE)])
        if bias_off is not None:
            off = pl.multiple_of(bias_off, LANES)
            sc = sc + _heads_on_lanes(tw_ref[:, pl.ds(off, n), :])
        s_ref[pl.ds(k0, n), :] = sc.astype(s_ref.dtype)
        return jnp.max(sc, axis=0, keepdims=True)

    def pv_tile(j, ntiles):
        n = ntiles * SEL_TILE
        k0 = pl.multiple_of(j * SEL_TILE, SEL_TILE)
        pt = jnp.exp(s_ref[pl.ds(k0, n), :] - m_ref[...].astype(s_ref.dtype))
        return _dot(vsat_ref[0, 0, :, pl.ds(k0, n)], pt)

    def sweep(total, block):
        n_grp = total // SWEEP_GROUP

        def body(jj, carry):
            block(SWEEP_GROUP * jj, SWEEP_GROUP)
            return carry
        lax.fori_loop(0, n_grp, body, 0)
        size = SWEEP_GROUP // 2
        while size >= 1:
            first = (total // (2 * size)) * (2 * size)

            @pl.when((total // size) % 2 == 1)
            def _(first=first, size=size):
                block(first, size)
            size //= 2

    def score_block(j, ntiles):
        mm = m_ref[...]
        for h in range(0, ntiles, 2):
            mm = jnp.maximum(mm, score_tile(j + h, min(2, ntiles - h), None))
        m_ref[...] = mm

    def pv_block(j, ntiles):
        part = None
        for h in range(0, ntiles, 2):
            pv = pv_tile(j + h, min(2, ntiles - h))
            part = pv if part is None else part + pv
        acc_ref[...] += part

    def front():
        qt = q_ref[0].astype(F32).T
        q_t = jnp.concatenate([qt[r * DH:(r + 1) * DH] for r in range(NSA_R)], axis=1)
        qpad_t = jnp.concatenate([q_t, jnp.zeros((DH, ROWS), F32)], axis=0).astype(BF16)

        bc = _heads_on_lanes(bc_ref[0])
        s = _dot(kvc_ref[0, 0], qpad_t) + bc
        m = jnp.max(s, axis=0, keepdims=True)
        p = jnp.exp(s - m)
        l = jnp.sum(p, axis=0, keepdims=True)
        pn = p * jnp.where(m > 0.5 * NEG, 1.0 / l, 0.0)
        o_c = _dot(kvct_ref[0, 0], pn.astype(BF16))[DH:2 * DH]

        wlen = WINDOW + QB
        start = pl.multiple_of(jnp.maximum(t0 - WINDOW, 0), LANES)
        offw = pl.multiple_of(start - t0 + WINDOW, LANES)
        sw = _dot(kvw_ref[0, 0, pl.ds(start, wlen), :], qpad_t) + _heads_on_lanes(tw_ref[:, pl.ds(offw, wlen), :])
        mw = jnp.max(sw, axis=0, keepdims=True)
        pw = jnp.exp((sw - mw).astype(BF16))
        acc_w = _dot(vwat_ref[0, 0, :, pl.ds(start, wlen)], pw)
        o_w = acc_w[0:DH] / acc_w[DH:DH + 1]

        sg = _sigmoid(misc_ref[0]).T

        def gate_row(c):
            rows = []
            for r in range(NSA_R):
                c0 = MISC_GATES + 3 * r + c
                c1 = c0 + 3 * NSA_R
                rows.append(jnp.where(g == 0, sg[c0:c0 + 1], sg[c1:c1 + 1]))
            return jnp.concatenate(rows, axis=1)
        partial = gate_row(0) * o_c + gate_row(2) * o_w
        gate_sel = gate_row(1)

        nc = pn.shape[0]
        nsel = nc * CMP_STRIDE // SEL_BLOCK
        per_blk = SEL_BLOCK // CMP_STRIDE
        psum = pn[:, 0:QB] + pn[:, QB:2 * QB] + pn[:, 2 * QB:3 * QB] + pn[:, 3 * QB:4 * QB]
        for c in range(QB // LANES):
            ps_ref[c, 0:8] = jnp.zeros((8, LANES), F32)
            ps_ref[c, 8:8 + nc] = psum[:, c * LANES:(c + 1) * LANES]
        tok = lambda k: jnp.concatenate(
            [ps_ref.at[c][pl.ds(8 + k, nsel, stride=per_blk), :] for c in range(QB // LANES)], axis=1)
        imp = 0.5 * (tok(-1) + tok(3)) + tok(0) + tok(1) + tok(2)
        nidx = lax.broadcasted_iota(jnp.int32, (nsel, QB), 0)
        qidx = lax.broadcasted_iota(jnp.int32, (nsel, QB), 1)
        qblk = (t0 + qidx) // SEL_BLOCK
        causal = nidx <= qblk
        forced = causal & ((nidx == 0) | (nidx >= qblk - (N_LOCAL - 1)))
        n_forced = N_LOCAL + 1
        w = jnp.where(causal & jnp.logical_not(forced), imp, -3e38)
        nf = nidx.astype(F32)
        sel = jnp.where(forced | (causal & (qblk < TOP_N)), 1.0, 0.0)
        for _ in range(min(TOP_N - n_forced, nsel)):
            mx = jnp.max(w, axis=0, keepdims=True)
            first = jnp.min(jnp.where(w == mx, nf, float(nsel)), axis=0, keepdims=True)
            pick = nf == first
            sel = jnp.where(pick, 1.0, sel)
            w = jnp.where(pick, -3e38, w)
        sel_bias = jnp.where((sel > 0.5) & causal, 0.0, NEG)

        for u in range(qa_ref.shape[0]):
            sb = sel_bias[u * 64:(u + 1) * 64]
            if sb.shape[0] < 64:
                sb = jnp.concatenate([sb, jnp.full((64 - sb.shape[0], QB), NEG, F32)], axis=0)
            sb4 = jnp.concatenate([sb] * NSA_R, axis=1)
            qa_ref[u] = jnp.concatenate([q_t, sb4], axis=0).astype(BF16)
        return partial, gate_sel

    partial, gate_sel = front()

    m_ref[...] = jnp.full(m_ref.shape, NEG, F32)
    sweep(n_far, score_block)
    m_lo = score_tile(n_far, 1, WINDOW - (t0 - n_far * SEL_TILE))
    m_hi = score_tile(jd1, 1, WINDOW - (t0 - jd1 * SEL_TILE))
    m_ref[...] = jnp.maximum(m_ref[...], jnp.maximum(m_lo, m_hi))

    acc_ref[...] = jnp.zeros(acc_ref.shape, F32)
    sweep(jd1 + 1, pv_block)
    acc = acc_ref[...]
    o_s = acc[0:DH] / acc[DH:DH + 1]
    out_t = partial + gate_sel * o_s
    out_t = jnp.concatenate([out_t[:, r * QB:(r + 1) * QB] for r in range(NSA_R)], axis=0)
    o_ref[0] = out_t.T.astype(o_ref.dtype)


def _nsa(qn, ksa, vsat, kvw, vwat, kvc, kvct, bias_w, bias_c, misc):
    B, S, _ = qn.shape
    nb = S // QB
    nc = kvc.shape[2]
    n_super = -(-S // SUPER)
    seq = lambda a: pl.BlockSpec((1, 1) + a.shape[2:], lambda b, g, i: (b, g, 0, 0))
    return pl.pallas_call(
        _nsa_kernel,
        grid=(B, NSA_G, nb),
        in_specs=[
            pl.BlockSpec((1, QB, NSA_R * DH), lambda b, g, i: (b, i, g)),
            seq(ksa), seq(vsat), seq(kvw), seq(vwat), seq(kvc), seq(kvct),
            pl.BlockSpec((NSA_R, TW_WIDTH, QB), lambda b, g, i: (g, 0, 0)),
            pl.BlockSpec((1, NSA_R, nc, QB), lambda b, g, i: (i, g, 0, 0)),
            pl.BlockSpec((1, QB, LANES), lambda b, g, i: (b, i, 0)),
        ],
        out_specs=pl.BlockSpec((1, QB, NSA_R * DH), lambda b, g, i: (b, i, g)),
        out_shape=jax.ShapeDtypeStruct((B, S, NSA_HEADS * DH), BF16),
        scratch_shapes=[pltpu.VMEM((n_super, LANES, ROWS), BF16),
                        pltpu.VMEM((1, ROWS), F32),
                        pltpu.VMEM((VROWS, ROWS), F32),
                        pltpu.VMEM((S, ROWS), BF16),
                        pltpu.VMEM((QB // LANES, nc + 8, LANES), F32)],
        compiler_params=pltpu.CompilerParams(dimension_semantics=("parallel", "parallel", "arbitrary"),
                                             vmem_limit_bytes=VMEM_LIMIT),
        name="nsa_attention",
    )(qn, ksa, vsat, kvw, vwat, kvc, kvct, bias_w, bias_c, misc)


def _gla_kernel(qk_ref, v_ref, z_ref, misc_ref, wa_ref, ba_ref, nw_ref, tri_ref, o_ref, st_ref):
    t = pl.program_id(2)
    T = qk_ref.shape[1]
    C = GLA_CHUNK

    @pl.when(t == 0)
    def _():
        st_ref[...] = jnp.zeros(st_ref.shape, F32)

    lane = lax.broadcasted_iota(jnp.int32, (1, LANES), 1)
    sign = jnp.where(lane < GLA_DK, 1.0 / GLA_TAU, -1.0 / GLA_TAU)
    zz = _dot(misc_ref[0].astype(BF16), wa_ref[0]) + ba_ref[0]
    log_sig = jnp.minimum(zz, 0.0) - jnp.log(1.0 + jnp.exp(-jnp.abs(zz)))
    la2 = log_sig * sign
    rr = lax.broadcasted_iota(jnp.int32, (C, C), 0)
    cc = lax.broadcasted_iota(jnp.int32, (C, C), 1)
    tril = rr >= cc
    scale = GLA_DK ** -0.5

    chunks = [slice(n * C, (n + 1) * C) for n in range(T // C)]
    hi = la2.astype(BF16)
    rem = la2 - hi.astype(F32)
    mid = rem.astype(BF16)
    lo = (rem - mid.astype(F32)).astype(BF16)
    tri = tri_ref[...]
    b2 = jnp.concatenate([_dot(tri, hi[sl]) + _dot(tri, mid[sl]) + _dot(tri, lo[sl]) for sl in chunks], axis=0)
    e = jnp.exp(b2)
    qke = qk_ref[0] * e
    q_e = (qke[:, 0:GLA_DK] * scale).astype(BF16)
    k_e = qke[:, GLA_DK:2 * GLA_DK].astype(BF16)
    v = v_ref[0].astype(BF16)
    attn = [jnp.where(tril, _dot_nt(q_e[sl], k_e[sl]), 0.0).astype(BF16) for sl in chunks]
    o_intra = [_dot(a, v[sl]) for a, sl in zip(attn, chunks)]
    kv = [_dot_tn(v[sl], k_e[sl]) for sl in chunks]
    st = st_ref[...]
    states = []
    for n, sl in enumerate(chunks):
        states.append(st.astype(BF16))
        st = (st + kv[n]) * e[sl.stop - 1:sl.stop, 0:GLA_DK]
    st_ref[...] = st
    o = jnp.concatenate([oi + _dot_nt(q_e[sl], s_in) for oi, sl, s_in in zip(o_intra, chunks, states)], axis=0)
    y = o * lax.rsqrt(jnp.mean(o * o, axis=-1, keepdims=True) + EPS) * nw_ref[...]
    o_ref[0] = (y * _silu(z_ref[0])).astype(o_ref.dtype)


def _gla(qkg, vg, zg, misc, wa_big, ba2, norm_w, tri, T):
    B, S, _ = qkg.shape
    hb = pl.BlockSpec((1, T, LANES), lambda b, h, t: (b, t, h))
    return pl.pallas_call(
        _gla_kernel,
        grid=(B, GLA_HEADS, S // T),
        in_specs=[hb, hb, hb,
                  pl.BlockSpec((1, T, LANES), lambda b, h, t: (b, t, 0)),
                  pl.BlockSpec((1, LANES, LANES), lambda b, h, t: (h, 0, 0)),
                  pl.BlockSpec((1, 1, LANES), lambda b, h, t: (h, 0, 0)),
                  pl.BlockSpec(norm_w.shape, lambda b, h, t: (0, 0)),
                  pl.BlockSpec(tri.shape, lambda b, h, t: (0, 0))],
        out_specs=hb,
        out_shape=jax.ShapeDtypeStruct((B, S, GLA_HEADS * GLA_DV), BF16),
        scratch_shapes=[pltpu.VMEM((GLA_DV, GLA_DK), F32)],
        compiler_params=pltpu.CompilerParams(dimension_semantics=("parallel", "parallel", "arbitrary"),
                                             vmem_limit_bytes=VMEM_LIMIT),
        name="gla_chunked",
    )(qkg, vg, zg, misc, wa_big, ba2, norm_w, tri)


def _out_kernel(hn_ref, on_ref, zs_ref, og_ref, p_ref, wo_ref, wpg_ref, bpg_ref, wpe_ref, lg_ref, lb_ref, o_ref):
    sub = 256
    for c in range(on_ref.shape[1] // sub):
        rows = slice(c * sub, (c + 1) * sub)
        mix_n = on_ref[0, rows, :] * zs_ref[0, rows, :]
        mix = jnp.concatenate([mix_n, og_ref[0, rows, :]], axis=1)
        r = hn_ref[0, rows, :] + _dot(mix, wo_ref[...])
        gate = _sigmoid(_dot(r.astype(BF16), wpg_ref[...]) + bpg_ref[...])
        r = r + gate * _dot(p_ref[0, rows, :].astype(BF16), wpe_ref[...])
        o_ref[0, rows, :] = _layer_norm(r, lg_ref[...], lb_ref[...])


def _out_projection(hn, o_nsa, zs, o_gla, p, w_out, w_pg, b_pg, w_pe, ln_g, ln_b, tm):
    B, S, D = hn.shape
    tok = lambda w: pl.BlockSpec((1, tm, w), lambda b, t: (b, t, 0))
    full2 = lambda a: pl.BlockSpec(a.shape, lambda b, t: (0, 0))
    return pl.pallas_call(
        _out_kernel,
        grid=(B, S // tm),
        in_specs=[tok(D), tok(512), tok(512), tok(512), tok(PLE_DIM),
                  full2(w_out), full2(w_pg), full2(b_pg), full2(w_pe), full2(ln_g), full2(ln_b)],
        out_specs=tok(D),
        out_shape=jax.ShapeDtypeStruct((B, S, D), F32),
        compiler_params=pltpu.CompilerParams(dimension_semantics=("parallel", "parallel"),
                                             vmem_limit_bytes=VMEM_LIMIT),
        name="out_proj_deepnorm",
    )(hn, o_nsa, zs, o_gla, p, w_out, w_pg, b_pg, w_pe, ln_g, ln_b)


def _permute_w_in(w):
    widths = (512, 128, 128, 128, 128, 128, 128, 24, 512, 256, 256, 512, 16, 512)
    offs = np.concatenate([[0], np.cumsum(widths)])
    (q_n, kc, vc, ks, vs, kw, vw, gates, z_n, q_g, k_g, v_g, a_low, z_g) = [
        w[:, int(offs[k]):int(offs[k + 1])] for k in range(len(widths))]
    pair = lambda a, b: [jnp.concatenate([a[:, g * DH:(g + 1) * DH], b[:, g * DH:(g + 1) * DH]], axis=1)
                         for g in range(NSA_G)]
    kvc = pair(kc, vc)
    kvw = pair(kw, vw)
    qk = jnp.concatenate([jnp.concatenate([q_g[:, h * GLA_DK:(h + 1) * GLA_DK],
                                           k_g[:, h * GLA_DK:(h + 1) * GLA_DK]], axis=1)
                          for h in range(GLA_HEADS)], axis=1)
    misc = jnp.concatenate([gates, a_low,
                            jnp.zeros((w.shape[0], LANES - gates.shape[1] - a_low.shape[1]), w.dtype)], axis=1)
    cols = [q_n * (DH ** -0.5), kvc[0], kvc[1], ks, vs, kvw[0], kvw[1], z_n, qk, v_g, z_g, misc]
    return jnp.concatenate(cols, axis=1).astype(BF16)


def kernel(x, p, ln0_g, ln0_b, rel_bias, w_in, w_a2, b_a, gla_norm_w, pos_cmp, w_ck1, b_ck1, w_ck2,
           w_cv1, b_cv1, w_cv2, w_out, w_pe, w_pg, b_pg, ln_g, ln_b):
    B, S, D = x.shape
    assert D == D_MODEL and S % 512 == 0 and S >= WINDOW + QB and w_in.shape[0] == DEPTH == 1
    nb = S // QB
    nch = S // CMP_STRIDE
    row = lambda a: a.reshape(1, -1)

    bias_c, bias_w = _bias_tables(rel_bias, nb, nch)

    w_perm = _permute_w_in(w_in[0])
    (hn, qn, xkv, ksa, vsat, kvw, vwat, zs, qkg, vg, zg, misc) = _in_projection(
        x, row(ln0_g), row(ln0_b), w_perm, TOKEN_TILE)

    kvc, kvct = _compress(xkv, *_compress_params(pos_cmp[0], w_ck1[0], b_ck1[0], w_ck2[0],
                                                 w_cv1[0], b_cv1[0], w_cv2[0]))

    o_nsa = _nsa(qn, ksa, vsat, kvw, vwat, kvc, kvct, bias_w, bias_c, misc)

    wa = w_a2[0]
    wa_big = jnp.zeros((GLA_HEADS, LANES, LANES), F32)
    for h in range(GLA_HEADS):
        wh = wa[:, h * GLA_DK:(h + 1) * GLA_DK]
        wa_big = wa_big.at[h, MISC_ALOW:MISC_ALOW + GLA_RANK, :].set(jnp.concatenate([wh, wh], axis=1))
    ba = b_a[0].reshape(GLA_HEADS, 1, GLA_DK)
    ba2 = jnp.concatenate([ba, ba], axis=2)
    tri = jnp.asarray(np.tril(np.ones((GLA_CHUNK, GLA_CHUNK), np.float32))).astype(BF16)
    gla_tile = math.gcd(S, GLA_TILE)
    o_gla = _gla(qkg, vg, zg, misc, wa_big.astype(BF16), ba2, row(gla_norm_w[0]), tri, gla_tile)

    return _out_projection(hn, o_nsa, zs, o_gla, p[0], w_out[0].astype(BF16), w_pg[0].astype(BF16),
                           row(b_pg[0]), w_pe[0].astype(BF16), row(ln_g[0]), row(ln_b[0]), TOKEN_TILE)
```

```python
import math

import numpy as np
import jax
import jax.numpy as jnp
from jax import lax
from jax.experimental import pallas as pl
from jax.experimental.pallas import tpu as pltpu

F32 = jnp.float32
BF16 = jnp.bfloat16

D_MODEL = 1024
PLE_DIM = 256
NSA_HEADS = 8
NSA_G = 2
NSA_R = NSA_HEADS // NSA_G
DH = 64
CMP_LEN = 32
CMP_STRIDE = 16
CMP_HIDDEN = 256
SEL_BLOCK = 64
TOP_N = 16
N_LOCAL = 2
WINDOW = 512
QB = 256
GLA_HEADS = 4
GLA_DK = 64
GLA_DV = 128
GLA_RANK = 16
GLA_TAU = 16.0
GLA_CHUNK = 64
NUM_BUCKETS = 32
MAX_DISTANCE = 128
DEPTH = 1
ALPHA = (2.0 * DEPTH) ** 0.25
EPS = 1e-5
NEG = -1e30
POS = 1e30

LANES = 128
VMEM_LIMIT = 56 * 1024 * 1024

TOKEN_TILE = 512
GLA_TILE = 4096
SEL_TILE = 256
SWEEP_GROUP = 16
VROWS = 80
SUPER = 64 * SEL_BLOCK
TW_WIDTH = WINDOW + QB + WINDOW
ROWS = NSA_R * QB

C_QN, C_KVC, C_KSVS, C_KVW, C_ZN, C_QKG, C_VG, C_ZG, C_MISC, C_END = (
    0, 512, 768, 1024, 1280, 1792, 2304, 2816, 3328, 3456)
MISC_GATES = 0
MISC_ALOW = 24
GATE_ROWS = 32


def _bucket_thresholds():
    d = np.arange(0, 4 * MAX_DISTANCE)
    max_exact = NUM_BUCKETS // 2
    nf = np.maximum(d, 1).astype(np.float32)
    large = max_exact + (np.log(nf / np.float32(max_exact)) / np.float32(math.log(MAX_DISTANCE / max_exact))
                         * np.float32(NUM_BUCKETS - max_exact)).astype(np.int32)
    large = np.minimum(large, NUM_BUCKETS - 1)
    bucket = np.where(d < max_exact, d, large)
    assert np.all(np.diff(bucket) >= 0) and bucket[-1] == NUM_BUCKETS - 1
    return [int(np.argmax(bucket >= k)) for k in range(NUM_BUCKETS)]


_THR = _bucket_thresholds()
CMP_BAND = 32
assert (QB - 1 + _THR[NUM_BUCKETS - 1]) // CMP_STRIDE + 1 + 7 <= CMP_BAND
assert SEL_TILE % QB == 0 and _THR[NUM_BUCKETS - 1] <= QB


def _dot(a, b, **kw):
    return jnp.dot(a, b, preferred_element_type=F32, **kw)


def _dot_nt(a, b, **kw):
    return lax.dot_general(a, b, (((1,), (1,)), ((), ())), preferred_element_type=F32, **kw)


def _dot_tn(a, b, **kw):
    return lax.dot_general(a, b, (((0,), (0,)), ((), ())), preferred_element_type=F32, **kw)


def _layer_norm(x, g, b):
    mu = jnp.mean(x, axis=-1, keepdims=True)
    xc = x - mu
    var = jnp.mean(xc * xc, axis=-1, keepdims=True)
    return xc * lax.rsqrt(var + EPS) * g + b


def _sigmoid(x):
    return 1.0 / (1.0 + jnp.exp(-x))


def _silu(x):
    return x * _sigmoid(x)


def _bias_from_dist(rb_ref, dist, valid, put):
    masks = [dist >= _THR[k] for k in range(1, NUM_BUCKETS)]
    for h in range(NSA_HEADS):
        val = jnp.full(dist.shape, rb_ref[0, h], F32)
        for k in range(1, NUM_BUCKETS):
            val = jnp.where(masks[k - 1], rb_ref[k, h], val)
        val = val - rb_ref[NUM_BUCKETS - 1, h]
        put(h, jnp.where(valid, val, NEG))


def _bias_cmp_kernel(rb_ref, out_ref):
    i = pl.program_id(0)
    nc = out_ref.shape[2]
    big = CMP_STRIDE * QB
    j_zero_max = (QB * i - (CMP_LEN - 1) - _THR[NUM_BUCKETS - 1] + CMP_STRIDE * big) // CMP_STRIDE - big
    start = jnp.clip((j_zero_max + 1 + 8 * big) // 8 * 8 - 8 * big, 0, nc - CMP_BAND)
    start = pl.multiple_of(start, 8)
    j_all = lax.broadcasted_iota(jnp.int32, (nc, QB), 0)
    fill = jnp.where(j_all <= j_zero_max, 0.0, NEG)
    for h in range(NSA_HEADS):
        out_ref[0, h] = fill
    j = start + lax.broadcasted_iota(jnp.int32, (CMP_BAND, QB), 0)
    a = lax.broadcasted_iota(jnp.int32, (CMP_BAND, QB), 1)
    dist = a + QB * i - CMP_STRIDE * j - (CMP_LEN - 1)

    def put(h, v):
        out_ref[0, h, pl.ds(start, CMP_BAND), :] = v
    _bias_from_dist(rb_ref, dist, dist >= 0, put)


def _bias_win_kernel(rb_ref, out_ref):
    shape = out_ref.shape[1:]
    y = lax.broadcasted_iota(jnp.int32, shape, 0)
    a = lax.broadcasted_iota(jnp.int32, shape, 1)
    dist = a + WINDOW - y

    def put(h, v):
        out_ref[h] = v
    _bias_from_dist(rb_ref, dist, (dist >= 0) & (dist < WINDOW), put)


def _bias_tables(rel_bias, nb, nc):
    smem = pl.BlockSpec(memory_space=pltpu.SMEM)
    bias_c = pl.pallas_call(
        _bias_cmp_kernel,
        grid=(nb,),
        in_specs=[smem],
        out_specs=pl.BlockSpec((1, NSA_HEADS, nc, QB), lambda i: (i, 0, 0, 0)),
        out_shape=jax.ShapeDtypeStruct((nb, NSA_HEADS, nc, QB), F32),
        compiler_params=pltpu.CompilerParams(dimension_semantics=("parallel",)),
        name="bias_cmp",
    )(rel_bias)
    bias_w = pl.pallas_call(
        _bias_win_kernel,
        in_specs=[smem],
        out_shape=jax.ShapeDtypeStruct((NSA_HEADS, TW_WIDTH, QB), F32),
        name="bias_win",
    )(rel_bias)
    return bias_c, bias_w


def _inproj_kernel(x_ref, g_ref, b_ref, w_ref, hn_ref, qn_ref, xkv_ref, ksa_ref, vsat_ref, kvw_ref, vwat_ref,
                   zs_ref, qkg_ref, vg_ref, zg_ref, misc_ref, sgt_ref, slab_ref):
    t = pl.program_id(1)
    tm = x_ref.shape[1]
    hf = _layer_norm(x_ref[0], g_ref[...], b_ref[...])
    hn_ref[0] = ALPHA * hf
    h = hf.astype(BF16)

    def mm(c0, c1):
        return _dot(h, w_ref[:, c0:c1])

    qn_ref[0] = mm(C_QN, C_KVC).T.astype(BF16)

    kvc = mm(C_KVC, C_KSVS)
    for g in range(NSA_G):
        slab_ref[...] = kvc[:, g * LANES:(g + 1) * LANES]
        for tok in range(CMP_STRIDE):
            rows = slab_ref[pl.ds(tok, tm // CMP_STRIDE, stride=CMP_STRIDE), :]
            xkv_ref[0, g, :, tok * LANES:(tok + 1) * LANES] = rows.astype(BF16)

    lane = lax.broadcasted_iota(jnp.int32, (tm, LANES), 1)
    row = lax.broadcasted_iota(jnp.int32, (tm, LANES), 0)
    low = lane < DH
    blk = ((t * tm + row) // SEL_BLOCK) % (SUPER // SEL_BLOCK)
    onehot = jnp.where(lane - DH == blk, 1.0, 0.0)
    ones_col = jnp.where(lane == DH, 1.0, 0.0)

    def values_t(v_low):
        return jnp.where(low, v_low, ones_col).T[0:VROWS].astype(BF16)

    ksvs = mm(C_KSVS, C_KVW)
    ks, vs = ksvs[:, 0:LANES], ksvs[:, LANES:2 * LANES]
    ksa_ref[0, 0] = jnp.where(low, ks, onehot).astype(BF16)
    ksa_ref[0, 1] = jnp.where(low, pltpu.roll(ks, DH, 1), onehot).astype(BF16)
    vsat_ref[0, 0] = values_t(vs)
    vsat_ref[0, 1] = values_t(pltpu.roll(vs, DH, 1))
    kvw = mm(C_KVW, C_ZN)
    for g in range(NSA_G):
        kvw_g = kvw[:, g * LANES:(g + 1) * LANES]
        kvw_ref[0, g] = kvw_g.astype(BF16)
        vwat_ref[0, g] = values_t(pltpu.roll(kvw_g, DH, 1))
    zs_ref[0] = _silu(mm(C_ZN, C_QKG)).astype(BF16)
    qkg_ref[0] = mm(C_QKG, C_VG)
    vg_ref[0] = mm(C_VG, C_ZG)
    zg_ref[0] = mm(C_ZG, C_MISC)
    misc = mm(C_MISC, C_END)
    misc_ref[0] = misc
    sgt_ref[0] = _sigmoid(misc).T[0:GATE_ROWS]


def _in_projection(x, ln0_g, ln0_b, w_perm, tm):
    B, S, D = x.shape
    tok = lambda w: pl.BlockSpec((1, tm, w), lambda b, t: (b, t, 0))
    tok_t = lambda w: pl.BlockSpec((1, w, tm), lambda b, t: (b, 0, t))
    grp = pl.BlockSpec((1, NSA_G, tm, LANES), lambda b, t: (b, 0, t, 0))
    grp_v = pl.BlockSpec((1, NSA_G, VROWS, tm), lambda b, t: (b, 0, 0, t))
    grp_x = pl.BlockSpec((1, NSA_G, tm // CMP_STRIDE, CMP_STRIDE * LANES), lambda b, t: (b, 0, t, 0))
    full2 = lambda a: pl.BlockSpec(a.shape, lambda b, t: (0, 0))
    sds = jax.ShapeDtypeStruct
    return pl.pallas_call(
        _inproj_kernel,
        grid=(B, S // tm),
        in_specs=[tok(D), full2(ln0_g), full2(ln0_b), full2(w_perm)],
        out_specs=[tok(D), tok_t(512), grp_x, grp, grp_v, grp, grp_v, tok(512), tok(512), tok(512), tok(512),
                   tok(LANES), tok_t(GATE_ROWS)],
        out_shape=[sds((B, S, D), F32), sds((B, 512, S), BF16),
                   sds((B, NSA_G, S // CMP_STRIDE, CMP_STRIDE * LANES), BF16),
                   sds((B, NSA_G, S, LANES), BF16), sds((B, NSA_G, VROWS, S), BF16),
                   sds((B, NSA_G, S, LANES), BF16), sds((B, NSA_G, VROWS, S), BF16),
                   sds((B, S, 512), BF16), sds((B, S, 512), F32), sds((B, S, 512), F32), sds((B, S, 512), F32),
                   sds((B, S, LANES), F32), sds((B, GATE_ROWS, S), F32)],
        scratch_shapes=[pltpu.VMEM((tm, LANES), F32)],
        compiler_params=pltpu.CompilerParams(dimension_semantics=("parallel", "parallel"),
                                             vmem_limit_bytes=VMEM_LIMIT),
        name="ln0_inproj",
    )(x, ln0_g, ln0_b, w_perm)


def _gelu_tanh(x):
    c = math.sqrt(2.0 / math.pi)
    return x * (0.5 * (1.0 + jnp.tanh(c * (x + 0.044715 * (x * x * x)))))


def _compress_kernel(x_ref, ptop_ref, pbot_ref, wtop_ref, wbot_ref, b1_ref, w2_ref, out_ref, outt_ref):
    nch = x_ref.shape[2]
    x = x_ref[0, 0]
    top = _dot(x, wtop_ref[...])
    bot = _dot(x, wbot_ref[...])
    cpos = (_dot(ptop_ref[...], wtop_ref[...]) + _dot(pbot_ref[...], wbot_ref[...]))[0:1, :]
    pre = top + pltpu.roll(bot, nch - 1, 0) + cpos + b1_ref[...]
    out = _dot(_gelu_tanh(pre).astype(BF16), w2_ref[...])
    out_ref[0, 0] = out.astype(BF16)
    outt_ref[0, 0] = out.T.astype(BF16)


def _compress(xkv, ptop, pbot, wtop, wbot, b1, w2):
    B, G, nch, width = xkv.shape
    full2 = lambda a: pl.BlockSpec(a.shape, lambda b, g: (0, 0))
    return pl.pallas_call(
        _compress_kernel,
        grid=(B, G),
        in_specs=[pl.BlockSpec((1, 1, nch, width), lambda b, g: (b, g, 0, 0)),
                  full2(ptop), full2(pbot), full2(wtop), full2(wbot), full2(b1), full2(w2)],
        out_specs=[pl.BlockSpec((1, 1, nch, LANES), lambda b, g: (b, g, 0, 0)),
                   pl.BlockSpec((1, 1, LANES, nch), lambda b, g: (b, g, 0, 0))],
        out_shape=[jax.ShapeDtypeStruct((B, G, nch, LANES), BF16),
                   jax.ShapeDtypeStruct((B, G, LANES, nch), BF16)],
        compiler_params=pltpu.CompilerParams(dimension_semantics=("parallel", "parallel"),
                                             vmem_limit_bytes=VMEM_LIMIT),
        name="kv_compress",
    )(xkv, ptop, pbot, wtop, wbot, b1, w2)


def _compress_params(pos, wk1, bk1, wk2, wv1, bv1, wv2):
    half = CMP_STRIDE * DH
    zero = jnp.zeros((CMP_STRIDE, DH, CMP_HIDDEN), F32)

    def rows(w_k, w_v):
        k_rows = jnp.concatenate([w_k.reshape(CMP_STRIDE, DH, CMP_HIDDEN), zero], axis=2)
        v_rows = jnp.concatenate([zero, w_v.reshape(CMP_STRIDE, DH, CMP_HIDDEN)], axis=2)
        return jnp.concatenate([k_rows, v_rows], axis=1).reshape(CMP_STRIDE * LANES, 2 * CMP_HIDDEN).astype(BF16)

    def pos_row(p):
        r = jnp.concatenate([p, p], axis=1).reshape(1, CMP_STRIDE * LANES)
        return jnp.broadcast_to(r, (8, CMP_STRIDE * LANES)).astype(BF16)

    zpad = jnp.zeros((CMP_HIDDEN, DH), F32)
    w2 = jnp.concatenate([jnp.concatenate([wk2, zpad], axis=1),
                          jnp.concatenate([zpad, wv2], axis=1)], axis=0).astype(BF16)
    b1 = jnp.concatenate([bk1, bv1]).reshape(1, -1)
    return (pos_row(pos[:CMP_STRIDE]), pos_row(pos[CMP_STRIDE:]), rows(wk1[:half], wv1[:half]),
            rows(wk1[half:], wv1[half:]), b1, w2)


def _heads_on_lanes(t4):
    return jnp.concatenate([t4[r] for r in range(NSA_R)], axis=1)


def _nsa_kernel(q_ref, ksa_ref, vsat_ref, kvw_ref, vwat_ref, kvc_ref, kvct_ref, tw_ref, bc_ref, sgt_ref,
                o_ref, qa_ref, m_ref, acc_ref, s_ref, ps_ref):
    g = pl.program_id(1)
    i = pl.program_id(2)
    t0 = i * QB
    jd1 = (t0 + QB - 1) // SEL_TILE
    n_far = jnp.maximum(t0 - (_THR[NUM_BUCKETS - 1] - 1), 0) // SEL_TILE

    def score_tile(j, ntiles, bias_off):
        n = ntiles * SEL_TILE
        k0 = pl.multiple_of(j * SEL_TILE, SEL_TILE)
        sc = _dot(ksa_ref[0, 0, pl.ds(k0, n), :], qa_ref[j // (SUPER // SEL_TILE)])
        if bias_off is not None:
            off = pl.multiple_of(bias_off, LANES)
            sc = sc + _heads_on_lanes(tw_ref[:, pl.ds(off, n), :])
        s_ref[pl.ds(k0, n), :] = sc.astype(s_ref.dtype)
        return jnp.max(sc, axis=0, keepdims=True)

    def pv_tile(j, ntiles):
        n = ntiles * SEL_TILE
        k0 = pl.multiple_of(j * SEL_TILE, SEL_TILE)
        pt = jnp.exp(s_ref[pl.ds(k0, n), :] - m_ref[...].astype(s_ref.dtype))
        return _dot(vsat_ref[0, 0, :, pl.ds(k0, n)], pt)

    def sweep(total, block):
        n_grp = total // SWEEP_GROUP

        def body(jj, carry):
            block(SWEEP_GROUP * jj, SWEEP_GROUP)
            return carry
        lax.fori_loop(0, n_grp, body, 0)
        size = SWEEP_GROUP // 2
        while size >= 1:
            first = (total // (2 * size)) * (2 * size)

            @pl.when((total // size) % 2 == 1)
            def _(first=first, size=size):
                block(first, size)
            size //= 2

    def score_block(j, ntiles):
        mm = m_ref[...]
        for h in range(0, ntiles, 2):
            mm = jnp.maximum(mm, score_tile(j + h, min(2, ntiles - h), None))
        m_ref[...] = mm

    def pv_block(j, ntiles):
        part = None
        for h in range(0, ntiles, 2):
            pv = pv_tile(j + h, min(2, ntiles - h))
            part = pv if part is None else part + pv
        acc_ref[...] += part

    def front():
        qt = q_ref[0].astype(F32)
        q_t = jnp.concatenate([qt[r * DH:(r + 1) * DH] for r in range(NSA_R)], axis=1)
        qpad_t = jnp.concatenate([q_t, jnp.zeros((DH, ROWS), F32)], axis=0).astype(BF16)

        bc = _heads_on_lanes(bc_ref[0])
        s = _dot(kvc_ref[0, 0], qpad_t) + bc
        m = jnp.max(s, axis=0, keepdims=True)
        p = jnp.exp(s - m)
        l = jnp.sum(p, axis=0, keepdims=True)
        pn = p * jnp.where(m > 0.5 * NEG, 1.0 / l, 0.0)
        o_c = _dot(kvct_ref[0, 0], pn.astype(BF16))[DH:2 * DH]

        wlen = WINDOW + QB
        start = pl.multiple_of(jnp.maximum(t0 - WINDOW, 0), LANES)
        offw = pl.multiple_of(start - t0 + WINDOW, LANES)
        sw = _dot(kvw_ref[0, 0, pl.ds(start, wlen), :], qpad_t) + _heads_on_lanes(tw_ref[:, pl.ds(offw, wlen), :])
        mw = jnp.max(sw, axis=0, keepdims=True)
        pw = jnp.exp((sw - mw).astype(BF16))
        acc_w = _dot(vwat_ref[0, 0, :, pl.ds(start, wlen)], pw)
        o_w = acc_w[0:DH] / acc_w[DH:DH + 1]

        sg = sgt_ref[0]

        def gate_row(c):
            rows = []
            for r in range(NSA_R):
                c0 = MISC_GATES + 3 * r + c
                c1 = c0 + 3 * NSA_R
                rows.append(jnp.where(g == 0, sg[c0:c0 + 1], sg[c1:c1 + 1]))
            return jnp.concatenate(rows, axis=1)
        partial = gate_row(0) * o_c + gate_row(2) * o_w
        gate_sel = gate_row(1)

        nc = pn.shape[0]
        nsel = nc * CMP_STRIDE // SEL_BLOCK
        per_blk = SEL_BLOCK // CMP_STRIDE
        psum = pn[:, 0:QB] + pn[:, QB:2 * QB] + pn[:, 2 * QB:3 * QB] + pn[:, 3 * QB:4 * QB]
        for c in range(QB // LANES):
            ps_ref[c, 0:8] = jnp.zeros((8, LANES), F32)
            ps_ref[c, 8:8 + nc] = psum[:, c * LANES:(c + 1) * LANES]
        tok = lambda k: jnp.concatenate(
            [ps_ref.at[c][pl.ds(8 + k, nsel, stride=per_blk), :] for c in range(QB // LANES)], axis=1)
        imp = 0.5 * (tok(-1) + tok(3)) + tok(0) + tok(1) + tok(2)
        nidx = lax.broadcasted_iota(jnp.int32, (nsel, QB), 0)
        qidx = lax.broadcasted_iota(jnp.int32, (nsel, QB), 1)
        qblk = (t0 + qidx) // SEL_BLOCK
        causal = nidx <= qblk
        forced = causal & ((nidx == 0) | (nidx >= qblk - (N_LOCAL - 1)))
        n_forced = N_LOCAL + 1
        w = jnp.where(causal & jnp.logical_not(forced), imp, -3e38)
        nf = nidx.astype(F32)
        sel = jnp.where(forced | (causal & (qblk < TOP_N)), 1.0, 0.0)
        for _ in range(min(TOP_N - n_forced, nsel)):
            mx = jnp.max(w, axis=0, keepdims=True)
            first = jnp.min(jnp.where(w == mx, nf, float(nsel)), axis=0, keepdims=True)
            pick = nf == first
            sel = jnp.where(pick, 1.0, sel)
            w = jnp.where(pick, -3e38, w)
        sel_bias = jnp.where((sel > 0.5) & causal, 0.0, NEG)

        for u in range(qa_ref.shape[0]):
            sb = sel_bias[u * 64:(u + 1) * 64]
            if sb.shape[0] < 64:
                sb = jnp.concatenate([sb, jnp.full((64 - sb.shape[0], QB), NEG, F32)], axis=0)
            sb4 = jnp.concatenate([sb] * NSA_R, axis=1)
            qa_ref[u] = jnp.concatenate([q_t, sb4], axis=0).astype(BF16)
        return partial, gate_sel

    partial, gate_sel = front()

    m_ref[...] = jnp.full(m_ref.shape, NEG, F32)
    sweep(n_far, score_block)
    m_lo = score_tile(n_far, 1, WINDOW - (t0 - n_far * SEL_TILE))
    m_hi = score_tile(jd1, 1, WINDOW - (t0 - jd1 * SEL_TILE))
    m_ref[...] = jnp.maximum(m_ref[...], jnp.maximum(m_lo, m_hi))

    acc_ref[...] = jnp.zeros(acc_ref.shape, F32)
    sweep(jd1 + 1, pv_block)
    acc = acc_ref[...]
    o_s = acc[0:DH] / acc[DH:DH + 1]
    out_t = partial + gate_sel * o_s
    out_t = jnp.concatenate([out_t[:, r * QB:(r + 1) * QB] for r in range(NSA_R)], axis=0)
    o_ref[0] = out_t.astype(o_ref.dtype)


def _nsa(qn_t, ksa, vsat, kvw, vwat, kvc, kvct, bias_w, bias_c, sg_t):
    B, _, S = qn_t.shape
    nb = S // QB
    nc = kvc.shape[2]
    n_super = -(-S // SUPER)
    seq = lambda a: pl.BlockSpec((1, 1) + a.shape[2:], lambda b, g, i: (b, g, 0, 0))
    return pl.pallas_call(
        _nsa_kernel,
        grid=(B, NSA_G, nb),
        in_specs=[
            pl.BlockSpec((1, NSA_R * DH, QB), lambda b, g, i: (b, g, i)),
            seq(ksa), seq(vsat), seq(kvw), seq(vwat), seq(kvc), seq(kvct),
            pl.BlockSpec((NSA_R, TW_WIDTH, QB), lambda b, g, i: (g, 0, 0)),
            pl.BlockSpec((1, NSA_R, nc, QB), lambda b, g, i: (i, g, 0, 0)),
            pl.BlockSpec((1, GATE_ROWS, QB), lambda b, g, i: (b, 0, i)),
        ],
        out_specs=pl.BlockSpec((1, NSA_R * DH, QB), lambda b, g, i: (b, g, i)),
        out_shape=jax.ShapeDtypeStruct((B, NSA_HEADS * DH, S), BF16),
        scratch_shapes=[pltpu.VMEM((n_super, LANES, ROWS), BF16),
                        pltpu.VMEM((1, ROWS), F32),
                        pltpu.VMEM((VROWS, ROWS), F32),
                        pltpu.VMEM((S, ROWS), BF16),
                        pltpu.VMEM((QB // LANES, nc + 8, LANES), F32)],
        compiler_params=pltpu.CompilerParams(dimension_semantics=("parallel", "parallel", "arbitrary"),
                                             vmem_limit_bytes=VMEM_LIMIT),
        name="nsa_attention",
    )(qn_t, ksa, vsat, kvw, vwat, kvc, kvct, bias_w, bias_c, sg_t)


def _gla_kernel(qk_ref, v_ref, z_ref, misc_ref, wa_ref, ba_ref, nw_ref, tri_ref, o_ref, st_ref):
    t = pl.program_id(2)
    T = qk_ref.shape[1]
    C = GLA_CHUNK

    @pl.when(t == 0)
    def _():
        st_ref[...] = jnp.zeros(st_ref.shape, F32)

    lane = lax.broadcasted_iota(jnp.int32, (1, LANES), 1)
    sign = jnp.where(lane < GLA_DK, 1.0 / GLA_TAU, -1.0 / GLA_TAU)
    zz = _dot(misc_ref[0].astype(BF16), wa_ref[0]) + ba_ref[0]
    log_sig = jnp.minimum(zz, 0.0) - jnp.log(1.0 + jnp.exp(-jnp.abs(zz)))
    la2 = log_sig * sign
    rr = lax.broadcasted_iota(jnp.int32, (C, C), 0)
    cc = lax.broadcasted_iota(jnp.int32, (C, C), 1)
    tril = rr >= cc
    scale = GLA_DK ** -0.5

    chunks = [slice(n * C, (n + 1) * C) for n in range(T // C)]
    hi = la2.astype(BF16)
    rem = la2 - hi.astype(F32)
    mid = rem.astype(BF16)
    lo = (rem - mid.astype(F32)).astype(BF16)
    tri = tri_ref[...]
    b2 = jnp.concatenate([_dot(tri, hi[sl]) + _dot(tri, mid[sl]) + _dot(tri, lo[sl]) for sl in chunks], axis=0)
    e = jnp.exp(b2)
    qke = qk_ref[0] * e
    q_e = (qke[:, 0:GLA_DK] * scale).astype(BF16)
    k_e = qke[:, GLA_DK:2 * GLA_DK].astype(BF16)
    v = v_ref[0].astype(BF16)
    attn = [jnp.where(tril, _dot_nt(q_e[sl], k_e[sl]), 0.0).astype(BF16) for sl in chunks]
    o_intra = [_dot(a, v[sl]) for a, sl in zip(attn, chunks)]
    kv = [_dot_tn(v[sl], k_e[sl]) for sl in chunks]
    st = st_ref[...]
    states = []
    for n, sl in enumerate(chunks):
        states.append(st.astype(BF16))
        st = (st + kv[n]) * e[sl.stop - 1:sl.stop, 0:GLA_DK]
    st_ref[...] = st
    o = jnp.concatenate([oi + _dot_nt(q_e[sl], s_in) for oi, sl, s_in in zip(o_intra, chunks, states)], axis=0)
    y = o * lax.rsqrt(jnp.mean(o * o, axis=-1, keepdims=True) + EPS) * nw_ref[...]
    o_ref[0] = (y * _silu(z_ref[0])).astype(o_ref.dtype)


def _gla(qkg, vg, zg, misc, wa_big, ba2, norm_w, tri, T):
    B, S, _ = qkg.shape
    hb = pl.BlockSpec((1, T, LANES), lambda b, h, t: (b, t, h))
    return pl.pallas_call(
        _gla_kernel,
        grid=(B, GLA_HEADS, S // T),
        in_specs=[hb, hb, hb,
                  pl.BlockSpec((1, T, LANES), lambda b, h, t: (b, t, 0)),
                  pl.BlockSpec((1, LANES, LANES), lambda b, h, t: (h, 0, 0)),
                  pl.BlockSpec((1, 1, LANES), lambda b, h, t: (h, 0, 0)),
                  pl.BlockSpec(norm_w.shape, lambda b, h, t: (0, 0)),
                  pl.BlockSpec(tri.shape, lambda b, h, t: (0, 0))],
        out_specs=hb,
        out_shape=jax.ShapeDtypeStruct((B, S, GLA_HEADS * GLA_DV), BF16),
        scratch_shapes=[pltpu.VMEM((GLA_DV, GLA_DK), F32)],
        compiler_params=pltpu.CompilerParams(dimension_semantics=("parallel", "parallel", "arbitrary"),
                                             vmem_limit_bytes=VMEM_LIMIT),
        name="gla_chunked",
    )(qkg, vg, zg, misc, wa_big, ba2, norm_w, tri)


def _out_kernel(hn_ref, on_ref, zs_ref, og_ref, p_ref, wo_ref, wpg_ref, bpg_ref, wpe_ref, lg_ref, lb_ref, o_ref):
    sub = 256
    for c in range(hn_ref.shape[1] // sub):
        rows = slice(c * sub, (c + 1) * sub)
        o_nsa = on_ref[0, :, rows].astype(F32).T.astype(BF16)
        mix_n = o_nsa * zs_ref[0, rows, :]
        mix = jnp.concatenate([mix_n, og_ref[0, rows, :]], axis=1)
        r = hn_ref[0, rows, :] + _dot(mix, wo_ref[...])
        gate = _sigmoid(_dot(r.astype(BF16), wpg_ref[...]) + bpg_ref[...])
        r = r + gate * _dot(p_ref[0, rows, :].astype(BF16), wpe_ref[...])
        o_ref[0, rows, :] = _layer_norm(r, lg_ref[...], lb_ref[...])


def _out_projection(hn, o_nsa, zs, o_gla, p, w_out, w_pg, b_pg, w_pe, ln_g, ln_b, tm):
    B, S, D = hn.shape
    tok = lambda w: pl.BlockSpec((1, tm, w), lambda b, t: (b, t, 0))
    full2 = lambda a: pl.BlockSpec(a.shape, lambda b, t: (0, 0))
    return pl.pallas_call(
        _out_kernel,
        grid=(B, S // tm),
        in_specs=[tok(D), pl.BlockSpec((1, 512, tm), lambda b, t: (b, 0, t)), tok(512), tok(512), tok(PLE_DIM),
                  full2(w_out), full2(w_pg), full2(b_pg), full2(w_pe), full2(ln_g), full2(ln_b)],
        out_specs=tok(D),
        out_shape=jax.ShapeDtypeStruct((B, S, D), F32),
        compiler_params=pltpu.CompilerParams(dimension_semantics=("parallel", "parallel"),
                                             vmem_limit_bytes=VMEM_LIMIT),
        name="out_proj_deepnorm",
    )(hn, o_nsa, zs, o_gla, p, w_out, w_pg, b_pg, w_pe, ln_g, ln_b)


def _permute_w_in(w):
    widths = (512, 128, 128, 128, 128, 128, 128, 24, 512, 256, 256, 512, 16, 512)
    offs = np.concatenate([[0], np.cumsum(widths)])
    (q_n, kc, vc, ks, vs, kw, vw, gates, z_n, q_g, k_g, v_g, a_low, z_g) = [
        w[:, int(offs[k]):int(offs[k + 1])] for k in range(len(widths))]
    pair = lambda a, b: [jnp.concatenate([a[:, g * DH:(g + 1) * DH], b[:, g * DH:(g + 1) * DH]], axis=1)
                         for g in range(NSA_G)]
    kvc = pair(kc, vc)
    kvw = pair(kw, vw)
    qk = jnp.concatenate([jnp.concatenate([q_g[:, h * GLA_DK:(h + 1) * GLA_DK],
                                           k_g[:, h * GLA_DK:(h + 1) * GLA_DK]], axis=1)
                          for h in range(GLA_HEADS)], axis=1)
    misc = jnp.concatenate([gates, a_low,
                            jnp.zeros((w.shape[0], LANES - gates.shape[1] - a_low.shape[1]), w.dtype)], axis=1)
    cols = [q_n * (DH ** -0.5), kvc[0], kvc[1], ks, vs, kvw[0], kvw[1], z_n, qk, v_g, z_g, misc]
    return jnp.concatenate(cols, axis=1).astype(BF16)


def kernel(x, p, ln0_g, ln0_b, rel_bias, w_in, w_a2, b_a, gla_norm_w, pos_cmp, w_ck1, b_ck1, w_ck2,
           w_cv1, b_cv1, w_cv2, w_out, w_pe, w_pg, b_pg, ln_g, ln_b):
    B, S, D = x.shape
    assert D == D_MODEL and S % 512 == 0 and S >= WINDOW + QB and w_in.shape[0] == DEPTH == 1
    nb = S // QB
    nch = S // CMP_STRIDE
    row = lambda a: a.reshape(1, -1)

    bias_c, bias_w = _bias_tables(rel_bias, nb, nch)

    w_perm = _permute_w_in(w_in[0])
    (hn, qn_t, xkv, ksa, vsat, kvw, vwat, zs, qkg, vg, zg, misc, sg_t) = _in_projection(
        x, row(ln0_g), row(ln0_b), w_perm, TOKEN_TILE)

    kvc, kvct = _compress(xkv, *_compress_params(pos_cmp[0], w_ck1[0], b_ck1[0], w_ck2[0],
                                                 w_cv1[0], b_cv1[0], w_cv2[0]))

    o_nsa = _nsa(qn_t, ksa, vsat, kvw, vwat, kvc, kvct, bias_w, bias_c, sg_t)

    wa = w_a2[0]
    wa_big = jnp.zeros((GLA_HEADS, LANES, LANES), F32)
    for h in range(GLA_HEADS):
        wh = wa[:, h * GLA_DK:(h + 1) * GLA_DK]
        wa_big = wa_big.at[h, MISC_ALOW:MISC_ALOW + GLA_RANK, :].set(jnp.concatenate([wh, wh], axis=1))
    ba = b_a[0].reshape(GLA_HEADS, 1, GLA_DK)
    ba2 = jnp.concatenate([ba, ba], axis=2)
    tri = jnp.asarray(np.tril(np.ones((GLA_CHUNK, GLA_CHUNK), np.float32))).astype(BF16)
    gla_tile = math.gcd(S, GLA_TILE)
    o_gla = _gla(qkg, vg, zg, misc, wa_big.astype(BF16), ba2, row(gla_norm_w[0]), tri, gla_tile)

    return _out_projection(hn, o_nsa, zs, o_gla, p[0], w_out[0].astype(BF16), w_pg[0].astype(BF16),
                           row(b_pg[0]), w_pe[0].astype(BF16), row(ln_g[0]), row(ln_b[0]), TOKEN_TILE)
```

```python
import math

import numpy as np
import jax
import jax.numpy as jnp
from jax import lax
from jax.experimental import pallas as pl
from jax.experimental.pallas import tpu as pltpu

F32 = jnp.float32
BF16 = jnp.bfloat16

D_MODEL = 1024
PLE_DIM = 256
NSA_HEADS = 8
NSA_G = 2
NSA_R = NSA_HEADS // NSA_G
DH = 64
CMP_LEN = 32
CMP_STRIDE = 16
CMP_HIDDEN = 256
SEL_BLOCK = 64
TOP_N = 16
N_LOCAL = 2
WINDOW = 512
QB = 256
GLA_HEADS = 4
GLA_DK = 64
GLA_DV = 128
GLA_RANK = 16
GLA_TAU = 16.0
GLA_CHUNK = 64
NUM_BUCKETS = 32
MAX_DISTANCE = 128
DEPTH = 1
ALPHA = (2.0 * DEPTH) ** 0.25
EPS = 1e-5
NEG = -1e30
POS = 1e30

LANES = 128
VMEM_LIMIT = 56 * 1024 * 1024

TOKEN_TILE = 512
GLA_TILE = 4096
SELECT_BLOCKS = 2
SEL_TILE = 256
SWEEP_GROUP = 16
VROWS = 80
SUPER = 64 * SEL_BLOCK
TW_WIDTH = WINDOW + QB + WINDOW
ROWS = NSA_R * QB

C_QN, C_KVC, C_KSVS, C_KVW, C_ZN, C_QKG, C_VG, C_ZG, C_MISC, C_END = (
    0, 512, 768, 1024, 1280, 1792, 2304, 2816, 3328, 3456)
MISC_GATES = 0
MISC_ALOW = 24
GATE_ROWS = 32


def _bucket_thresholds():
    d = np.arange(0, 4 * MAX_DISTANCE)
    max_exact = NUM_BUCKETS // 2
    nf = np.maximum(d, 1).astype(np.float32)
    large = max_exact + (np.log(nf / np.float32(max_exact)) / np.float32(math.log(MAX_DISTANCE / max_exact))
                         * np.float32(NUM_BUCKETS - max_exact)).astype(np.int32)
    large = np.minimum(large, NUM_BUCKETS - 1)
    bucket = np.where(d < max_exact, d, large)
    assert np.all(np.diff(bucket) >= 0) and bucket[-1] == NUM_BUCKETS - 1
    return [int(np.argmax(bucket >= k)) for k in range(NUM_BUCKETS)]


_THR = _bucket_thresholds()
CMP_BAND = 32
assert (QB - 1 + _THR[NUM_BUCKETS - 1]) // CMP_STRIDE + 1 + 7 <= CMP_BAND
assert SEL_TILE % QB == 0 and _THR[NUM_BUCKETS - 1] <= QB


def _dot(a, b, **kw):
    return jnp.dot(a, b, preferred_element_type=F32, **kw)


def _dot_nt(a, b, **kw):
    return lax.dot_general(a, b, (((1,), (1,)), ((), ())), preferred_element_type=F32, **kw)


def _dot_tn(a, b, **kw):
    return lax.dot_general(a, b, (((0,), (0,)), ((), ())), preferred_element_type=F32, **kw)


def _layer_norm(x, g, b):
    mu = jnp.mean(x, axis=-1, keepdims=True)
    xc = x - mu
    var = jnp.mean(xc * xc, axis=-1, keepdims=True)
    return xc * lax.rsqrt(var + EPS) * g + b


def _sigmoid(x):
    return 1.0 / (1.0 + jnp.exp(-x))


def _silu(x):
    return x * _sigmoid(x)


def _bias_from_dist(rb_ref, dist, valid, put):
    masks = [dist >= _THR[k] for k in range(1, NUM_BUCKETS)]
    for h in range(NSA_HEADS):
        val = jnp.full(dist.shape, rb_ref[0, h], F32)
        for k in range(1, NUM_BUCKETS):
            val = jnp.where(masks[k - 1], rb_ref[k, h], val)
        val = val - rb_ref[NUM_BUCKETS - 1, h]
        put(h, jnp.where(valid, val, NEG))


def _bias_cmp_kernel(rb_ref, out_ref):
    i = pl.program_id(0)
    nc = out_ref.shape[2]
    big = CMP_STRIDE * QB
    j_zero_max = (QB * i - (CMP_LEN - 1) - _THR[NUM_BUCKETS - 1] + CMP_STRIDE * big) // CMP_STRIDE - big
    start = jnp.clip((j_zero_max + 1 + 8 * big) // 8 * 8 - 8 * big, 0, nc - CMP_BAND)
    start = pl.multiple_of(start, 8)
    j_all = lax.broadcasted_iota(jnp.int32, (nc, QB), 0)
    fill = jnp.where(j_all <= j_zero_max, 0.0, NEG)
    for h in range(NSA_HEADS):
        out_ref[0, h] = fill
    j = start + lax.broadcasted_iota(jnp.int32, (CMP_BAND, QB), 0)
    a = lax.broadcasted_iota(jnp.int32, (CMP_BAND, QB), 1)
    dist = a + QB * i - CMP_STRIDE * j - (CMP_LEN - 1)

    def put(h, v):
        out_ref[0, h, pl.ds(start, CMP_BAND), :] = v
    _bias_from_dist(rb_ref, dist, dist >= 0, put)


def _bias_win_kernel(rb_ref, out_ref):
    shape = out_ref.shape[1:]
    y = lax.broadcasted_iota(jnp.int32, shape, 0)
    a = lax.broadcasted_iota(jnp.int32, shape, 1)
    dist = a + WINDOW - y

    def put(h, v):
        out_ref[h] = v
    _bias_from_dist(rb_ref, dist, (dist >= 0) & (dist < WINDOW), put)


def _bias_tables(rel_bias, nb, nc):
    smem = pl.BlockSpec(memory_space=pltpu.SMEM)
    bias_c = pl.pallas_call(
        _bias_cmp_kernel,
        grid=(nb,),
        in_specs=[smem],
        out_specs=pl.BlockSpec((1, NSA_HEADS, nc, QB), lambda i: (i, 0, 0, 0)),
        out_shape=jax.ShapeDtypeStruct((nb, NSA_HEADS, nc, QB), F32),
        compiler_params=pltpu.CompilerParams(dimension_semantics=("parallel",)),
        name="bias_cmp",
    )(rel_bias)
    bias_w = pl.pallas_call(
        _bias_win_kernel,
        in_specs=[smem],
        out_shape=jax.ShapeDtypeStruct((NSA_HEADS, TW_WIDTH, QB), F32),
        name="bias_win",
    )(rel_bias)
    return bias_c, bias_w


def _inproj_kernel(x_ref, g_ref, b_ref, w_ref, hn_ref, qn_ref, xkv_ref, ksa_ref, vsat_ref, kvw_ref, vwat_ref,
                   zs_ref, qkg_ref, vg_ref, zg_ref, misc_ref, sgt_ref, slab_ref):
    t = pl.program_id(1)
    tm = x_ref.shape[1]
    hf = _layer_norm(x_ref[0], g_ref[...], b_ref[...])
    hn_ref[0] = ALPHA * hf
    h = hf.astype(BF16)

    def mm(c0, c1):
        return _dot(h, w_ref[:, c0:c1])

    qn_ref[0] = mm(C_QN, C_KVC).T.astype(BF16)

    kvc = mm(C_KVC, C_KSVS)
    for g in range(NSA_G):
        slab_ref[...] = kvc[:, g * LANES:(g + 1) * LANES]
        for tok in range(CMP_STRIDE):
            rows = slab_ref[pl.ds(tok, tm // CMP_STRIDE, stride=CMP_STRIDE), :]
            xkv_ref[0, g, :, tok * LANES:(tok + 1) * LANES] = rows.astype(BF16)

    lane = lax.broadcasted_iota(jnp.int32, (tm, LANES), 1)
    row = lax.broadcasted_iota(jnp.int32, (tm, LANES), 0)
    low = lane < DH
    blk = ((t * tm + row) // SEL_BLOCK) % (SUPER // SEL_BLOCK)
    onehot = jnp.where(lane - DH == blk, 1.0, 0.0)
    ones_col = jnp.where(lane == DH, 1.0, 0.0)

    def values_t(v_low):
        return jnp.where(low, v_low, ones_col).T[0:VROWS].astype(BF16)

    ksvs = mm(C_KSVS, C_KVW)
    ks, vs = ksvs[:, 0:LANES], ksvs[:, LANES:2 * LANES]
    ksa_ref[0, 0] = jnp.where(low, ks, onehot).astype(BF16)
    ksa_ref[0, 1] = jnp.where(low, pltpu.roll(ks, DH, 1), onehot).astype(BF16)
    vsat_ref[0, 0] = values_t(vs)
    vsat_ref[0, 1] = values_t(pltpu.roll(vs, DH, 1))
    kvw = mm(C_KVW, C_ZN)
    for g in range(NSA_G):
        kvw_g = kvw[:, g * LANES:(g + 1) * LANES]
        kvw_ref[0, g] = kvw_g.astype(BF16)
        vwat_ref[0, g] = values_t(pltpu.roll(kvw_g, DH, 1))
    zs_ref[0] = _silu(mm(C_ZN, C_QKG)).astype(BF16)
    qkg_ref[0] = mm(C_QKG, C_VG)
    vg_ref[0] = mm(C_VG, C_ZG)
    zg_ref[0] = mm(C_ZG, C_MISC)
    misc = mm(C_MISC, C_END)
    misc_ref[0] = misc
    sgt_ref[0] = _sigmoid(misc).T[0:GATE_ROWS]


def _in_projection(x, ln0_g, ln0_b, w_perm, tm):
    B, S, D = x.shape
    tok = lambda w: pl.BlockSpec((1, tm, w), lambda b, t: (b, t, 0))
    tok_t = lambda w: pl.BlockSpec((1, w, tm), lambda b, t: (b, 0, t))
    grp = pl.BlockSpec((1, NSA_G, tm, LANES), lambda b, t: (b, 0, t, 0))
    grp_v = pl.BlockSpec((1, NSA_G, VROWS, tm), lambda b, t: (b, 0, 0, t))
    grp_x = pl.BlockSpec((1, NSA_G, tm // CMP_STRIDE, CMP_STRIDE * LANES), lambda b, t: (b, 0, t, 0))
    full2 = lambda a: pl.BlockSpec(a.shape, lambda b, t: (0, 0))
    sds = jax.ShapeDtypeStruct
    return pl.pallas_call(
        _inproj_kernel,
        grid=(B, S // tm),
        in_specs=[tok(D), full2(ln0_g), full2(ln0_b), full2(w_perm)],
        out_specs=[tok(D), tok_t(512), grp_x, grp, grp_v, grp, grp_v, tok(512), tok(512), tok(512), tok(512),
                   tok(LANES), tok_t(GATE_ROWS)],
        out_shape=[sds((B, S, D), F32), sds((B, 512, S), BF16),
                   sds((B, NSA_G, S // CMP_STRIDE, CMP_STRIDE * LANES), BF16),
                   sds((B, NSA_G, S, LANES), BF16), sds((B, NSA_G, VROWS, S), BF16),
                   sds((B, NSA_G, S, LANES), BF16), sds((B, NSA_G, VROWS, S), BF16),
                   sds((B, S, 512), BF16), sds((B, S, 512), F32), sds((B, S, 512), F32), sds((B, S, 512), F32),
                   sds((B, S, LANES), F32), sds((B, GATE_ROWS, S), F32)],
        scratch_shapes=[pltpu.VMEM((tm, LANES), F32)],
        compiler_params=pltpu.CompilerParams(dimension_semantics=("parallel", "parallel"),
                                             vmem_limit_bytes=VMEM_LIMIT),
        name="ln0_inproj",
    )(x, ln0_g, ln0_b, w_perm)


def _gelu_tanh(x):
    c = math.sqrt(2.0 / math.pi)
    return x * (0.5 * (1.0 + jnp.tanh(c * (x + 0.044715 * (x * x * x)))))


def _compress_kernel(x_ref, ptop_ref, pbot_ref, wtop_ref, wbot_ref, b1_ref, w2_ref, out_ref, outt_ref):
    nch = x_ref.shape[2]
    x = x_ref[0, 0]
    top = _dot(x, wtop_ref[...])
    bot = _dot(x, wbot_ref[...])
    cpos = (_dot(ptop_ref[...], wtop_ref[...]) + _dot(pbot_ref[...], wbot_ref[...]))[0:1, :]
    pre = top + pltpu.roll(bot, nch - 1, 0) + cpos + b1_ref[...]
    out = _dot(_gelu_tanh(pre).astype(BF16), w2_ref[...])
    out_ref[0, 0] = out.astype(BF16)
    outt_ref[0, 0] = out.T.astype(BF16)


def _compress(xkv, ptop, pbot, wtop, wbot, b1, w2):
    B, G, nch, width = xkv.shape
    full2 = lambda a: pl.BlockSpec(a.shape, lambda b, g: (0, 0))
    return pl.pallas_call(
        _compress_kernel,
        grid=(B, G),
        in_specs=[pl.BlockSpec((1, 1, nch, width), lambda b, g: (b, g, 0, 0)),
                  full2(ptop), full2(pbot), full2(wtop), full2(wbot), full2(b1), full2(w2)],
        out_specs=[pl.BlockSpec((1, 1, nch, LANES), lambda b, g: (b, g, 0, 0)),
                   pl.BlockSpec((1, 1, LANES, nch), lambda b, g: (b, g, 0, 0))],
        out_shape=[jax.ShapeDtypeStruct((B, G, nch, LANES), BF16),
                   jax.ShapeDtypeStruct((B, G, LANES, nch), BF16)],
        compiler_params=pltpu.CompilerParams(dimension_semantics=("parallel", "parallel"),
                                             vmem_limit_bytes=VMEM_LIMIT),
        name="kv_compress",
    )(xkv, ptop, pbot, wtop, wbot, b1, w2)


def _compress_params(pos, wk1, bk1, wk2, wv1, bv1, wv2):
    half = CMP_STRIDE * DH
    zero = jnp.zeros((CMP_STRIDE, DH, CMP_HIDDEN), F32)

    def rows(w_k, w_v):
        k_rows = jnp.concatenate([w_k.reshape(CMP_STRIDE, DH, CMP_HIDDEN), zero], axis=2)
        v_rows = jnp.concatenate([zero, w_v.reshape(CMP_STRIDE, DH, CMP_HIDDEN)], axis=2)
        return jnp.concatenate([k_rows, v_rows], axis=1).reshape(CMP_STRIDE * LANES, 2 * CMP_HIDDEN).astype(BF16)

    def pos_row(p):
        r = jnp.concatenate([p, p], axis=1).reshape(1, CMP_STRIDE * LANES)
        return jnp.broadcast_to(r, (8, CMP_STRIDE * LANES)).astype(BF16)

    zpad = jnp.zeros((CMP_HIDDEN, DH), F32)
    w2 = jnp.concatenate([jnp.concatenate([wk2, zpad], axis=1),
                          jnp.concatenate([zpad, wv2], axis=1)], axis=0).astype(BF16)
    b1 = jnp.concatenate([bk1, bv1]).reshape(1, -1)
    return (pos_row(pos[:CMP_STRIDE]), pos_row(pos[CMP_STRIDE:]), rows(wk1[:half], wv1[:half]),
            rows(wk1[half:], wv1[half:]), b1, w2)


def _heads_on_lanes(t4):
    return jnp.concatenate([t4[r] for r in range(NSA_R)], axis=1)


def _select_kernel(q_ref, kvw_ref, vwat_ref, kvc_ref, kvct_ref, tw_ref, bc_ref, sgt_ref,
                   qa_ref, part_ref, gsel_ref, ps_ref):
    g = pl.program_id(1)

    def front(u):
        i = pl.program_id(2) * SELECT_BLOCKS + u
        t0 = i * QB
        qt = q_ref[0, :, u * QB:(u + 1) * QB].astype(F32)
        q_t = jnp.concatenate([qt[r * DH:(r + 1) * DH] for r in range(NSA_R)], axis=1)
        qpad_t = jnp.concatenate([q_t, jnp.zeros((DH, ROWS), F32)], axis=0).astype(BF16)

        bc = _heads_on_lanes(bc_ref[u])
        s = _dot(kvc_ref[0, 0], qpad_t) + bc
        m = jnp.max(s, axis=0, keepdims=True)
        p = jnp.exp(s - m)
        l = jnp.sum(p, axis=0, keepdims=True)
        pn = p * jnp.where(m > 0.5 * NEG, 1.0 / l, 0.0)
        o_c = _dot(kvct_ref[0, 0], pn.astype(BF16))[DH:2 * DH]

        wlen = WINDOW + QB
        start = pl.multiple_of(jnp.maximum(t0 - WINDOW, 0), LANES)
        offw = pl.multiple_of(start - t0 + WINDOW, LANES)
        sw = _dot(kvw_ref[0, 0, pl.ds(start, wlen), :], qpad_t) + _heads_on_lanes(tw_ref[:, pl.ds(offw, wlen), :])
        mw = jnp.max(sw, axis=0, keepdims=True)
        pw = jnp.exp((sw - mw).astype(BF16))
        acc_w = _dot(vwat_ref[0, 0, :, pl.ds(start, wlen)], pw)
        o_w = acc_w[0:DH] / acc_w[DH:DH + 1]

        sg = sgt_ref[0, :, u * QB:(u + 1) * QB]

        def gate_row(c):
            rows = []
            for r in range(NSA_R):
                c0 = MISC_GATES + 3 * r + c
                c1 = c0 + 3 * NSA_R
                rows.append(jnp.where(g == 0, sg[c0:c0 + 1], sg[c1:c1 + 1]))
            return jnp.concatenate(rows, axis=1)
        partial = gate_row(0) * o_c + gate_row(2) * o_w
        gate_sel = gate_row(1)

        nc = pn.shape[0]
        nsel = nc * CMP_STRIDE // SEL_BLOCK
        per_blk = SEL_BLOCK // CMP_STRIDE
        psum = pn[:, 0:QB] + pn[:, QB:2 * QB] + pn[:, 2 * QB:3 * QB] + pn[:, 3 * QB:4 * QB]
        for c in range(QB // LANES):
            ps_ref[u, c, 0:8] = jnp.zeros((8, LANES), F32)
            ps_ref[u, c, 8:8 + nc] = psum[:, c * LANES:(c + 1) * LANES]
        tok = lambda k: jnp.concatenate(
            [ps_ref.at[u].at[c][pl.ds(8 + k, nsel, stride=per_blk), :] for c in range(QB // LANES)], axis=1)
        imp = 0.5 * (tok(-1) + tok(3)) + tok(0) + tok(1) + tok(2)
        nidx = lax.broadcasted_iota(jnp.int32, (nsel, QB), 0)
        qidx = lax.broadcasted_iota(jnp.int32, (nsel, QB), 1)
        qblk = (t0 + qidx) // SEL_BLOCK
        causal = nidx <= qblk
        forced = causal & ((nidx == 0) | (nidx >= qblk - (N_LOCAL - 1)))
        n_forced = N_LOCAL + 1
        w = jnp.where(causal & jnp.logical_not(forced), imp, -3e38)
        nf = nidx.astype(F32)
        sel = jnp.where(forced | (causal & (qblk < TOP_N)), 1.0, 0.0)
        for _ in range(min(TOP_N - n_forced, nsel)):
            mx = jnp.max(w, axis=0, keepdims=True)
            first = jnp.min(jnp.where(w == mx, nf, float(nsel)), axis=0, keepdims=True)
            pick = nf == first
            sel = jnp.where(pick, 1.0, sel)
            w = jnp.where(pick, -3e38, w)
        sel_bias = jnp.where((sel > 0.5) & causal, 0.0, NEG)

        for sup in range(qa_ref.shape[3]):
            sb = sel_bias[sup * 64:(sup + 1) * 64]
            if sb.shape[0] < 64:
                sb = jnp.concatenate([sb, jnp.full((64 - sb.shape[0], QB), NEG, F32)], axis=0)
            sb4 = jnp.concatenate([sb] * NSA_R, axis=1)
            qa_ref[0, 0, u, sup] = jnp.concatenate([q_t, sb4], axis=0).astype(BF16)
        part_ref[0, 0, u] = partial
        gsel_ref[0, 0, u] = gate_sel

    for u in range(SELECT_BLOCKS):
        front(u)


def _select(qn_t, kvw, vwat, kvc, kvct, bias_w, bias_c, sg_t):
    B, _, S = qn_t.shape
    nb = S // QB
    nc = kvc.shape[2]
    n_super = -(-S // SUPER)
    nblk = SELECT_BLOCKS
    seq = lambda a: pl.BlockSpec((1, 1) + a.shape[2:], lambda b, g, t: (b, g, 0, 0))
    per_blk = lambda *tail: pl.BlockSpec((1, 1, nblk) + tail, lambda b, g, t: (b, g, t) + (0,) * len(tail))
    sds = jax.ShapeDtypeStruct
    return pl.pallas_call(
        _select_kernel,
        grid=(B, NSA_G, nb // nblk),
        in_specs=[
            pl.BlockSpec((1, NSA_R * DH, nblk * QB), lambda b, g, t: (b, g, t)),
            seq(kvw), seq(vwat), seq(kvc), seq(kvct),
            pl.BlockSpec((NSA_R, TW_WIDTH, QB), lambda b, g, t: (g, 0, 0)),
            pl.BlockSpec((nblk, NSA_R, nc, QB), lambda b, g, t: (t, g, 0, 0)),
            pl.BlockSpec((1, GATE_ROWS, nblk * QB), lambda b, g, t: (b, 0, t)),
        ],
        out_specs=[per_blk(n_super, LANES, ROWS), per_blk(DH, ROWS), per_blk(1, ROWS)],
        out_shape=[sds((B, NSA_G, nb, n_super, LANES, ROWS), BF16),
                   sds((B, NSA_G, nb, DH, ROWS), F32),
                   sds((B, NSA_G, nb, 1, ROWS), F32)],
        scratch_shapes=[pltpu.VMEM((nblk, QB // LANES, nc + 8, LANES), F32)],
        compiler_params=pltpu.CompilerParams(dimension_semantics=("parallel", "parallel", "parallel"),
                                             vmem_limit_bytes=VMEM_LIMIT),
        name="nsa_select",
    )(qn_t, kvw, vwat, kvc, kvct, bias_w, bias_c, sg_t)


def _sweep_kernel(qa_ref, part_ref, gsel_ref, ksa_ref, vsat_ref, tw_ref, o_ref, m_ref, acc_ref, s_ref):
    i = pl.program_id(2)
    t0 = i * QB
    jd1 = (t0 + QB - 1) // SEL_TILE
    n_far = jnp.maximum(t0 - (_THR[NUM_BUCKETS - 1] - 1), 0) // SEL_TILE

    def score_tile(j, ntiles, bias_off):
        n = ntiles * SEL_TILE
        k0 = pl.multiple_of(j * SEL_TILE, SEL_TILE)
        sc = _dot(ksa_ref[0, 0, pl.ds(k0, n), :], qa_ref[0, 0, 0, j // (SUPER // SEL_TILE)])
        if bias_off is not None:
            off = pl.multiple_of(bias_off, LANES)
            sc = sc + _heads_on_lanes(tw_ref[:, pl.ds(off, n), :])
        s_ref[pl.ds(k0, n), :] = sc.astype(s_ref.dtype)
        return jnp.max(sc, axis=0, keepdims=True)

    def pv_tile(j, ntiles):
        n = ntiles * SEL_TILE
        k0 = pl.multiple_of(j * SEL_TILE, SEL_TILE)
        pt = jnp.exp(s_ref[pl.ds(k0, n), :] - m_ref[...].astype(s_ref.dtype))
        return _dot(vsat_ref[0, 0, :, pl.ds(k0, n)], pt)

    def sweep(total, block):
        n_grp = total // SWEEP_GROUP

        def body(jj, carry):
            block(SWEEP_GROUP * jj, SWEEP_GROUP)
            return carry
        lax.fori_loop(0, n_grp, body, 0)
        size = SWEEP_GROUP // 2
        while size >= 1:
            first = (total // (2 * size)) * (2 * size)

            @pl.when((total // size) % 2 == 1)
            def _(first=first, size=size):
                block(first, size)
            size //= 2

    def score_block(j, ntiles):
        mm = m_ref[...]
        for h in range(0, ntiles, 2):
            mm = jnp.maximum(mm, score_tile(j + h, min(2, ntiles - h), None))
        m_ref[...] = mm

    def pv_block(j, ntiles):
        part = None
        for h in range(0, ntiles, 2):
            pv = pv_tile(j + h, min(2, ntiles - h))
            part = pv if part is None else part + pv
        acc_ref[...] += part

    m_ref[...] = jnp.full(m_ref.shape, NEG, F32)
    sweep(n_far, score_block)
    m_lo = score_tile(n_far, 1, WINDOW - (t0 - n_far * SEL_TILE))
    m_hi = score_tile(jd1, 1, WINDOW - (t0 - jd1 * SEL_TILE))
    m_ref[...] = jnp.maximum(m_ref[...], jnp.maximum(m_lo, m_hi))

    acc_ref[...] = jnp.zeros(acc_ref.shape, F32)
    sweep(jd1 + 1, pv_block)
    acc = acc_ref[...]
    o_s = acc[0:DH] / acc[DH:DH + 1]
    out_t = part_ref[0, 0, 0] + gsel_ref[0, 0, 0] * o_s
    out_t = jnp.concatenate([out_t[:, r * QB:(r + 1) * QB] for r in range(NSA_R)], axis=0)
    o_ref[0] = out_t.astype(o_ref.dtype)


def _sweep(qa, part, gsel, ksa, vsat, bias_w):
    B, G, nb = qa.shape[:3]
    S = ksa.shape[2]
    seq = lambda a: pl.BlockSpec((1, 1) + a.shape[2:], lambda b, g, i: (b, g, 0, 0))
    blk = lambda a: pl.BlockSpec((1, 1, 1) + a.shape[3:], lambda b, g, i: (b, g, i) + (0,) * (a.ndim - 3))
    return pl.pallas_call(
        _sweep_kernel,
        grid=(B, G, nb),
        in_specs=[blk(qa), blk(part), blk(gsel), seq(ksa), seq(vsat),
                  pl.BlockSpec((NSA_R, TW_WIDTH, QB), lambda b, g, i: (g, 0, 0))],
        out_specs=pl.BlockSpec((1, NSA_R * DH, QB), lambda b, g, i: (b, g, i)),
        out_shape=jax.ShapeDtypeStruct((B, NSA_HEADS * DH, S), BF16),
        scratch_shapes=[pltpu.VMEM((1, ROWS), F32),
                        pltpu.VMEM((VROWS, ROWS), F32),
                        pltpu.VMEM((S, ROWS), BF16)],
        compiler_params=pltpu.CompilerParams(dimension_semantics=("parallel", "parallel", "arbitrary"),
                                             vmem_limit_bytes=VMEM_LIMIT),
        name="nsa_sweep",
    )(qa, part, gsel, ksa, vsat, bias_w)


def _gla_kernel(qk_ref, v_ref, z_ref, misc_ref, wa_ref, ba_ref, nw_ref, tri_ref, o_ref, st_ref):
    t = pl.program_id(2)
    T = qk_ref.shape[1]
    C = GLA_CHUNK

    @pl.when(t == 0)
    def _():
        st_ref[...] = jnp.zeros(st_ref.shape, F32)

    lane = lax.broadcasted_iota(jnp.int32, (1, LANES), 1)
    sign = jnp.where(lane < GLA_DK, 1.0 / GLA_TAU, -1.0 / GLA_TAU)
    zz = _dot(misc_ref[0].astype(BF16), wa_ref[0]) + ba_ref[0]
    log_sig = jnp.minimum(zz, 0.0) - jnp.log(1.0 + jnp.exp(-jnp.abs(zz)))
    la2 = log_sig * sign
    rr = lax.broadcasted_iota(jnp.int32, (C, C), 0)
    cc = lax.broadcasted_iota(jnp.int32, (C, C), 1)
    tril = rr >= cc
    scale = GLA_DK ** -0.5

    chunks = [slice(n * C, (n + 1) * C) for n in range(T // C)]
    hi = la2.astype(BF16)
    rem = la2 - hi.astype(F32)
    mid = rem.astype(BF16)
    lo = (rem - mid.astype(F32)).astype(BF16)
    tri = tri_ref[...]
    b2 = jnp.concatenate([_dot(tri, hi[sl]) + _dot(tri, mid[sl]) + _dot(tri, lo[sl]) for sl in chunks], axis=0)
    e = jnp.exp(b2)
    qke = qk_ref[0] * e
    q_e = (qke[:, 0:GLA_DK] * scale).astype(BF16)
    k_e = qke[:, GLA_DK:2 * GLA_DK].astype(BF16)
    v = v_ref[0].astype(BF16)
    attn = [jnp.where(tril, _dot_nt(q_e[sl], k_e[sl]), 0.0).astype(BF16) for sl in chunks]
    o_intra = [_dot(a, v[sl]) for a, sl in zip(attn, chunks)]
    kv = [_dot_tn(v[sl], k_e[sl]) for sl in chunks]
    st = st_ref[...]
    states = []
    for n, sl in enumerate(chunks):
        states.append(st.astype(BF16))
        st = (st + kv[n]) * e[sl.stop - 1:sl.stop, 0:GLA_DK]
    st_ref[...] = st
    o = jnp.concatenate([oi + _dot_nt(q_e[sl], s_in) for oi, sl, s_in in zip(o_intra, chunks, states)], axis=0)
    y = o * lax.rsqrt(jnp.mean(o * o, axis=-1, keepdims=True) + EPS) * nw_ref[...]
    o_ref[0] = (y * _silu(z_ref[0])).astype(o_ref.dtype)


def _gla(qkg, vg, zg, misc, wa_big, ba2, norm_w, tri, T):
    B, S, _ = qkg.shape
    hb = pl.BlockSpec((1, T, LANES), lambda b, h, t: (b, t, h))
    return pl.pallas_call(
        _gla_kernel,
        grid=(B, GLA_HEADS, S // T),
        in_specs=[hb, hb, hb,
                  pl.BlockSpec((1, T, LANES), lambda b, h, t: (b, t, 0)),
                  pl.BlockSpec((1, LANES, LANES), lambda b, h, t: (h, 0, 0)),
                  pl.BlockSpec((1, 1, LANES), lambda b, h, t: (h, 0, 0)),
                  pl.BlockSpec(norm_w.shape, lambda b, h, t: (0, 0)),
                  pl.BlockSpec(tri.shape, lambda b, h, t: (0, 0))],
        out_specs=hb,
        out_shape=jax.ShapeDtypeStruct((B, S, GLA_HEADS * GLA_DV), BF16),
        scratch_shapes=[pltpu.VMEM((GLA_DV, GLA_DK), F32)],
        compiler_params=pltpu.CompilerParams(dimension_semantics=("parallel", "parallel", "arbitrary"),
                                             vmem_limit_bytes=VMEM_LIMIT),
        name="gla_chunked",
    )(qkg, vg, zg, misc, wa_big, ba2, norm_w, tri)


def _out_kernel(hn_ref, on_ref, zs_ref, og_ref, p_ref, wo_ref, wpg_ref, bpg_ref, wpe_ref, lg_ref, lb_ref, o_ref):
    sub = 256
    for c in range(hn_ref.shape[1] // sub):
        rows = slice(c * sub, (c + 1) * sub)
        o_nsa = on_ref[0, :, rows].astype(F32).T.astype(BF16)
        mix_n = o_nsa * zs_ref[0, rows, :]
        mix = jnp.concatenate([mix_n, og_ref[0, rows, :]], axis=1)
        r = hn_ref[0, rows, :] + _dot(mix, wo_ref[...])
        gate = _sigmoid(_dot(r.astype(BF16), wpg_ref[...]) + bpg_ref[...])
        r = r + gate * _dot(p_ref[0, rows, :].astype(BF16), wpe_ref[...])
        o_ref[0, rows, :] = _layer_norm(r, lg_ref[...], lb_ref[...])


def _out_projection(hn, o_nsa, zs, o_gla, p, w_out, w_pg, b_pg, w_pe, ln_g, ln_b, tm):
    B, S, D = hn.shape
    tok = lambda w: pl.BlockSpec((1, tm, w), lambda b, t: (b, t, 0))
    full2 = lambda a: pl.BlockSpec(a.shape, lambda b, t: (0, 0))
    return pl.pallas_call(
        _out_kernel,
        grid=(B, S // tm),
        in_specs=[tok(D), pl.BlockSpec((1, 512, tm), lambda b, t: (b, 0, t)), tok(512), tok(512), tok(PLE_DIM),
                  full2(w_out), full2(w_pg), full2(b_pg), full2(w_pe), full2(ln_g), full2(ln_b)],
        out_specs=tok(D),
        out_shape=jax.ShapeDtypeStruct((B, S, D), F32),
        compiler_params=pltpu.CompilerParams(dimension_semantics=("parallel", "parallel"),
                                             vmem_limit_bytes=VMEM_LIMIT),
        name="out_proj_deepnorm",
    )(hn, o_nsa, zs, o_gla, p, w_out, w_pg, b_pg, w_pe, ln_g, ln_b)


def _permute_w_in(w):
    widths = (512, 128, 128, 128, 128, 128, 128, 24, 512, 256, 256, 512, 16, 512)
    offs = np.concatenate([[0], np.cumsum(widths)])
    (q_n, kc, vc, ks, vs, kw, vw, gates, z_n, q_g, k_g, v_g, a_low, z_g) = [
        w[:, int(offs[k]):int(offs[k + 1])] for k in range(len(widths))]
    pair = lambda a, b: [jnp.concatenate([a[:, g * DH:(g + 1) * DH], b[:, g * DH:(g + 1) * DH]], axis=1)
                         for g in range(NSA_G)]
    kvc = pair(kc, vc)
    kvw = pair(kw, vw)
    qk = jnp.concatenate([jnp.concatenate([q_g[:, h * GLA_DK:(h + 1) * GLA_DK],
                                           k_g[:, h * GLA_DK:(h + 1) * GLA_DK]], axis=1)
                          for h in range(GLA_HEADS)], axis=1)
    misc = jnp.concatenate([gates, a_low,
                            jnp.zeros((w.shape[0], LANES - gates.shape[1] - a_low.shape[1]), w.dtype)], axis=1)
    cols = [q_n * (DH ** -0.5), kvc[0], kvc[1], ks, vs, kvw[0], kvw[1], z_n, qk, v_g, z_g, misc]
    return jnp.concatenate(cols, axis=1).astype(BF16)


def kernel(x, p, ln0_g, ln0_b, rel_bias, w_in, w_a2, b_a, gla_norm_w, pos_cmp, w_ck1, b_ck1, w_ck2,
           w_cv1, b_cv1, w_cv2, w_out, w_pe, w_pg, b_pg, ln_g, ln_b):
    B, S, D = x.shape
    assert D == D_MODEL and S % 512 == 0 and S >= WINDOW + QB and w_in.shape[0] == DEPTH == 1
    nb = S // QB
    nch = S // CMP_STRIDE
    row = lambda a: a.reshape(1, -1)

    bias_c, bias_w = _bias_tables(rel_bias, nb, nch)

    w_perm = _permute_w_in(w_in[0])
    (hn, qn_t, xkv, ksa, vsat, kvw, vwat, zs, qkg, vg, zg, misc, sg_t) = _in_projection(
        x, row(ln0_g), row(ln0_b), w_perm, TOKEN_TILE)

    kvc, kvct = _compress(xkv, *_compress_params(pos_cmp[0], w_ck1[0], b_ck1[0], w_ck2[0],
                                                 w_cv1[0], b_cv1[0], w_cv2[0]))

    qa, part, gsel = _select(qn_t, kvw, vwat, kvc, kvct, bias_w, bias_c, sg_t)
    o_nsa = _sweep(qa, part, gsel, ksa, vsat, bias_w)

    wa = w_a2[0]
    wa_big = jnp.zeros((GLA_HEADS, LANES, LANES), F32)
    for h in range(GLA_HEADS):
        wh = wa[:, h * GLA_DK:(h + 1) * GLA_DK]
        wa_big = wa_big.at[h, MISC_ALOW:MISC_ALOW + GLA_RANK, :].set(jnp.concatenate([wh, wh], axis=1))
    ba = b_a[0].reshape(GLA_HEADS, 1, GLA_DK)
    ba2 = jnp.concatenate([ba, ba], axis=2)
    tri = jnp.asarray(np.tril(np.ones((GLA_CHUNK, GLA_CHUNK), np.float32))).astype(BF16)
    gla_tile = math.gcd(S, GLA_TILE)
    o_gla = _gla(qkg, vg, zg, misc, wa_big.astype(BF16), ba2, row(gla_norm_w[0]), tri, gla_tile)

    return _out_projection(hn, o_nsa, zs, o_gla, p[0], w_out[0].astype(BF16), w_pg[0].astype(BF16),
                           row(b_pg[0]), w_pe[0].astype(BF16), row(ln_g[0]), row(ln_b[0]), TOKEN_TILE)
```

```python
import math

import numpy as np
import jax
import jax.numpy as jnp
from jax import lax
from jax.experimental import pallas as pl
from jax.experimental.pallas import tpu as pltpu

F32 = jnp.float32
BF16 = jnp.bfloat16

D_MODEL = 1024
PLE_DIM = 256
NSA_HEADS = 8
NSA_G = 2
NSA_R = NSA_HEADS // NSA_G
DH = 64
CMP_LEN = 32
CMP_STRIDE = 16
CMP_HIDDEN = 256
SEL_BLOCK = 64
TOP_N = 16
N_LOCAL = 2
WINDOW = 512
QB = 256
GLA_HEADS = 4
GLA_DK = 64
GLA_DV = 128
GLA_RANK = 16
GLA_TAU = 16.0
GLA_CHUNK = 64
NUM_BUCKETS = 32
MAX_DISTANCE = 128
DEPTH = 1
ALPHA = (2.0 * DEPTH) ** 0.25
EPS = 1e-5
NEG = -1e30
POS = 1e30

LANES = 128
VMEM_LIMIT = 56 * 1024 * 1024

TOKEN_TILE = 512
GLA_TILE = 4096
SELECT_BLOCKS = 4
SEL_TILE = 256
SWEEP_GROUP = 16
VROWS = 80
SUPER = 64 * SEL_BLOCK
TW_WIDTH = WINDOW + QB + WINDOW
ROWS = NSA_R * QB

C_QN, C_KVC, C_KSVS, C_KVW, C_ZN, C_QKG, C_VG, C_ZG, C_MISC, C_END = (
    0, 512, 768, 1024, 1280, 1792, 2304, 2816, 3328, 3456)
MISC_GATES = 0
MISC_ALOW = 24
GATE_ROWS = 32


def _bucket_thresholds():
    d = np.arange(0, 4 * MAX_DISTANCE)
    max_exact = NUM_BUCKETS // 2
    nf = np.maximum(d, 1).astype(np.float32)
    large = max_exact + (np.log(nf / np.float32(max_exact)) / np.float32(math.log(MAX_DISTANCE / max_exact))
                         * np.float32(NUM_BUCKETS - max_exact)).astype(np.int32)
    large = np.minimum(large, NUM_BUCKETS - 1)
    bucket = np.where(d < max_exact, d, large)
    assert np.all(np.diff(bucket) >= 0) and bucket[-1] == NUM_BUCKETS - 1
    return [int(np.argmax(bucket >= k)) for k in range(NUM_BUCKETS)]


_THR = _bucket_thresholds()
CMP_BAND = 32
assert (QB - 1 + _THR[NUM_BUCKETS - 1]) // CMP_STRIDE + 1 + 7 <= CMP_BAND
assert SEL_TILE % QB == 0 and _THR[NUM_BUCKETS - 1] <= QB


def _dot(a, b, **kw):
    return jnp.dot(a, b, preferred_element_type=F32, **kw)


def _dot_nt(a, b, **kw):
    return lax.dot_general(a, b, (((1,), (1,)), ((), ())), preferred_element_type=F32, **kw)


def _dot_tn(a, b, **kw):
    return lax.dot_general(a, b, (((0,), (0,)), ((), ())), preferred_element_type=F32, **kw)


def _layer_norm(x, g, b):
    mu = jnp.mean(x, axis=-1, keepdims=True)
    xc = x - mu
    var = jnp.mean(xc * xc, axis=-1, keepdims=True)
    return xc * lax.rsqrt(var + EPS) * g + b


def _sigmoid(x):
    return 1.0 / (1.0 + jnp.exp(-x))


def _silu(x):
    return x * _sigmoid(x)


def _bias_from_dist(rb_ref, dist, valid, put):
    masks = [dist >= _THR[k] for k in range(1, NUM_BUCKETS)]
    for h in range(NSA_HEADS):
        val = jnp.full(dist.shape, rb_ref[0, h], F32)
        for k in range(1, NUM_BUCKETS):
            val = jnp.where(masks[k - 1], rb_ref[k, h], val)
        val = val - rb_ref[NUM_BUCKETS - 1, h]
        put(h, jnp.where(valid, val, NEG))


def _bias_cmp_kernel(rb_ref, out_ref):
    i = pl.program_id(0)
    nc = out_ref.shape[2]
    big = CMP_STRIDE * QB
    j_zero_max = (QB * i - (CMP_LEN - 1) - _THR[NUM_BUCKETS - 1] + CMP_STRIDE * big) // CMP_STRIDE - big
    start = jnp.clip((j_zero_max + 1 + 8 * big) // 8 * 8 - 8 * big, 0, nc - CMP_BAND)
    start = pl.multiple_of(start, 8)
    j_all = lax.broadcasted_iota(jnp.int32, (nc, QB), 0)
    fill = jnp.where(j_all <= j_zero_max, 0.0, NEG)
    for h in range(NSA_HEADS):
        out_ref[0, h] = fill
    j = start + lax.broadcasted_iota(jnp.int32, (CMP_BAND, QB), 0)
    a = lax.broadcasted_iota(jnp.int32, (CMP_BAND, QB), 1)
    dist = a + QB * i - CMP_STRIDE * j - (CMP_LEN - 1)

    def put(h, v):
        out_ref[0, h, pl.ds(start, CMP_BAND), :] = v
    _bias_from_dist(rb_ref, dist, dist >= 0, put)


def _bias_win_kernel(rb_ref, out_ref):
    shape = out_ref.shape[1:]
    y = lax.broadcasted_iota(jnp.int32, shape, 0)
    a = lax.broadcasted_iota(jnp.int32, shape, 1)
    dist = a + WINDOW - y

    def put(h, v):
        out_ref[h] = v
    _bias_from_dist(rb_ref, dist, (dist >= 0) & (dist < WINDOW), put)


def _bias_tables(rel_bias, nb, nc):
    smem = pl.BlockSpec(memory_space=pltpu.SMEM)
    bias_c = pl.pallas_call(
        _bias_cmp_kernel,
        grid=(nb,),
        in_specs=[smem],
        out_specs=pl.BlockSpec((1, NSA_HEADS, nc, QB), lambda i: (i, 0, 0, 0)),
        out_shape=jax.ShapeDtypeStruct((nb, NSA_HEADS, nc, QB), F32),
        compiler_params=pltpu.CompilerParams(dimension_semantics=("parallel",)),
        name="bias_cmp",
    )(rel_bias)
    bias_w = pl.pallas_call(
        _bias_win_kernel,
        in_specs=[smem],
        out_shape=jax.ShapeDtypeStruct((NSA_HEADS, TW_WIDTH, QB), F32),
        name="bias_win",
    )(rel_bias)
    return bias_c, bias_w


def _inproj_kernel(x_ref, g_ref, b_ref, w_ref, hn_ref, qn_ref, xkv_ref, ksa_ref, vsat_ref, kvw_ref, vwat_ref,
                   zs_ref, qkg_ref, vg_ref, zg_ref, misc_ref, sgt_ref, slab_ref):
    t = pl.program_id(1)
    tm = x_ref.shape[1]
    hf = _layer_norm(x_ref[0], g_ref[...], b_ref[...])
    hn_ref[0] = ALPHA * hf
    h = hf.astype(BF16)

    def mm(c0, c1):
        return _dot(h, w_ref[:, c0:c1])

    qn_ref[0] = mm(C_QN, C_KVC).T.astype(BF16)

    kvc = mm(C_KVC, C_KSVS)
    for g in range(NSA_G):
        slab_ref[...] = kvc[:, g * LANES:(g + 1) * LANES]
        for tok in range(CMP_STRIDE):
            rows = slab_ref[pl.ds(tok, tm // CMP_STRIDE, stride=CMP_STRIDE), :]
            xkv_ref[0, g, :, tok * LANES:(tok + 1) * LANES] = rows.astype(BF16)

    lane = lax.broadcasted_iota(jnp.int32, (tm, LANES), 1)
    row = lax.broadcasted_iota(jnp.int32, (tm, LANES), 0)
    low = lane < DH
    blk = ((t * tm + row) // SEL_BLOCK) % (SUPER // SEL_BLOCK)
    onehot = jnp.where(lane - DH == blk, 1.0, 0.0)
    ones_col = jnp.where(lane == DH, 1.0, 0.0)

    def values_t(v_low):
        return jnp.where(low, v_low, ones_col).T[0:VROWS].astype(BF16)

    ksvs = mm(C_KSVS, C_KVW)
    ks, vs = ksvs[:, 0:LANES], ksvs[:, LANES:2 * LANES]
    ksa_ref[0, 0] = jnp.where(low, ks, onehot).astype(BF16)
    ksa_ref[0, 1] = jnp.where(low, pltpu.roll(ks, DH, 1), onehot).astype(BF16)
    vsat_ref[0, 0] = values_t(vs)
    vsat_ref[0, 1] = values_t(pltpu.roll(vs, DH, 1))
    kvw = mm(C_KVW, C_ZN)
    for g in range(NSA_G):
        kvw_g = kvw[:, g * LANES:(g + 1) * LANES]
        kvw_ref[0, g] = kvw_g.astype(BF16)
        vwat_ref[0, g] = values_t(pltpu.roll(kvw_g, DH, 1))
    zs_ref[0] = _silu(mm(C_ZN, C_QKG)).astype(BF16)
    qkg_ref[0] = mm(C_QKG, C_VG)
    vg_ref[0] = mm(C_VG, C_ZG)
    zg_ref[0] = mm(C_ZG, C_MISC)
    misc = mm(C_MISC, C_END)
    misc_ref[0] = misc
    sgt_ref[0] = _sigmoid(misc).T[0:GATE_ROWS]


def _in_projection(x, ln0_g, ln0_b, w_perm, tm):
    B, S, D = x.shape
    tok = lambda w: pl.BlockSpec((1, tm, w), lambda b, t: (b, t, 0))
    tok_t = lambda w: pl.BlockSpec((1, w, tm), lambda b, t: (b, 0, t))
    grp = pl.BlockSpec((1, NSA_G, tm, LANES), lambda b, t: (b, 0, t, 0))
    grp_v = pl.BlockSpec((1, NSA_G, VROWS, tm), lambda b, t: (b, 0, 0, t))
    grp_x = pl.BlockSpec((1, NSA_G, tm // CMP_STRIDE, CMP_STRIDE * LANES), lambda b, t: (b, 0, t, 0))
    full2 = lambda a: pl.BlockSpec(a.shape, lambda b, t: (0, 0))
    sds = jax.ShapeDtypeStruct
    return pl.pallas_call(
        _inproj_kernel,
        grid=(B, S // tm),
        in_specs=[tok(D), full2(ln0_g), full2(ln0_b), full2(w_perm)],
        out_specs=[tok(D), tok_t(512), grp_x, grp, grp_v, grp, grp_v, tok(512), tok(512), tok(512), tok(512),
                   tok(LANES), tok_t(GATE_ROWS)],
        out_shape=[sds((B, S, D), F32), sds((B, 512, S), BF16),
                   sds((B, NSA_G, S // CMP_STRIDE, CMP_STRIDE * LANES), BF16),
                   sds((B, NSA_G, S, LANES), BF16), sds((B, NSA_G, VROWS, S), BF16),
                   sds((B, NSA_G, S, LANES), BF16), sds((B, NSA_G, VROWS, S), BF16),
                   sds((B, S, 512), BF16), sds((B, S, 512), F32), sds((B, S, 512), F32), sds((B, S, 512), F32),
                   sds((B, S, LANES), F32), sds((B, GATE_ROWS, S), F32)],
        scratch_shapes=[pltpu.VMEM((tm, LANES), F32)],
        compiler_params=pltpu.CompilerParams(dimension_semantics=("parallel", "parallel"),
                                             vmem_limit_bytes=VMEM_LIMIT),
        name="ln0_inproj",
    )(x, ln0_g, ln0_b, w_perm)


def _gelu_tanh(x):
    c = math.sqrt(2.0 / math.pi)
    return x * (0.5 * (1.0 + jnp.tanh(c * (x + 0.044715 * (x * x * x)))))


def _compress_kernel(x_ref, ptop_ref, pbot_ref, wtop_ref, wbot_ref, b1_ref, w2_ref, out_ref, outt_ref):
    nch = x_ref.shape[2]
    x = x_ref[0, 0]
    top = _dot(x, wtop_ref[...])
    bot = _dot(x, wbot_ref[...])
    cpos = (_dot(ptop_ref[...], wtop_ref[...]) + _dot(pbot_ref[...], wbot_ref[...]))[0:1, :]
    pre = top + pltpu.roll(bot, nch - 1, 0) + cpos + b1_ref[...]
    out = _dot(_gelu_tanh(pre).astype(BF16), w2_ref[...])
    out_ref[0, 0] = out.astype(BF16)
    outt_ref[0, 0] = out.T.astype(BF16)


def _compress(xkv, ptop, pbot, wtop, wbot, b1, w2):
    B, G, nch, width = xkv.shape
    full2 = lambda a: pl.BlockSpec(a.shape, lambda b, g: (0, 0))
    return pl.pallas_call(
        _compress_kernel,
        grid=(B, G),
        in_specs=[pl.BlockSpec((1, 1, nch, width), lambda b, g: (b, g, 0, 0)),
                  full2(ptop), full2(pbot), full2(wtop), full2(wbot), full2(b1), full2(w2)],
        out_specs=[pl.BlockSpec((1, 1, nch, LANES), lambda b, g: (b, g, 0, 0)),
                   pl.BlockSpec((1, 1, LANES, nch), lambda b, g: (b, g, 0, 0))],
        out_shape=[jax.ShapeDtypeStruct((B, G, nch, LANES), BF16),
                   jax.ShapeDtypeStruct((B, G, LANES, nch), BF16)],
        compiler_params=pltpu.CompilerParams(dimension_semantics=("parallel", "parallel"),
                                             vmem_limit_bytes=VMEM_LIMIT),
        name="kv_compress",
    )(xkv, ptop, pbot, wtop, wbot, b1, w2)


def _compress_params(pos, wk1, bk1, wk2, wv1, bv1, wv2):
    half = CMP_STRIDE * DH
    zero = jnp.zeros((CMP_STRIDE, DH, CMP_HIDDEN), F32)

    def rows(w_k, w_v):
        k_rows = jnp.concatenate([w_k.reshape(CMP_STRIDE, DH, CMP_HIDDEN), zero], axis=2)
        v_rows = jnp.concatenate([zero, w_v.reshape(CMP_STRIDE, DH, CMP_HIDDEN)], axis=2)
        return jnp.concatenate([k_rows, v_rows], axis=1).reshape(CMP_STRIDE * LANES, 2 * CMP_HIDDEN).astype(BF16)

    def pos_row(p):
        r = jnp.concatenate([p, p], axis=1).reshape(1, CMP_STRIDE * LANES)
        return jnp.broadcast_to(r, (8, CMP_STRIDE * LANES)).astype(BF16)

    zpad = jnp.zeros((CMP_HIDDEN, DH), F32)
    w2 = jnp.concatenate([jnp.concatenate([wk2, zpad], axis=1),
                          jnp.concatenate([zpad, wv2], axis=1)], axis=0).astype(BF16)
    b1 = jnp.concatenate([bk1, bv1]).reshape(1, -1)
    return (pos_row(pos[:CMP_STRIDE]), pos_row(pos[CMP_STRIDE:]), rows(wk1[:half], wv1[:half]),
            rows(wk1[half:], wv1[half:]), b1, w2)


def _heads_on_lanes(t4):
    return jnp.concatenate([t4[r] for r in range(NSA_R)], axis=1)


def _select_kernel(q_ref, kvw_ref, vwat_ref, kvc_ref, kvct_ref, tw_ref, bc_ref, sgt_ref,
                   qa_ref, part_ref, gsel_ref, ps_ref):
    g = pl.program_id(1)

    def front(u):
        i = pl.program_id(2) * SELECT_BLOCKS + u
        t0 = i * QB
        qt = q_ref[0, :, u * QB:(u + 1) * QB].astype(F32)
        q_t = jnp.concatenate([qt[r * DH:(r + 1) * DH] for r in range(NSA_R)], axis=1)
        qpad_t = jnp.concatenate([q_t, jnp.zeros((DH, ROWS), F32)], axis=0).astype(BF16)

        bc = _heads_on_lanes(bc_ref[u])
        s = _dot(kvc_ref[0, 0], qpad_t) + bc
        m = jnp.max(s, axis=0, keepdims=True)
        p = jnp.exp(s - m)
        l = jnp.sum(p, axis=0, keepdims=True)
        pn = p * jnp.where(m > 0.5 * NEG, 1.0 / l, 0.0)
        o_c = _dot(kvct_ref[0, 0], pn.astype(BF16))[DH:2 * DH]

        wlen = WINDOW + QB
        start = pl.multiple_of(jnp.maximum(t0 - WINDOW, 0), LANES)
        offw = pl.multiple_of(start - t0 + WINDOW, LANES)
        sw = _dot(kvw_ref[0, 0, pl.ds(start, wlen), :], qpad_t) + _heads_on_lanes(tw_ref[:, pl.ds(offw, wlen), :])
        mw = jnp.max(sw, axis=0, keepdims=True)
        pw = jnp.exp((sw - mw).astype(BF16))
        acc_w = _dot(vwat_ref[0, 0, :, pl.ds(start, wlen)], pw)
        o_w = acc_w[0:DH] / acc_w[DH:DH + 1]

        sg = sgt_ref[0, :, u * QB:(u + 1) * QB]

        def gate_row(c):
            rows = []
            for r in range(NSA_R):
                c0 = MISC_GATES + 3 * r + c
                c1 = c0 + 3 * NSA_R
                rows.append(jnp.where(g == 0, sg[c0:c0 + 1], sg[c1:c1 + 1]))
            return jnp.concatenate(rows, axis=1)
        partial = gate_row(0) * o_c + gate_row(2) * o_w
        gate_sel = gate_row(1)

        nc = pn.shape[0]
        nsel = nc * CMP_STRIDE // SEL_BLOCK
        per_blk = SEL_BLOCK // CMP_STRIDE
        psum = pn[:, 0:QB] + pn[:, QB:2 * QB] + pn[:, 2 * QB:3 * QB] + pn[:, 3 * QB:4 * QB]
        for c in range(QB // LANES):
            ps_ref[u, c, 0:8] = jnp.zeros((8, LANES), F32)
            ps_ref[u, c, 8:8 + nc] = psum[:, c * LANES:(c + 1) * LANES]
        tok = lambda k: jnp.concatenate(
            [ps_ref.at[u].at[c][pl.ds(8 + k, nsel, stride=per_blk), :] for c in range(QB // LANES)], axis=1)
        imp = 0.5 * (tok(-1) + tok(3)) + tok(0) + tok(1) + tok(2)
        nidx = lax.broadcasted_iota(jnp.int32, (nsel, QB), 0)
        qidx = lax.broadcasted_iota(jnp.int32, (nsel, QB), 1)
        qblk = (t0 + qidx) // SEL_BLOCK
        causal = nidx <= qblk
        forced = causal & ((nidx == 0) | (nidx >= qblk - (N_LOCAL - 1)))
        n_forced = N_LOCAL + 1
        w = jnp.where(causal & jnp.logical_not(forced), imp, -3e38)
        nf = nidx.astype(F32)
        sel = jnp.where(forced | (causal & (qblk < TOP_N)), 1.0, 0.0)
        for _ in range(min(TOP_N - n_forced, nsel)):
            mx = jnp.max(w, axis=0, keepdims=True)
            first = jnp.min(jnp.where(w == mx, nf, float(nsel)), axis=0, keepdims=True)
            pick = nf == first
            sel = jnp.where(pick, 1.0, sel)
            w = jnp.where(pick, -3e38, w)
        sel_bias = jnp.where((sel > 0.5) & causal, 0.0, NEG)

        for sup in range(qa_ref.shape[3]):
            sb = sel_bias[sup * 64:(sup + 1) * 64]
            if sb.shape[0] < 64:
                sb = jnp.concatenate([sb, jnp.full((64 - sb.shape[0], QB), NEG, F32)], axis=0)
            sb4 = jnp.concatenate([sb] * NSA_R, axis=1)
            qa_ref[0, 0, u, sup] = jnp.concatenate([q_t, sb4], axis=0).astype(BF16)
        part_ref[0, 0, u] = partial
        gsel_ref[0, 0, u] = gate_sel

    for u in range(SELECT_BLOCKS):
        front(u)


def _select(qn_t, kvw, vwat, kvc, kvct, bias_w, bias_c, sg_t):
    B, _, S = qn_t.shape
    nb = S // QB
    nc = kvc.shape[2]
    n_super = -(-S // SUPER)
    nblk = SELECT_BLOCKS
    seq = lambda a: pl.BlockSpec((1, 1) + a.shape[2:], lambda b, g, t: (b, g, 0, 0))
    per_blk = lambda *tail: pl.BlockSpec((1, 1, nblk) + tail, lambda b, g, t: (b, g, t) + (0,) * len(tail))
    sds = jax.ShapeDtypeStruct
    return pl.pallas_call(
        _select_kernel,
        grid=(B, NSA_G, nb // nblk),
        in_specs=[
            pl.BlockSpec((1, NSA_R * DH, nblk * QB), lambda b, g, t: (b, g, t)),
            seq(kvw), seq(vwat), seq(kvc), seq(kvct),
            pl.BlockSpec((NSA_R, TW_WIDTH, QB), lambda b, g, t: (g, 0, 0)),
            pl.BlockSpec((nblk, NSA_R, nc, QB), lambda b, g, t: (t, g, 0, 0)),
            pl.BlockSpec((1, GATE_ROWS, nblk * QB), lambda b, g, t: (b, 0, t)),
        ],
        out_specs=[per_blk(n_super, LANES, ROWS), per_blk(DH, ROWS), per_blk(1, ROWS)],
        out_shape=[sds((B, NSA_G, nb, n_super, LANES, ROWS), BF16),
                   sds((B, NSA_G, nb, DH, ROWS), F32),
                   sds((B, NSA_G, nb, 1, ROWS), F32)],
        scratch_shapes=[pltpu.VMEM((nblk, QB // LANES, nc + 8, LANES), F32)],
        compiler_params=pltpu.CompilerParams(dimension_semantics=("parallel", "parallel", "parallel"),
                                             vmem_limit_bytes=VMEM_LIMIT),
        name="nsa_select",
    )(qn_t, kvw, vwat, kvc, kvct, bias_w, bias_c, sg_t)


def _sweep_kernel(qa_ref, part_ref, gsel_ref, ksa_ref, vsat_ref, tw_ref, o_ref, m_ref, acc_ref, s_ref):
    i = pl.program_id(2)
    t0 = i * QB
    jd1 = (t0 + QB - 1) // SEL_TILE
    n_far = jnp.maximum(t0 - (_THR[NUM_BUCKETS - 1] - 1), 0) // SEL_TILE

    def score_tile(j, ntiles, bias_off):
        n = ntiles * SEL_TILE
        k0 = pl.multiple_of(j * SEL_TILE, SEL_TILE)
        sc = _dot(ksa_ref[0, 0, pl.ds(k0, n), :], qa_ref[0, 0, 0, j // (SUPER // SEL_TILE)])
        if bias_off is not None:
            off = pl.multiple_of(bias_off, LANES)
            sc = sc + _heads_on_lanes(tw_ref[:, pl.ds(off, n), :])
        s_ref[pl.ds(k0, n), :] = sc.astype(s_ref.dtype)
        return jnp.max(sc, axis=0, keepdims=True)

    def pv_tile(j, ntiles):
        n = ntiles * SEL_TILE
        k0 = pl.multiple_of(j * SEL_TILE, SEL_TILE)
        pt = jnp.exp(s_ref[pl.ds(k0, n), :] - m_ref[...].astype(s_ref.dtype))
        return _dot(vsat_ref[0, 0, :, pl.ds(k0, n)], pt)

    def sweep(total, block):
        n_grp = total // SWEEP_GROUP

        def body(jj, carry):
            block(SWEEP_GROUP * jj, SWEEP_GROUP)
            return carry
        lax.fori_loop(0, n_grp, body, 0)
        size = SWEEP_GROUP // 2
        while size >= 1:
            first = (total // (2 * size)) * (2 * size)

            @pl.when((total // size) % 2 == 1)
            def _(first=first, size=size):
                block(first, size)
            size //= 2

    def score_block(j, ntiles):
        mm = m_ref[...]
        for h in range(0, ntiles, 2):
            mm = jnp.maximum(mm, score_tile(j + h, min(2, ntiles - h), None))
        m_ref[...] = mm

    def pv_block(j, ntiles):
        part = None
        for h in range(0, ntiles, 2):
            pv = pv_tile(j + h, min(2, ntiles - h))
            part = pv if part is None else part + pv
        acc_ref[...] += part

    m_ref[...] = jnp.full(m_ref.shape, NEG, F32)
    sweep(n_far, score_block)
    m_lo = score_tile(n_far, 1, WINDOW - (t0 - n_far * SEL_TILE))
    m_hi = score_tile(jd1, 1, WINDOW - (t0 - jd1 * SEL_TILE))
    m_ref[...] = jnp.maximum(m_ref[...], jnp.maximum(m_lo, m_hi))

    acc_ref[...] = jnp.zeros(acc_ref.shape, F32)
    sweep(jd1 + 1, pv_block)
    acc = acc_ref[...]
    o_s = acc[0:DH] / acc[DH:DH + 1]
    out_t = part_ref[0, 0, 0] + gsel_ref[0, 0, 0] * o_s
    out_t = jnp.concatenate([out_t[:, r * QB:(r + 1) * QB] for r in range(NSA_R)], axis=0)
    o_ref[0] = out_t.astype(o_ref.dtype)


def _sweep(qa, part, gsel, ksa, vsat, bias_w):
    B, G, nb = qa.shape[:3]
    S = ksa.shape[2]
    seq = lambda a: pl.BlockSpec((1, 1) + a.shape[2:], lambda b, g, i: (b, g, 0, 0))
    blk = lambda a: pl.BlockSpec((1, 1, 1) + a.shape[3:], lambda b, g, i: (b, g, i) + (0,) * (a.ndim - 3))
    return pl.pallas_call(
        _sweep_kernel,
        grid=(B, G, nb),
        in_specs=[blk(qa), blk(part), blk(gsel), seq(ksa), seq(vsat),
                  pl.BlockSpec((NSA_R, TW_WIDTH, QB), lambda b, g, i: (g, 0, 0))],
        out_specs=pl.BlockSpec((1, NSA_R * DH, QB), lambda b, g, i: (b, g, i)),
        out_shape=jax.ShapeDtypeStruct((B, NSA_HEADS * DH, S), BF16),
        scratch_shapes=[pltpu.VMEM((1, ROWS), F32),
                        pltpu.VMEM((VROWS, ROWS), F32),
                        pltpu.VMEM((S, ROWS), BF16)],
        compiler_params=pltpu.CompilerParams(dimension_semantics=("parallel", "parallel", "arbitrary"),
                                             vmem_limit_bytes=VMEM_LIMIT),
        name="nsa_sweep",
    )(qa, part, gsel, ksa, vsat, bias_w)


def _gla_kernel(qk_ref, v_ref, z_ref, misc_ref, wa_ref, ba_ref, nw_ref, tri_ref, o_ref, st_ref):
    t = pl.program_id(2)
    T = qk_ref.shape[1]
    C = GLA_CHUNK

    @pl.when(t == 0)
    def _():
        st_ref[...] = jnp.zeros(st_ref.shape, F32)

    lane = lax.broadcasted_iota(jnp.int32, (1, LANES), 1)
    sign = jnp.where(lane < GLA_DK, 1.0 / GLA_TAU, -1.0 / GLA_TAU)
    zz = _dot(misc_ref[0].astype(BF16), wa_ref[0]) + ba_ref[0]
    log_sig = jnp.minimum(zz, 0.0) - jnp.log(1.0 + jnp.exp(-jnp.abs(zz)))
    la2 = log_sig * sign
    rr = lax.broadcasted_iota(jnp.int32, (C, C), 0)
    cc = lax.broadcasted_iota(jnp.int32, (C, C), 1)
    tril = rr >= cc
    scale = GLA_DK ** -0.5

    chunks = [slice(n * C, (n + 1) * C) for n in range(T // C)]
    hi = la2.astype(BF16)
    rem = la2 - hi.astype(F32)
    mid = rem.astype(BF16)
    lo = (rem - mid.astype(F32)).astype(BF16)
    tri = tri_ref[...]
    b2 = jnp.concatenate([_dot(tri, hi[sl]) + _dot(tri, mid[sl]) + _dot(tri, lo[sl]) for sl in chunks], axis=0)
    e = jnp.exp(b2)
    qke = qk_ref[0] * e
    q_e = (qke[:, 0:GLA_DK] * scale).astype(BF16)
    k_e = qke[:, GLA_DK:2 * GLA_DK].astype(BF16)
    v = v_ref[0].astype(BF16)
    attn = [jnp.where(tril, _dot_nt(q_e[sl], k_e[sl]), 0.0).astype(BF16) for sl in chunks]
    o_intra = [_dot(a, v[sl]) for a, sl in zip(attn, chunks)]
    kv = [_dot_tn(v[sl], k_e[sl]) for sl in chunks]
    st = st_ref[...]
    states = []
    for n, sl in enumerate(chunks):
        states.append(st.astype(BF16))
        st = (st + kv[n]) * e[sl.stop - 1:sl.stop, 0:GLA_DK]
    st_ref[...] = st
    o = jnp.concatenate([oi + _dot_nt(q_e[sl], s_in) for oi, sl, s_in in zip(o_intra, chunks, states)], axis=0)
    y = o * lax.rsqrt(jnp.mean(o * o, axis=-1, keepdims=True) + EPS) * nw_ref[...]
    o_ref[0] = (y * _silu(z_ref[0])).astype(o_ref.dtype)


def _gla(qkg, vg, zg, misc, wa_big, ba2, norm_w, tri, T):
    B, S, _ = qkg.shape
    hb = pl.BlockSpec((1, T, LANES), lambda b, h, t: (b, t, h))
    return pl.pallas_call(
        _gla_kernel,
        grid=(B, GLA_HEADS, S // T),
        in_specs=[hb, hb, hb,
                  pl.BlockSpec((1, T, LANES), lambda b, h, t: (b, t, 0)),
                  pl.BlockSpec((1, LANES, LANES), lambda b, h, t: (h, 0, 0)),
                  pl.BlockSpec((1, 1, LANES), lambda b, h, t: (h, 0, 0)),
                  pl.BlockSpec(norm_w.shape, lambda b, h, t: (0, 0)),
                  pl.BlockSpec(tri.shape, lambda b, h, t: (0, 0))],
        out_specs=hb,
        out_shape=jax.ShapeDtypeStruct((B, S, GLA_HEADS * GLA_DV), BF16),
        scratch_shapes=[pltpu.VMEM((GLA_DV, GLA_DK), F32)],
        compiler_params=pltpu.CompilerParams(dimension_semantics=("parallel", "parallel", "arbitrary"),
                                             vmem_limit_bytes=VMEM_LIMIT),
        name="gla_chunked",
    )(qkg, vg, zg, misc, wa_big, ba2, norm_w, tri)


def _out_kernel(hn_ref, on_ref, zs_ref, og_ref, p_ref, wo_ref, wpg_ref, bpg_ref, wpe_ref, lg_ref, lb_ref, o_ref):
    sub = 256
    for c in range(hn_ref.shape[1] // sub):
        rows = slice(c * sub, (c + 1) * sub)
        o_nsa = on_ref[0, :, rows].astype(F32).T.astype(BF16)
        mix_n = o_nsa * zs_ref[0, rows, :]
        mix = jnp.concatenate([mix_n, og_ref[0, rows, :]], axis=1)
        r = hn_ref[0, rows, :] + _dot(mix, wo_ref[...])
        gate = _sigmoid(_dot(r.astype(BF16), wpg_ref[...]) + bpg_ref[...])
        r = r + gate * _dot(p_ref[0, rows, :].astype(BF16), wpe_ref[...])
        o_ref[0, rows, :] = _layer_norm(r, lg_ref[...], lb_ref[...])


def _out_projection(hn, o_nsa, zs, o_gla, p, w_out, w_pg, b_pg, w_pe, ln_g, ln_b, tm):
    B, S, D = hn.shape
    tok = lambda w: pl.BlockSpec((1, tm, w), lambda b, t: (b, t, 0))
    full2 = lambda a: pl.BlockSpec(a.shape, lambda b, t: (0, 0))
    return pl.pallas_call(
        _out_kernel,
        grid=(B, S // tm),
        in_specs=[tok(D), pl.BlockSpec((1, 512, tm), lambda b, t: (b, 0, t)), tok(512), tok(512), tok(PLE_DIM),
                  full2(w_out), full2(w_pg), full2(b_pg), full2(w_pe), full2(ln_g), full2(ln_b)],
        out_specs=tok(D),
        out_shape=jax.ShapeDtypeStruct((B, S, D), F32),
        compiler_params=pltpu.CompilerParams(dimension_semantics=("parallel", "parallel"),
                                             vmem_limit_bytes=VMEM_LIMIT),
        name="out_proj_deepnorm",
    )(hn, o_nsa, zs, o_gla, p, w_out, w_pg, b_pg, w_pe, ln_g, ln_b)


def _permute_w_in(w):
    widths = (512, 128, 128, 128, 128, 128, 128, 24, 512, 256, 256, 512, 16, 512)
    offs = np.concatenate([[0], np.cumsum(widths)])
    (q_n, kc, vc, ks, vs, kw, vw, gates, z_n, q_g, k_g, v_g, a_low, z_g) = [
        w[:, int(offs[k]):int(offs[k + 1])] for k in range(len(widths))]
    pair = lambda a, b: [jnp.concatenate([a[:, g * DH:(g + 1) * DH], b[:, g * DH:(g + 1) * DH]], axis=1)
                         for g in range(NSA_G)]
    kvc = pair(kc, vc)
    kvw = pair(kw, vw)
    qk = jnp.concatenate([jnp.concatenate([q_g[:, h * GLA_DK:(h + 1) * GLA_DK],
                                           k_g[:, h * GLA_DK:(h + 1) * GLA_DK]], axis=1)
                          for h in range(GLA_HEADS)], axis=1)
    misc = jnp.concatenate([gates, a_low,
                            jnp.zeros((w.shape[0], LANES - gates.shape[1] - a_low.shape[1]), w.dtype)], axis=1)
    cols = [q_n * (DH ** -0.5), kvc[0], kvc[1], ks, vs, kvw[0], kvw[1], z_n, qk, v_g, z_g, misc]
    return jnp.concatenate(cols, axis=1).astype(BF16)


def kernel(x, p, ln0_g, ln0_b, rel_bias, w_in, w_a2, b_a, gla_norm_w, pos_cmp, w_ck1, b_ck1, w_ck2,
           w_cv1, b_cv1, w_cv2, w_out, w_pe, w_pg, b_pg, ln_g, ln_b):
    B, S, D = x.shape
    assert D == D_MODEL and S % 512 == 0 and S >= WINDOW + QB and w_in.shape[0] == DEPTH == 1
    nb = S // QB
    nch = S // CMP_STRIDE
    row = lambda a: a.reshape(1, -1)

    bias_c, bias_w = _bias_tables(rel_bias, nb, nch)

    w_perm = _permute_w_in(w_in[0])
    (hn, qn_t, xkv, ksa, vsat, kvw, vwat, zs, qkg, vg, zg, misc, sg_t) = _in_projection(
        x, row(ln0_g), row(ln0_b), w_perm, TOKEN_TILE)

    kvc, kvct = _compress(xkv, *_compress_params(pos_cmp[0], w_ck1[0], b_ck1[0], w_ck2[0],
                                                 w_cv1[0], b_cv1[0], w_cv2[0]))

    qa, part, gsel = _select(qn_t, kvw, vwat, kvc, kvct, bias_w, bias_c, sg_t)
    o_nsa = _sweep(qa, part, gsel, ksa, vsat, bias_w)

    wa = w_a2[0]
    wa_big = jnp.zeros((GLA_HEADS, LANES, LANES), F32)
    for h in range(GLA_HEADS):
        wh = wa[:, h * GLA_DK:(h + 1) * GLA_DK]
        wa_big = wa_big.at[h, MISC_ALOW:MISC_ALOW + GLA_RANK, :].set(jnp.concatenate([wh, wh], axis=1))
    ba = b_a[0].reshape(GLA_HEADS, 1, GLA_DK)
    ba2 = jnp.concatenate([ba, ba], axis=2)
    tri = jnp.asarray(np.tril(np.ones((GLA_CHUNK, GLA_CHUNK), np.float32))).astype(BF16)
    gla_tile = math.gcd(S, GLA_TILE)
    o_gla = _gla(qkg, vg, zg, misc, wa_big.astype(BF16), ba2, row(gla_norm_w[0]), tri, gla_tile)

    return _out_projection(hn, o_nsa, zs, o_gla, p[0], w_out[0].astype(BF16), w_pg[0].astype(BF16),
                           row(b_pg[0]), w_pe[0].astype(BF16), row(ln_g[0]), row(ln_b[0]), TOKEN_TILE)
```

```python
import math

import numpy as np
import jax
import jax.numpy as jnp
from jax import lax
from jax.experimental import pallas as pl
from jax.experimental.pallas import tpu as pltpu

F32 = jnp.float32
BF16 = jnp.bfloat16

D_MODEL = 1024
PLE_DIM = 256
NSA_HEADS = 8
NSA_G = 2
NSA_R = NSA_HEADS // NSA_G
DH = 64
CMP_LEN = 32
CMP_STRIDE = 16
CMP_HIDDEN = 256
SEL_BLOCK = 64
TOP_N = 16
N_LOCAL = 2
WINDOW = 512
QB = 256
GLA_HEADS = 4
GLA_DK = 64
GLA_DV = 128
GLA_RANK = 16
GLA_TAU = 16.0
GLA_CHUNK = 64
NUM_BUCKETS = 32
MAX_DISTANCE = 128
DEPTH = 1
ALPHA = (2.0 * DEPTH) ** 0.25
EPS = 1e-5
NEG = -1e30
POS = 1e30

LANES = 128
VMEM_LIMIT = 56 * 1024 * 1024

TOKEN_TILE = 512
GLA_TILE = 4096
SELECT_BLOCKS = 4
SEL_TILE = 256
SWEEP_GROUP = 16
VROWS = 80
SUPER = 64 * SEL_BLOCK
TW_WIDTH = WINDOW + QB + WINDOW
ROWS = NSA_R * QB

C_QN, C_KVC, C_KSVS, C_KVW, C_ZN, C_QKG, C_VG, C_ZG, C_MISC, C_END = (
    0, 512, 768, 1024, 1280, 1792, 2304, 2816, 3328, 3456)
MISC_GATES = 0
MISC_ALOW = 24
GATE_ROWS = 32


def _bucket_thresholds():
    d = np.arange(0, 4 * MAX_DISTANCE)
    max_exact = NUM_BUCKETS // 2
    nf = np.maximum(d, 1).astype(np.float32)
    large = max_exact + (np.log(nf / np.float32(max_exact)) / np.float32(math.log(MAX_DISTANCE / max_exact))
                         * np.float32(NUM_BUCKETS - max_exact)).astype(np.int32)
    large = np.minimum(large, NUM_BUCKETS - 1)
    bucket = np.where(d < max_exact, d, large)
    assert np.all(np.diff(bucket) >= 0) and bucket[-1] == NUM_BUCKETS - 1
    return [int(np.argmax(bucket >= k)) for k in range(NUM_BUCKETS)]


_THR = _bucket_thresholds()
CMP_BAND = 32
assert (QB - 1 + _THR[NUM_BUCKETS - 1]) // CMP_STRIDE + 1 + 7 <= CMP_BAND
assert SEL_TILE % QB == 0 and _THR[NUM_BUCKETS - 1] <= QB


def _dot(a, b, **kw):
    return jnp.dot(a, b, preferred_element_type=F32, **kw)


def _dot_nt(a, b, **kw):
    return lax.dot_general(a, b, (((1,), (1,)), ((), ())), preferred_element_type=F32, **kw)


def _dot_tn(a, b, **kw):
    return lax.dot_general(a, b, (((0,), (0,)), ((), ())), preferred_element_type=F32, **kw)


def _layer_norm(x, g, b):
    mu = jnp.mean(x, axis=-1, keepdims=True)
    xc = x - mu
    var = jnp.mean(xc * xc, axis=-1, keepdims=True)
    return xc * lax.rsqrt(var + EPS) * g + b


def _sigmoid(x):
    return 1.0 / (1.0 + jnp.exp(-x))


def _silu(x):
    return x * _sigmoid(x)


def _bias_from_dist(rb_ref, dist, valid, put):
    masks = [dist >= _THR[k] for k in range(1, NUM_BUCKETS)]
    for h in range(NSA_HEADS):
        val = jnp.full(dist.shape, rb_ref[0, h], F32)
        for k in range(1, NUM_BUCKETS):
            val = jnp.where(masks[k - 1], rb_ref[k, h], val)
        val = val - rb_ref[NUM_BUCKETS - 1, h]
        put(h, jnp.where(valid, val, NEG))


def _bias_cmp_kernel(rb_ref, out_ref):
    i = pl.program_id(0)
    nc = out_ref.shape[2]
    big = CMP_STRIDE * QB
    j_zero_max = (QB * i - (CMP_LEN - 1) - _THR[NUM_BUCKETS - 1] + CMP_STRIDE * big) // CMP_STRIDE - big
    start = jnp.clip((j_zero_max + 1 + 8 * big) // 8 * 8 - 8 * big, 0, nc - CMP_BAND)
    start = pl.multiple_of(start, 8)
    j_all = lax.broadcasted_iota(jnp.int32, (nc, QB), 0)
    fill = jnp.where(j_all <= j_zero_max, 0.0, NEG)
    for h in range(NSA_HEADS):
        out_ref[0, h] = fill
    j = start + lax.broadcasted_iota(jnp.int32, (CMP_BAND, QB), 0)
    a = lax.broadcasted_iota(jnp.int32, (CMP_BAND, QB), 1)
    dist = a + QB * i - CMP_STRIDE * j - (CMP_LEN - 1)

    def put(h, v):
        out_ref[0, h, pl.ds(start, CMP_BAND), :] = v
    _bias_from_dist(rb_ref, dist, dist >= 0, put)


def _bias_win_kernel(rb_ref, out_ref):
    shape = out_ref.shape[1:]
    y = lax.broadcasted_iota(jnp.int32, shape, 0)
    a = lax.broadcasted_iota(jnp.int32, shape, 1)
    dist = a + WINDOW - y

    def put(h, v):
        out_ref[h] = v
    _bias_from_dist(rb_ref, dist, (dist >= 0) & (dist < WINDOW), put)


def _bias_tables(rel_bias, nb, nc):
    smem = pl.BlockSpec(memory_space=pltpu.SMEM)
    bias_c = pl.pallas_call(
        _bias_cmp_kernel,
        grid=(nb,),
        in_specs=[smem],
        out_specs=pl.BlockSpec((1, NSA_HEADS, nc, QB), lambda i: (i, 0, 0, 0)),
        out_shape=jax.ShapeDtypeStruct((nb, NSA_HEADS, nc, QB), F32),
        compiler_params=pltpu.CompilerParams(dimension_semantics=("parallel",)),
        name="bias_cmp",
    )(rel_bias)
    bias_w = pl.pallas_call(
        _bias_win_kernel,
        in_specs=[smem],
        out_shape=jax.ShapeDtypeStruct((NSA_HEADS, TW_WIDTH, QB), F32),
        name="bias_win",
    )(rel_bias)
    return bias_c, bias_w


def _inproj_kernel(x_ref, g_ref, b_ref, w_ref, hn_ref, qn_ref, xkv_ref, ksa_ref, vsat_ref, kvw_ref, vwat_ref,
                   zs_ref, qkg_ref, vg_ref, zg_ref, misc_ref, sgt_ref, slab_ref):
    t = pl.program_id(1)
    tm = x_ref.shape[1]
    hf = _layer_norm(x_ref[0], g_ref[...], b_ref[...])
    hn_ref[0] = ALPHA * hf
    h = hf.astype(BF16)

    def mm(c0, c1):
        return _dot(h, w_ref[:, c0:c1])

    qn_ref[0] = mm(C_QN, C_KVC).T.astype(BF16)

    kvc = mm(C_KVC, C_KSVS)
    for g in range(NSA_G):
        slab_ref[...] = kvc[:, g * LANES:(g + 1) * LANES]
        for tok in range(CMP_STRIDE):
            rows = slab_ref[pl.ds(tok, tm // CMP_STRIDE, stride=CMP_STRIDE), :]
            xkv_ref[0, g, :, tok * LANES:(tok + 1) * LANES] = rows.astype(BF16)

    lane = lax.broadcasted_iota(jnp.int32, (tm, LANES), 1)
    row = lax.broadcasted_iota(jnp.int32, (tm, LANES), 0)
    low = lane < DH
    blk = ((t * tm + row) // SEL_BLOCK) % (SUPER // SEL_BLOCK)
    onehot = jnp.where(lane - DH == blk, 1.0, 0.0)
    ones_col = jnp.where(lane == DH, 1.0, 0.0)

    def values_t(v_low):
        return jnp.where(low, v_low, ones_col).T[0:VROWS].astype(BF16)

    ksvs = mm(C_KSVS, C_KVW)
    ks, vs = ksvs[:, 0:LANES], ksvs[:, LANES:2 * LANES]
    ksa_ref[0, 0] = jnp.where(low, ks, onehot).astype(BF16)
    ksa_ref[0, 1] = jnp.where(low, pltpu.roll(ks, DH, 1), onehot).astype(BF16)
    vsat_ref[0, 0] = values_t(vs)
    vsat_ref[0, 1] = values_t(pltpu.roll(vs, DH, 1))
    kvw = mm(C_KVW, C_ZN)
    for g in range(NSA_G):
        kvw_g = kvw[:, g * LANES:(g + 1) * LANES]
        kvw_ref[0, g] = kvw_g.astype(BF16)
        vwat_ref[0, g] = values_t(pltpu.roll(kvw_g, DH, 1))
    zs_ref[0] = _silu(mm(C_ZN, C_QKG)).astype(BF16)
    qkg_ref[0] = mm(C_QKG, C_VG)
    vg_ref[0] = mm(C_VG, C_ZG)
    zg_ref[0] = mm(C_ZG, C_MISC)
    misc = mm(C_MISC, C_END)
    misc_ref[0] = misc
    sgt_ref[0] = _sigmoid(misc).T[0:GATE_ROWS]


def _in_projection(x, ln0_g, ln0_b, w_perm, tm):
    B, S, D = x.shape
    tok = lambda w: pl.BlockSpec((1, tm, w), lambda b, t: (b, t, 0))
    tok_t = lambda w: pl.BlockSpec((1, w, tm), lambda b, t: (b, 0, t))
    grp = pl.BlockSpec((1, NSA_G, tm, LANES), lambda b, t: (b, 0, t, 0))
    grp_v = pl.BlockSpec((1, NSA_G, VROWS, tm), lambda b, t: (b, 0, 0, t))
    grp_x = pl.BlockSpec((1, NSA_G, tm // CMP_STRIDE, CMP_STRIDE * LANES), lambda b, t: (b, 0, t, 0))
    full2 = lambda a: pl.BlockSpec(a.shape, lambda b, t: (0, 0))
    sds = jax.ShapeDtypeStruct
    return pl.pallas_call(
        _inproj_kernel,
        grid=(B, S // tm),
        in_specs=[tok(D), full2(ln0_g), full2(ln0_b), full2(w_perm)],
        out_specs=[tok(D), tok_t(512), grp_x, grp, grp_v, grp, grp_v, tok(512), tok(512), tok(512), tok(512),
                   tok(LANES), tok_t(GATE_ROWS)],
        out_shape=[sds((B, S, D), F32), sds((B, 512, S), BF16),
                   sds((B, NSA_G, S // CMP_STRIDE, CMP_STRIDE * LANES), BF16),
                   sds((B, NSA_G, S, LANES), BF16), sds((B, NSA_G, VROWS, S), BF16),
                   sds((B, NSA_G, S, LANES), BF16), sds((B, NSA_G, VROWS, S), BF16),
                   sds((B, S, 512), BF16), sds((B, S, 512), F32), sds((B, S, 512), F32), sds((B, S, 512), F32),
                   sds((B, S, LANES), F32), sds((B, GATE_ROWS, S), F32)],
        scratch_shapes=[pltpu.VMEM((tm, LANES), F32)],
        compiler_params=pltpu.CompilerParams(dimension_semantics=("parallel", "parallel"),
                                             vmem_limit_bytes=VMEM_LIMIT),
        name="ln0_inproj",
    )(x, ln0_g, ln0_b, w_perm)


def _gelu_tanh(x):
    c = math.sqrt(2.0 / math.pi)
    return x * (0.5 * (1.0 + jnp.tanh(c * (x + 0.044715 * (x * x * x)))))


def _compress_kernel(x_ref, ptop_ref, pbot_ref, wtop_ref, wbot_ref, b1_ref, w2_ref, out_ref, outt_ref):
    nch = x_ref.shape[2]
    x = x_ref[0, 0]
    top = _dot(x, wtop_ref[...])
    bot = _dot(x, wbot_ref[...])
    cpos = (_dot(ptop_ref[...], wtop_ref[...]) + _dot(pbot_ref[...], wbot_ref[...]))[0:1, :]
    pre = top + pltpu.roll(bot, nch - 1, 0) + cpos + b1_ref[...]
    out = _dot(_gelu_tanh(pre).astype(BF16), w2_ref[...])
    out_ref[0, 0] = out.astype(BF16)
    outt_ref[0, 0] = out.T.astype(BF16)


def _compress(xkv, ptop, pbot, wtop, wbot, b1, w2):
    B, G, nch, width = xkv.shape
    full2 = lambda a: pl.BlockSpec(a.shape, lambda b, g: (0, 0))
    return pl.pallas_call(
        _compress_kernel,
        grid=(B, G),
        in_specs=[pl.BlockSpec((1, 1, nch, width), lambda b, g: (b, g, 0, 0)),
                  full2(ptop), full2(pbot), full2(wtop), full2(wbot), full2(b1), full2(w2)],
        out_specs=[pl.BlockSpec((1, 1, nch, LANES), lambda b, g: (b, g, 0, 0)),
                   pl.BlockSpec((1, 1, LANES, nch), lambda b, g: (b, g, 0, 0))],
        out_shape=[jax.ShapeDtypeStruct((B, G, nch, LANES), BF16),
                   jax.ShapeDtypeStruct((B, G, LANES, nch), BF16)],
        compiler_params=pltpu.CompilerParams(dimension_semantics=("parallel", "parallel"),
                                             vmem_limit_bytes=VMEM_LIMIT),
        name="kv_compress",
    )(xkv, ptop, pbot, wtop, wbot, b1, w2)


def _compress_params(pos, wk1, bk1, wk2, wv1, bv1, wv2):
    half = CMP_STRIDE * DH
    zero = jnp.zeros((CMP_STRIDE, DH, CMP_HIDDEN), F32)

    def rows(w_k, w_v):
        k_rows = jnp.concatenate([w_k.reshape(CMP_STRIDE, DH, CMP_HIDDEN), zero], axis=2)
        v_rows = jnp.concatenate([zero, w_v.reshape(CMP_STRIDE, DH, CMP_HIDDEN)], axis=2)
        return jnp.concatenate([k_rows, v_rows], axis=1).reshape(CMP_STRIDE * LANES, 2 * CMP_HIDDEN).astype(BF16)

    def pos_row(p):
        r = jnp.concatenate([p, p], axis=1).reshape(1, CMP_STRIDE * LANES)
        return jnp.broadcast_to(r, (8, CMP_STRIDE * LANES)).astype(BF16)

    zpad = jnp.zeros((CMP_HIDDEN, DH), F32)
    w2 = jnp.concatenate([jnp.concatenate([wk2, zpad], axis=1),
                          jnp.concatenate([zpad, wv2], axis=1)], axis=0).astype(BF16)
    b1 = jnp.concatenate([bk1, bv1]).reshape(1, -1)
    return (pos_row(pos[:CMP_STRIDE]), pos_row(pos[CMP_STRIDE:]), rows(wk1[:half], wv1[:half]),
            rows(wk1[half:], wv1[half:]), b1, w2)


def _heads_on_lanes(t4):
    return jnp.concatenate([t4[r] for r in range(NSA_R)], axis=1)


def _select_kernel(q_ref, kvw_ref, vwat_ref, kvc_ref, kvct_ref, tw_ref, bc_ref, sgt_ref,
                   qa_ref, part_ref, gsel_ref, ps_ref):
    g = pl.program_id(1)

    def front(u):
        i = pl.program_id(2) * SELECT_BLOCKS + u
        t0 = i * QB
        qt = q_ref[0, :, u * QB:(u + 1) * QB].astype(F32)
        q_t = jnp.concatenate([qt[r * DH:(r + 1) * DH] for r in range(NSA_R)], axis=1)
        qpad_t = jnp.concatenate([q_t, jnp.zeros((DH, ROWS), F32)], axis=0).astype(BF16)

        bc = _heads_on_lanes(bc_ref[u])
        s = _dot(kvc_ref[0, 0], qpad_t) + bc
        m = jnp.max(s, axis=0, keepdims=True)
        p = jnp.exp(s - m)
        l = jnp.sum(p, axis=0, keepdims=True)
        pn = p * jnp.where(m > 0.5 * NEG, 1.0 / l, 0.0)
        o_c = _dot(kvct_ref[0, 0], pn.astype(BF16))[DH:2 * DH]

        wlen = WINDOW + QB
        start = pl.multiple_of(jnp.maximum(t0 - WINDOW, 0), LANES)
        offw = pl.multiple_of(start - t0 + WINDOW, LANES)
        sw = _dot(kvw_ref[0, 0, pl.ds(start, wlen), :], qpad_t) + _heads_on_lanes(tw_ref[:, pl.ds(offw, wlen), :])
        mw = jnp.max(sw, axis=0, keepdims=True)
        pw = jnp.exp((sw - mw).astype(BF16))
        acc_w = _dot(vwat_ref[0, 0, :, pl.ds(start, wlen)], pw)
        o_w = acc_w[0:DH] / acc_w[DH:DH + 1]

        sg = sgt_ref[0, :, u * QB:(u + 1) * QB]

        def gate_row(c):
            rows = []
            for r in range(NSA_R):
                c0 = MISC_GATES + 3 * r + c
                c1 = c0 + 3 * NSA_R
                rows.append(jnp.where(g == 0, sg[c0:c0 + 1], sg[c1:c1 + 1]))
            return jnp.concatenate(rows, axis=1)
        partial = gate_row(0) * o_c + gate_row(2) * o_w
        gate_sel = gate_row(1)

        nc = pn.shape[0]
        nsel = nc * CMP_STRIDE // SEL_BLOCK
        per_blk = SEL_BLOCK // CMP_STRIDE
        psum = pn[:, 0:QB] + pn[:, QB:2 * QB] + pn[:, 2 * QB:3 * QB] + pn[:, 3 * QB:4 * QB]
        for c in range(QB // LANES):
            ps_ref[u, c, 0:8] = jnp.zeros((8, LANES), F32)
            ps_ref[u, c, 8:8 + nc] = psum[:, c * LANES:(c + 1) * LANES]
        tok = lambda k: jnp.concatenate(
            [ps_ref.at[u].at[c][pl.ds(8 + k, nsel, stride=per_blk), :] for c in range(QB // LANES)], axis=1)
        imp = 0.5 * (tok(-1) + tok(3)) + tok(0) + tok(1) + tok(2)
        nidx = lax.broadcasted_iota(jnp.int32, (nsel, QB), 0)
        qidx = lax.broadcasted_iota(jnp.int32, (nsel, QB), 1)
        qblk = (t0 + qidx) // SEL_BLOCK
        causal = nidx <= qblk
        forced = causal & ((nidx == 0) | (nidx >= qblk - (N_LOCAL - 1)))
        n_forced = N_LOCAL + 1
        w = jnp.where(causal & jnp.logical_not(forced), imp, -3e38)
        nf = nidx.astype(F32)
        sel = jnp.where(forced | (causal & (qblk < TOP_N)), 1.0, 0.0)
        for _ in range(min(TOP_N - n_forced, nsel)):
            mx = jnp.max(w, axis=0, keepdims=True)
            first = jnp.min(jnp.where(w == mx, nf, float(nsel)), axis=0, keepdims=True)
            pick = nf == first
            sel = jnp.where(pick, 1.0, sel)
            w = jnp.where(pick, -3e38, w)
        sel_bias = jnp.where((sel > 0.5) & causal, 0.0, NEG)

        for sup in range(qa_ref.shape[3]):
            sb = sel_bias[sup * 64:(sup + 1) * 64]
            if sb.shape[0] < 64:
                sb = jnp.concatenate([sb, jnp.full((64 - sb.shape[0], QB), NEG, F32)], axis=0)
            sb4 = jnp.concatenate([sb] * NSA_R, axis=1)
            qa_ref[0, 0, u, sup] = jnp.concatenate([q_t, sb4], axis=0).astype(BF16)
        part_ref[0, 0, u] = partial
        gsel_ref[0, 0, u] = gate_sel

    for u in range(SELECT_BLOCKS):
        front(u)


def _select(qn_t, kvw, vwat, kvc, kvct, bias_w, bias_c, sg_t):
    B, _, S = qn_t.shape
    nb = S // QB
    nc = kvc.shape[2]
    n_super = -(-S // SUPER)
    nblk = SELECT_BLOCKS
    seq = lambda a: pl.BlockSpec((1, 1) + a.shape[2:], lambda b, g, t: (b, g, 0, 0))
    per_blk = lambda *tail: pl.BlockSpec((1, 1, nblk) + tail, lambda b, g, t: (b, g, t) + (0,) * len(tail))
    sds = jax.ShapeDtypeStruct
    return pl.pallas_call(
        _select_kernel,
        grid=(B, NSA_G, nb // nblk),
        in_specs=[
            pl.BlockSpec((1, NSA_R * DH, nblk * QB), lambda b, g, t: (b, g, t)),
            seq(kvw), seq(vwat), seq(kvc), seq(kvct),
            pl.BlockSpec((NSA_R, TW_WIDTH, QB), lambda b, g, t: (g, 0, 0)),
            pl.BlockSpec((nblk, NSA_R, nc, QB), lambda b, g, t: (t, g, 0, 0)),
            pl.BlockSpec((1, GATE_ROWS, nblk * QB), lambda b, g, t: (b, 0, t)),
        ],
        out_specs=[per_blk(n_super, LANES, ROWS), per_blk(DH, ROWS), per_blk(1, ROWS)],
        out_shape=[sds((B, NSA_G, nb, n_super, LANES, ROWS), BF16),
                   sds((B, NSA_G, nb, DH, ROWS), F32),
                   sds((B, NSA_G, nb, 1, ROWS), F32)],
        scratch_shapes=[pltpu.VMEM((nblk, QB // LANES, nc + 8, LANES), F32)],
        compiler_params=pltpu.CompilerParams(dimension_semantics=("parallel", "parallel", "parallel"),
                                             vmem_limit_bytes=VMEM_LIMIT),
        name="nsa_select",
    )(qn_t, kvw, vwat, kvc, kvct, bias_w, bias_c, sg_t)


def _sweep_kernel(qa_ref, part_ref, gsel_ref, ksa_ref, vsat_ref, tw_ref, o_ref, m_ref, acc_ref, s_ref):
    i = pl.program_id(2)
    t0 = i * QB
    jd1 = (t0 + QB - 1) // SEL_TILE
    n_far = jnp.maximum(t0 - (_THR[NUM_BUCKETS - 1] - 1), 0) // SEL_TILE

    def score_tile(j, ntiles, bias_off):
        n = ntiles * SEL_TILE
        k0 = pl.multiple_of(j * SEL_TILE, SEL_TILE)
        sc = _dot(ksa_ref[0, 0, pl.ds(k0, n), :], qa_ref[0, 0, 0, j // (SUPER // SEL_TILE)])
        if bias_off is not None:
            off = pl.multiple_of(bias_off, LANES)
            sc = sc + _heads_on_lanes(tw_ref[:, pl.ds(off, n), :])
        s_ref[pl.ds(k0, n), :] = sc.astype(s_ref.dtype)
        return jnp.max(sc, axis=0, keepdims=True)

    def pv_tile(j, ntiles):
        n = ntiles * SEL_TILE
        k0 = pl.multiple_of(j * SEL_TILE, SEL_TILE)
        pt = jnp.exp(s_ref[pl.ds(k0, n), :] - m_ref[...].astype(s_ref.dtype))
        return _dot(vsat_ref[0, 0, :, pl.ds(k0, n)], pt)

    def sweep(total, block):
        n_grp = total // SWEEP_GROUP

        def body(jj, carry):
            block(SWEEP_GROUP * jj, SWEEP_GROUP)
            return carry
        lax.fori_loop(0, n_grp, body, 0)
        size = SWEEP_GROUP // 2
        while size >= 1:
            first = (total // (2 * size)) * (2 * size)

            @pl.when((total // size) % 2 == 1)
            def _(first=first, size=size):
                block(first, size)
            size //= 2

    def score_block(j, ntiles):
        mm = m_ref[...]
        for h in range(0, ntiles, 2):
            mm = jnp.maximum(mm, score_tile(j + h, min(2, ntiles - h), None))
        m_ref[...] = mm

    def pv_block(j, ntiles):
        part = None
        for h in range(0, ntiles, 2):
            pv = pv_tile(j + h, min(2, ntiles - h))
            part = pv if part is None else part + pv
        acc_ref[...] += part

    m_ref[...] = jnp.full(m_ref.shape, NEG, F32)
    sweep(n_far, score_block)
    m_lo = score_tile(n_far, 1, WINDOW - (t0 - n_far * SEL_TILE))
    m_hi = score_tile(jd1, 1, WINDOW - (t0 - jd1 * SEL_TILE))
    m_ref[...] = jnp.maximum(m_ref[...], jnp.maximum(m_lo, m_hi))

    acc_ref[...] = jnp.zeros(acc_ref.shape, F32)
    sweep(jd1 + 1, pv_block)
    acc = acc_ref[...]
    o_s = acc[0:DH] / acc[DH:DH + 1]
    out_t = part_ref[0, 0, 0] + gsel_ref[0, 0, 0] * o_s
    out_t = jnp.concatenate([out_t[:, r * QB:(r + 1) * QB] for r in range(NSA_R)], axis=0)
    o_ref[0] = out_t.astype(o_ref.dtype)


def _sweep(qa, part, gsel, ksa, vsat, bias_w):
    B, G, nb = qa.shape[:3]
    S = ksa.shape[2]
    seq = lambda a: pl.BlockSpec((1, 1) + a.shape[2:], lambda b, g, i: (b, g, 0, 0))
    blk = lambda a: pl.BlockSpec((1, 1, 1) + a.shape[3:], lambda b, g, i: (b, g, i) + (0,) * (a.ndim - 3))
    return pl.pallas_call(
        _sweep_kernel,
        grid=(B, G, nb),
        in_specs=[blk(qa), blk(part), blk(gsel), seq(ksa), seq(vsat),
                  pl.BlockSpec((NSA_R, TW_WIDTH, QB), lambda b, g, i: (g, 0, 0))],
        out_specs=pl.BlockSpec((1, NSA_R * DH, QB), lambda b, g, i: (b, g, i)),
        out_shape=jax.ShapeDtypeStruct((B, NSA_HEADS * DH, S), BF16),
        scratch_shapes=[pltpu.VMEM((1, ROWS), F32),
                        pltpu.VMEM((VROWS, ROWS), F32),
                        pltpu.VMEM((S, ROWS), BF16)],
        compiler_params=pltpu.CompilerParams(dimension_semantics=("parallel", "parallel", "arbitrary"),
                                             vmem_limit_bytes=VMEM_LIMIT),
        name="nsa_sweep",
    )(qa, part, gsel, ksa, vsat, bias_w)


def _gla_kernel(qk_ref, v_ref, z_ref, misc_ref, wa_ref, ba_ref, nw_ref, tri_ref, o_ref, st_ref):
    t = pl.program_id(2)
    T = qk_ref.shape[1]
    C = GLA_CHUNK

    @pl.when(t == 0)
    def _():
        st_ref[...] = jnp.zeros(st_ref.shape, F32)

    lane = lax.broadcasted_iota(jnp.int32, (1, LANES), 1)
    sign = jnp.where(lane < GLA_DK, 1.0 / GLA_TAU, -1.0 / GLA_TAU)
    zz = _dot(misc_ref[0].astype(BF16), wa_ref[0]) + ba_ref[0]
    log_sig = jnp.minimum(zz, 0.0) - jnp.log(1.0 + jnp.exp(-jnp.abs(zz)))
    la2 = log_sig * sign
    rr = lax.broadcasted_iota(jnp.int32, (C, C), 0)
    cc = lax.broadcasted_iota(jnp.int32, (C, C), 1)
    tril = rr >= cc
    scale = GLA_DK ** -0.5

    chunks = [slice(n * C, (n + 1) * C) for n in range(T // C)]
    hi = la2.astype(BF16)
    rem = la2 - hi.astype(F32)
    mid = rem.astype(BF16)
    lo = (rem - mid.astype(F32)).astype(BF16)
    tri = tri_ref[...]
    b2 = jnp.concatenate([_dot(tri, hi[sl]) + _dot(tri, mid[sl]) + _dot(tri, lo[sl]) for sl in chunks], axis=0)
    e = jnp.exp(b2)
    qke = qk_ref[0] * e
    q_e = (qke[:, 0:GLA_DK] * scale).astype(BF16)
    k_e = qke[:, GLA_DK:2 * GLA_DK].astype(BF16)
    v = v_ref[0].astype(BF16)
    attn = [jnp.where(tril, _dot_nt(q_e[sl], k_e[sl]), 0.0).astype(BF16) for sl in chunks]
    o_intra = [_dot(a, v[sl]) for a, sl in zip(attn, chunks)]
    kv = [_dot_tn(v[sl], k_e[sl]) for sl in chunks]
    st = st_ref[...]
    states = []
    for n, sl in enumerate(chunks):
        states.append(st.astype(BF16))
        st = (st + kv[n]) * e[sl.stop - 1:sl.stop, 0:GLA_DK]
    st_ref[...] = st
    o = jnp.concatenate([oi + _dot_nt(q_e[sl], s_in) for oi, sl, s_in in zip(o_intra, chunks, states)], axis=0)
    y = o * lax.rsqrt(jnp.mean(o * o, axis=-1, keepdims=True) + EPS) * nw_ref[...]
    o_ref[0] = (y * _silu(z_ref[0])).astype(o_ref.dtype)


def _gla(qkg, vg, zg, misc, wa_big, ba2, norm_w, tri, T):
    B, S, _ = qkg.shape
    hb = pl.BlockSpec((1, T, LANES), lambda b, h, t: (b, t, h))
    return pl.pallas_call(
        _gla_kernel,
        grid=(B, GLA_HEADS, S // T),
        in_specs=[hb, hb, hb,
                  pl.BlockSpec((1, T, LANES), lambda b, h, t: (b, t, 0)),
                  pl.BlockSpec((1, LANES, LANES), lambda b, h, t: (h, 0, 0)),
                  pl.BlockSpec((1, 1, LANES), lambda b, h, t: (h, 0, 0)),
                  pl.BlockSpec(norm_w.shape, lambda b, h, t: (0, 0)),
                  pl.BlockSpec(tri.shape, lambda b, h, t: (0, 0))],
        out_specs=hb,
        out_shape=jax.ShapeDtypeStruct((B, S, GLA_HEADS * GLA_DV), BF16),
        scratch_shapes=[pltpu.VMEM((GLA_DV, GLA_DK), F32)],
        compiler_params=pltpu.CompilerParams(dimension_semantics=("parallel", "parallel", "arbitrary"),
                                             vmem_limit_bytes=VMEM_LIMIT),
        name="gla_chunked",
    )(qkg, vg, zg, misc, wa_big, ba2, norm_w, tri)


def _out_kernel(hn_ref, on_ref, zs_ref, og_ref, p_ref, wo_ref, wpg_ref, bpg_ref, wpe_ref, lg_ref, lb_ref, o_ref):
    sub = 256
    for c in range(hn_ref.shape[1] // sub):
        rows = slice(c * sub, (c + 1) * sub)
        o_nsa = on_ref[0, :, rows].astype(F32).T.astype(BF16)
        mix_n = o_nsa * zs_ref[0, rows, :]
        mix = jnp.concatenate([mix_n, og_ref[0, rows, :]], axis=1)
        r = hn_ref[0, rows, :] + _dot(mix, wo_ref[...])
        gate = _sigmoid(_dot(r.astype(BF16), wpg_ref[...]) + bpg_ref[...])
        r = r + gate * _dot(p_ref[0, rows, :].astype(BF16), wpe_ref[...])
        o_ref[0, rows, :] = _layer_norm(r, lg_ref[...], lb_ref[...])


def _out_projection(hn, o_nsa, zs, o_gla, p, w_out, w_pg, b_pg, w_pe, ln_g, ln_b, tm):
    B, S, D = hn.shape
    tok = lambda w: pl.BlockSpec((1, tm, w), lambda b, t: (b, t, 0))
    full2 = lambda a: pl.BlockSpec(a.shape, lambda b, t: (0, 0))
    return pl.pallas_call(
        _out_kernel,
        grid=(B, S // tm),
        in_specs=[tok(D), pl.BlockSpec((1, 512, tm), lambda b, t: (b, 0, t)), tok(512), tok(512), tok(PLE_DIM),
                  full2(w_out), full2(w_pg), full2(b_pg), full2(w_pe), full2(ln_g), full2(ln_b)],
        out_specs=tok(D),
        out_shape=jax.ShapeDtypeStruct((B, S, D), F32),
        compiler_params=pltpu.CompilerParams(dimension_semantics=("parallel", "parallel"),
                                             vmem_limit_bytes=VMEM_LIMIT),
        name="out_proj_deepnorm",
    )(hn, o_nsa, zs, o_gla, p, w_out, w_pg, b_pg, w_pe, ln_g, ln_b)


def _permute_w_in(w):
    widths = (512, 128, 128, 128, 128, 128, 128, 24, 512, 256, 256, 512, 16, 512)
    offs = np.concatenate([[0], np.cumsum(widths)])
    (q_n, kc, vc, ks, vs, kw, vw, gates, z_n, q_g, k_g, v_g, a_low, z_g) = [
        w[:, int(offs[k]):int(offs[k + 1])] for k in range(len(widths))]
    pair = lambda a, b: [jnp.concatenate([a[:, g * DH:(g + 1) * DH], b[:, g * DH:(g + 1) * DH]], axis=1)
                         for g in range(NSA_G)]
    kvc = pair(kc, vc)
    kvw = pair(kw, vw)
    qk = jnp.concatenate([jnp.concatenate([q_g[:, h * GLA_DK:(h + 1) * GLA_DK],
                                           k_g[:, h * GLA_DK:(h + 1) * GLA_DK]], axis=1)
                          for h in range(GLA_HEADS)], axis=1)
    misc = jnp.concatenate([gates, a_low,
                            jnp.zeros((w.shape[0], LANES - gates.shape[1] - a_low.shape[1]), w.dtype)], axis=1)
    cols = [q_n * (DH ** -0.5), kvc[0], kvc[1], ks, vs, kvw[0], kvw[1], z_n, qk, v_g, z_g, misc]
    return jnp.concatenate(cols, axis=1).astype(BF16)


def kernel(x, p, ln0_g, ln0_b, rel_bias, w_in, w_a2, b_a, gla_norm_w, pos_cmp, w_ck1, b_ck1, w_ck2,
           w_cv1, b_cv1, w_cv2, w_out, w_pe, w_pg, b_pg, ln_g, ln_b):
    B, S, D = x.shape
    assert D == D_MODEL and S % 512 == 0 and S >= WINDOW + QB and w_in.shape[0] == DEPTH == 1
    nb = S // QB
    assert S % QB == 0 and nb % SELECT_BLOCKS == 0
    nch = S // CMP_STRIDE
    row = lambda a: a.reshape(1, -1)

    bias_c, bias_w = _bias_tables(rel_bias, nb, nch)

    w_perm = _permute_w_in(w_in[0])
    (hn, qn_t, xkv, ksa, vsat, kvw, vwat, zs, qkg, vg, zg, misc, sg_t) = _in_projection(
        x, row(ln0_g), row(ln0_b), w_perm, TOKEN_TILE)

    kvc, kvct = _compress(xkv, *_compress_params(pos_cmp[0], w_ck1[0], b_ck1[0], w_ck2[0],
                                                 w_cv1[0], b_cv1[0], w_cv2[0]))

    qa, part, gsel = _select(qn_t, kvw, vwat, kvc, kvct, bias_w, bias_c, sg_t)
    o_nsa = _sweep(qa, part, gsel, ksa, vsat, bias_w)

    wa = w_a2[0]
    wa_big = jnp.zeros((GLA_HEADS, LANES, LANES), F32)
    for h in range(GLA_HEADS):
        wh = wa[:, h * GLA_DK:(h + 1) * GLA_DK]
        wa_big = wa_big.at[h, MISC_ALOW:MISC_ALOW + GLA_RANK, :].set(jnp.concatenate([wh, wh], axis=1))
    ba = b_a[0].reshape(GLA_HEADS, 1, GLA_DK)
    ba2 = jnp.concatenate([ba, ba], axis=2)
    tri = jnp.asarray(np.tril(np.ones((GLA_CHUNK, GLA_CHUNK), np.float32))).astype(BF16)
    gla_tile = math.gcd(S, GLA_TILE)
    o_gla = _gla(qkg, vg, zg, misc, wa_big.astype(BF16), ba2, row(gla_norm_w[0]), tri, gla_tile)

    return _out_projection(hn, o_nsa, zs, o_gla, p[0], w_out[0].astype(BF16), w_pg[0].astype(BF16),
                           row(b_pg[0]), w_pe[0].astype(BF16), row(ln_g[0]), row(ln_b[0]), TOKEN_TILE)
```
